```python
import math
import jax, jax.numpy as jnp
from jax import lax
import numpy as np

D_MODEL = 1024
BATCH = 1
SEQ = 16384
DEPTH = 2

CTX_LEN = 256
GRID_W = 64
CHUNK = 64
CONV_W = 5
EPS = 1e-6

SSD_HEAD_DIM = 64
SSD_INNER = D_MODEL
SSD_HEADS = SSD_INNER // SSD_HEAD_DIM
SSD_GROUPS = 2
SSD_STATE = 128
SSD_CONV_DIM = SSD_INNER + 2 * SSD_GROUPS * SSD_STATE

GLA_HEADS = 4
GLA_KEY = D_MODEL // 2
GLA_VAL = D_MODEL
GLA_DK = GLA_KEY // GLA_HEADS
GLA_DV = GLA_VAL // GLA_HEADS
GLA_GATE_RANK = 16
GLA_GATE_NORM = 16.0

GDN_DK = 128
GDN_DV = 128
GDN_QK_HEADS = D_MODEL // GDN_DK
GDN_V_HEADS = 2 * GDN_QK_HEADS
GDN_KEY = GDN_QK_HEADS * GDN_DK
GDN_VAL = GDN_V_HEADS * GDN_DV
GDN_CONV_DIM = 2 * GDN_KEY + GDN_VAL

D_FF = ((8 * D_MODEL + 3 * 256 - 1) // (3 * 256)) * 256

EV_SPLITS = (SSD_INNER, SSD_CONV_DIM, SSD_HEADS, SSD_HEADS,
             GLA_KEY, GLA_KEY, GLA_VAL, GLA_VAL, GLA_GATE_RANK, GLA_GATE_RANK)
EV_IN = sum(EV_SPLITS)
EV_OUT = SSD_INNER + GLA_VAL
OD_SPLITS = (GDN_CONV_DIM, GDN_VAL, GDN_V_HEADS, GDN_V_HEADS, GDN_V_HEADS, GDN_V_HEADS)
OD_IN = sum(OD_SPLITS)
OD_OUT = GDN_VAL

kernel_name = "hybrid_ssd_gla_gdn_prefix_ctx_block"


def split_cols(y, sizes):
    idx, acc = [], 0
    for s in sizes[:-1]:
        acc += s
        idx.append(acc)
    return jnp.split(y, idx, axis=-1)


def rms_norm(x, w):
    xf = x.astype(jnp.float32)
    y = xf * lax.rsqrt(jnp.mean(xf * xf, axis=-1, keepdims=True) + EPS)
    return (y * w).astype(x.dtype)


def l2_normalize(x):
    return x * lax.rsqrt(jnp.sum(x * x, axis=-1, keepdims=True) + EPS)


def centred_conv(x, w, b=None):
    pad = CONV_W // 2
    y = lax.conv_general_dilated(x, w[:, None, :].astype(x.dtype), (1,), [(pad, pad)],
                                 dimension_numbers=('NWC', 'WIO', 'NWC'),
                                 feature_group_count=x.shape[-1])
    return y if b is None else y + b


def swiglu(h, w1, w3, w2):
    return (jax.nn.silu(h @ w1) * (h @ w3)) @ w2


def to_column_major(h):
    b, t, d = h.shape
    rows = t // GRID_W
    return h.reshape(b, rows, GRID_W, d).transpose(0, 2, 1, 3).reshape(b, t, d)


def from_column_major(h):
    b, t, d = h.shape
    rows = t // GRID_W
    return h.reshape(b, GRID_W, rows, d).transpose(0, 2, 1, 3).reshape(b, t, d)


def ssd_scan(xdt, da, bm, cm, s0):
    b, t, h, p = xdt.shape
    g, n = bm.shape[2:]
    e = h // g
    nc = t // CHUNK
    tril = jnp.tril(jnp.ones((CHUNK, CHUNK), bool))
    xc = xdt.reshape(b, nc, CHUNK, g, e, p)
    ac = jnp.cumsum(da.reshape(b, nc, CHUNK, g, e), axis=2)
    bc = bm.reshape(b, nc, CHUNK, g, n)
    cc = cm.reshape(b, nc, CHUNK, g, n)
    seg = ac[:, :, :, None] - ac[:, :, None]
    decay = jnp.exp(jnp.where(tril[:, :, None, None], seg, -jnp.inf))
    scores = jnp.einsum('bclgn,bcsgn->bclsg', cc, bc)[..., None] * decay
    y_diag = jnp.einsum('bclsge,bcsgep->bclgep', scores, xc)
    to_end = jnp.exp(ac[:, :, -1:] - ac)
    states = jnp.einsum('bclgn,bclgep->bcgepn', bc, xc * to_end[..., None])
    chunk_decay = jnp.exp(ac[:, :, -1])

    def step(s, inp):
        dec, st = inp
        return dec[..., None, None] * s + st, s

    s_final, s_start = lax.scan(step, s0, (jnp.moveaxis(chunk_decay, 1, 0), jnp.moveaxis(states, 1, 0)))
    s_start = jnp.moveaxis(s_start, 0, 1)
    y_off = jnp.einsum('bclgn,bcgepn->bclgep', cc, s_start) * jnp.exp(ac)[..., None]
    return (y_diag + y_off).reshape(b, t, h, p), s_final


def gla_scan(q, k, v, gk, s0):
    b, t, h, dk = q.shape
    dv = v.shape[-1]
    nc = t // CHUNK
    tril = jnp.tril(jnp.ones((CHUNK, CHUNK), bool))

    def chunks(a):
        return a.reshape(b, nc, CHUNK, h, a.shape[-1]).transpose(1, 0, 3, 2, 4)

    def step(s, inp):
        qc, kc, vc, gc = inp
        gcum = jnp.cumsum(gc, axis=2)
        rel = jnp.exp(jnp.where(tril[:, :, None], gcum[:, :, :, None] - gcum[:, :, None], -jnp.inf))
        att = jnp.einsum('bhld,bhlsd->bhls', qc, rel * kc[:, :, None])
        o = (jnp.einsum('bhls,bhsv->bhlv', att, vc)
             + jnp.einsum('bhld,bhdv->bhlv', qc * jnp.exp(gcum), s))
        glast = gcum[:, :, -1:]
        s = (jnp.exp(glast[:, :, 0])[..., None] * s
             + jnp.einsum('bhld,bhlv->bhdv', kc * jnp.exp(glast - gcum), vc))
        return s, o

    s_final, o = lax.scan(step, s0, (chunks(q), chunks(k), chunks(v), chunks(gk)))
    return o.transpose(1, 0, 3, 2, 4).reshape(b, t, h, dv), s_final


def gdn_scan(q, k, v, g, beta, s0):
    b, t, h, dk = q.shape
    dv = v.shape[-1]
    nc = t // CHUNK
    tril = jnp.tril(jnp.ones((CHUNK, CHUNK), bool))
    strict = jnp.tril(jnp.ones((CHUNK, CHUNK), bool), -1)
    eye = jnp.eye(CHUNK, dtype=jnp.float32)

    def chunks(a):
        return a.reshape(b, nc, CHUNK, h, a.shape[-1]).transpose(0, 1, 3, 2, 4)

    qc, kc, vc = chunks(q), chunks(k), chunks(v)
    gcum = jnp.cumsum(g.reshape(b, nc, CHUNK, h).transpose(0, 1, 3, 2), axis=-1)
    bc = beta.reshape(b, nc, CHUNK, h).transpose(0, 1, 3, 2)
    decay = jnp.exp(jnp.where(tril, gcum[..., :, None] - gcum[..., None, :], -jnp.inf))
    kb = kc * bc[..., None]
    a_mat = jnp.where(strict, jnp.einsum('bchld,bchsd->bchls', kb, kc) * decay, 0.0)
    t_inv = lax.linalg.triangular_solve(eye + a_mat, jnp.broadcast_to(eye, a_mat.shape),
                                        left_side=True, lower=True, unit_diagonal=True)
    u = t_inv @ (vc * bc[..., None])
    w = t_inv @ (kb * jnp.exp(gcum)[..., None])
    att = jnp.einsum('bchld,bchsd->bchls', qc, kc) * decay
    q_dec = qc * jnp.exp(gcum)[..., None]
    k_dec = kc * jnp.exp(gcum[..., -1:] - gcum)[..., None]
    g_last = jnp.exp(gcum[..., -1])

    def step(s, inp):
        u_c, w_c, att_c, qd_c, kd_c, gl_c = inp
        v_new = u_c - jnp.einsum('bhld,bhdv->bhlv', w_c, s)
        o = jnp.einsum('bhld,bhdv->bhlv', qd_c, s) + jnp.einsum('bhls,bhsv->bhlv', att_c, v_new)
        s = gl_c[..., None, None] * s + jnp.einsum('bhld,bhlv->bhdv', kd_c, v_new)
        return s, o

    xs = tuple(jnp.moveaxis(a, 1, 0) for a in (u, w, att, q_dec, k_dec, g_last))
    s_final, o = lax.scan(step, s0, xs)
    return o.transpose(1, 0, 3, 2, 4).reshape(b, t, h, dv), s_final


def _flip(ts):
    return tuple(jnp.flip(a, axis=1) for a in ts)


def bidirectional(core, ctx_f, lat_f, ctx_b, lat_b, s0, need_ctx):
    oc_f, sc_f = core(*ctx_f, s0)
    ol_f, _ = core(*lat_f, sc_f)
    oc_b, sc_b = core(*_flip(ctx_b), s0)
    ol_b, _ = core(*_flip(lat_b), sc_b)
    ol = ol_f + jnp.flip(ol_b, axis=1)
    oc = oc_f + jnp.flip(oc_b, axis=1) if need_ctx else None
    return oc, ol


def even_mixer(hc, hl, w_in, conv_w, conv_b, dt_bias, a_log, d_skip, ssd_nw, gate_w, gate_b, gla_nw, w_out, need_ctx):
    f32 = jnp.float32
    a_neg = -jnp.exp(a_log.astype(f32))

    def project(h):
        b, t, _ = h.shape
        z, xbc, dt_f, dt_b, q, k, v, r, gl_f, gl_b = split_cols(h @ w_in, EV_SPLITS)
        xbc = jax.nn.silu(centred_conv(xbc, conv_w, conv_b)).astype(f32)
        xs, bm, cm = split_cols(xbc, (SSD_INNER, SSD_GROUPS * SSD_STATE, SSD_GROUPS * SSD_STATE))
        xs = xs.reshape(b, t, SSD_HEADS, SSD_HEAD_DIM)
        bm = bm.reshape(b, t, SSD_GROUPS, SSD_STATE)
        cm = cm.reshape(b, t, SSD_GROUPS, SSD_STATE)
        dt_f = jax.nn.softplus(dt_f.astype(f32) + dt_bias[0])
        dt_b = jax.nn.softplus(dt_b.astype(f32) + dt_bias[1])
        ssd_f = (xs * dt_f[..., None], dt_f * a_neg[0], bm, cm)
        ssd_b = (xs * dt_b[..., None], dt_b * a_neg[1], bm, cm)
        q = q.astype(f32).reshape(b, t, GLA_HEADS, GLA_DK) * GLA_DK ** -0.5
        k = k.astype(f32).reshape(b, t, GLA_HEADS, GLA_DK)
        v = v.astype(f32).reshape(b, t, GLA_HEADS, GLA_DV)
        gk_f = (jax.nn.log_sigmoid((gl_f @ gate_w[0] + gate_b[0]).astype(f32)) / GLA_GATE_NORM).reshape(b, t, GLA_HEADS, GLA_DK)
        gk_b = (jax.nn.log_sigmoid((gl_b @ gate_w[1] + gate_b[1]).astype(f32)) / GLA_GATE_NORM).reshape(b, t, GLA_HEADS, GLA_DK)
        return ssd_f, ssd_b, (q, k, v, gk_f), (q, k, v, gk_b), (xs, z, r)

    def merge(y_ssd, y_gla, aux):
        xs, z, r = aux
        b, t = z.shape[:2]
        y = (y_ssd + d_skip[:, None] * xs).reshape(b, t, SSD_INNER) * jax.nn.silu(z)
        y = rms_norm(y.reshape(b, t, SSD_GROUPS, SSD_INNER // SSD_GROUPS),
                     ssd_nw.reshape(SSD_GROUPS, SSD_INNER // SSD_GROUPS)).reshape(b, t, SSD_INNER)
        o = rms_norm(y_gla, gla_nw).reshape(b, t, GLA_VAL) * jax.nn.silu(r)
        return jnp.concatenate([y, o], axis=-1) @ w_out

    pc, pl = project(hc), project(hl)
    b = hl.shape[0]
    s0_ssd = jnp.zeros((b, SSD_GROUPS, SSD_HEADS // SSD_GROUPS, SSD_HEAD_DIM, SSD_STATE), f32)
    ssd_c, ssd_l = bidirectional(ssd_scan, pc[0], pl[0], pc[1], pl[1], s0_ssd, need_ctx)
    s0_gla = jnp.zeros((b, GLA_HEADS, GLA_DK, GLA_DV), f32)
    gla_c, gla_l = bidirectional(gla_scan, pc[2], pl[2], pc[3], pl[3], s0_gla, need_ctx)
    out_l = merge(ssd_l, gla_l, pl[4])
    out_c = merge(ssd_c, gla_c, pc[4]) if need_ctx else None
    return out_c, out_l


def odd_mixer(hc, hl, w_in, conv_w, dt_bias, a_log, gdn_nw, w_out, need_ctx):
    f32 = jnp.float32
    a_pos = jnp.exp(a_log.astype(f32))
    rep = GDN_V_HEADS // GDN_QK_HEADS

    def project(h):
        b, t, _ = h.shape
        qkv, z, a_f, a_b, b_f, b_b = split_cols(h @ w_in, OD_SPLITS)
        qkv = jax.nn.silu(centred_conv(qkv, conv_w)).astype(f32)
        q, k, v = split_cols(qkv, (GDN_KEY, GDN_KEY, GDN_VAL))
        q = jnp.repeat(l2_normalize(q.reshape(b, t, GDN_QK_HEADS, GDN_DK)) * GDN_DK ** -0.5, rep, axis=2)
        k = jnp.repeat(l2_normalize(k.reshape(b, t, GDN_QK_HEADS, GDN_DK)), rep, axis=2)
        v = v.reshape(b, t, GDN_V_HEADS, GDN_DV)
        g_f = -a_pos[0] * jax.nn.softplus(a_f.astype(f32) + dt_bias[0])
        g_b = -a_pos[1] * jax.nn.softplus(a_b.astype(f32) + dt_bias[1])
        beta_f = jax.nn.sigmoid(b_f.astype(f32))
        beta_b = jax.nn.sigmoid(b_b.astype(f32))
        return (q, k, v, g_f, beta_f), (q, k, v, g_b, beta_b), z

    def merge(o, z):
        b, t = z.shape[:2]
        o = rms_norm(o, gdn_nw) * jax.nn.silu(z.reshape(b, t, GDN_V_HEADS, GDN_DV))
        return o.reshape(b, t, GDN_VAL) @ w_out

    pc, pl = project(hc), project(hl)
    s0 = jnp.zeros((hl.shape[0], GDN_V_HEADS, GDN_DK, GDN_DV), f32)
    oc, ol = bidirectional(gdn_scan, pc[0], pl[0], pc[1], pl[1], s0, need_ctx)
    out_l = merge(ol, pl[2])
    out_c = merge(oc, pc[2]) if need_ctx else None
    return out_c, out_l


def setup_inputs(seed: int = 0) -> dict:
    key = jax.random.key(seed)
    ks = iter(jax.random.split(key, 40))
    n_ev = (DEPTH + 1) // 2
    n_od = DEPTH // 2

    def nrm(shape, scale):
        return scale * jax.random.normal(next(ks), shape, jnp.float32)

    def gain(shape):
        return 1.0 + 0.02 * jax.random.normal(next(ks), shape, jnp.float32)

    def dt_bias(shape):
        dt = jnp.exp(jax.random.uniform(next(ks), shape, jnp.float32, math.log(1e-3), math.log(1e-1)))
        return dt + jnp.log(-jnp.expm1(-dt))

    def a_log(shape):
        return jnp.log(jax.random.uniform(next(ks), shape, jnp.float32, 1.0, 16.0))

    return {
        "x": nrm((BATCH, SEQ, D_MODEL), 1.0),
        "c": nrm((BATCH, D_MODEL), 1.0),
        "ctx": nrm((BATCH, CTX_LEN, D_MODEL), 1.0),
        "c_ctx": nrm((D_MODEL,), 1.0),
        "mod_w": nrm((DEPTH, D_MODEL, 6 * D_MODEL), 0.5 * D_MODEL ** -0.5),
        "mod_b": nrm((DEPTH, 6 * D_MODEL), 0.02),
        "norm_mix": gain((DEPTH, D_MODEL)),
        "norm_ffn": gain((DEPTH, D_MODEL)),
        "ffn_w1": nrm((DEPTH, D_MODEL, D_FF), D_MODEL ** -0.5),
        "ffn_w3": nrm((DEPTH, D_MODEL, D_FF), D_MODEL ** -0.5),
        "ffn_w2": nrm((DEPTH, D_FF, D_MODEL), D_FF ** -0.5),
        "ev_w_in": nrm((n_ev, D_MODEL, EV_IN), D_MODEL ** -0.5),
        "ssd_conv_w": nrm((n_ev, CONV_W, SSD_CONV_DIM), CONV_W ** -0.5),
        "ssd_conv_b": nrm((n_ev, SSD_CONV_DIM), 0.02),
        "ssd_dt_bias": dt_bias((n_ev, 2, SSD_HEADS)),
        "ssd_a_log": a_log((n_ev, 2, SSD_HEADS)),
        "ssd_d": gain((n_ev, SSD_HEADS)),
        "ssd_norm": gain((n_ev, SSD_INNER)),
        "gla_gate_w": nrm((n_ev, 2, GLA_GATE_RANK, GLA_KEY), GLA_GATE_RANK ** -0.5),
        "gla_gate_b": nrm((n_ev, 2, GLA_KEY), 0.02),
        "gla_norm": gain((n_ev, GLA_DV)),
        "ev_w_out": nrm((n_ev, EV_OUT, D_MODEL), EV_OUT ** -0.5),
        "od_w_in": nrm((n_od, D_MODEL, OD_IN), D_MODEL ** -0.5),
        "gdn_conv_w": nrm((n_od, CONV_W, GDN_CONV_DIM), CONV_W ** -0.5),
        "gdn_dt_bias": dt_bias((n_od, 2, GDN_V_HEADS)),
        "gdn_a_log": a_log((n_od, 2, GDN_V_HEADS)),
        "gdn_norm": gain((n_od, GDN_DV)),
        "od_w_out": nrm((n_od, OD_OUT, D_MODEL), OD_OUT ** -0.5),
        "final_norm": gain((D_MODEL,)),
    }


def reference(x, c, ctx, c_ctx, mod_w, mod_b, norm_mix, norm_ffn, ffn_w1, ffn_w3, ffn_w2,
              ev_w_in, ssd_conv_w, ssd_conv_b, ssd_dt_bias, ssd_a_log, ssd_d, ssd_norm,
              gla_gate_w, gla_gate_b, gla_norm, ev_w_out,
              od_w_in, gdn_conv_w, gdn_dt_bias, gdn_a_log, gdn_norm, od_w_out, final_norm):
    xc = ctx
    for i in range(DEPTH):
        last = i == DEPTH - 1
        j = i // 2
        sh1, sc1, g1, sh2, sc2, g2 = (m[:, None, :] for m in jnp.split(jax.nn.silu(c) @ mod_w[i] + mod_b[i], 6, axis=-1))
        ch1, cs1, cg1, ch2, cs2, cg2 = jnp.split(jax.nn.silu(c_ctx) @ mod_w[i] + mod_b[i], 6, axis=-1)
        hl = rms_norm(x, norm_mix[i]) * (1.0 + sc1) + sh1
        hc = rms_norm(xc, norm_mix[i]) * (1.0 + cs1) + ch1
        if i % 2 == 0:
            yc, yl = even_mixer(hc, hl, ev_w_in[j], ssd_conv_w[j], ssd_conv_b[j], ssd_dt_bias[j], ssd_a_log[j],
                                ssd_d[j], ssd_norm[j], gla_gate_w[j], gla_gate_b[j], gla_norm[j], ev_w_out[j],
                                not last)
        else:
            yc, yl = odd_mixer(hc, to_column_major(hl), od_w_in[j], gdn_conv_w[j], gdn_dt_bias[j], gdn_a_log[j],
                               gdn_norm[j], od_w_out[j], not last)
            yl = from_column_major(yl)
        x = x + g1 * yl
        x = x + g2 * swiglu(rms_norm(x, norm_ffn[i]) * (1.0 + sc2) + sh2, ffn_w1[i], ffn_w3[i], ffn_w2[i])
        if not last:
            xc = xc + cg1 * yc
            xc = xc + cg2 * swiglu(rms_norm(xc, norm_ffn[i]) * (1.0 + cs2) + ch2, ffn_w1[i], ffn_w3[i], ffn_w2[i])
    return rms_norm(x, final_norm)
```

```python
import functools

import jax
import jax.numpy as jnp
from jax import lax
from jax.experimental import pallas as pl
from jax.experimental.pallas import tpu as pltpu

F32 = jnp.float32
BF16 = jnp.bfloat16
HIGHEST = lax.Precision.HIGHEST

EPS = 1e-6
CHUNK = 64
GRID_W = 64
CONV_W = 5
ROW_TILE = 256
HALO = 8
LANES = 128
VMEM_LIMIT_BYTES = 56 * 1024 * 1024

SSD_HEADS = 16
SSD_HEAD_DIM = 64
SSD_GROUPS = 2
SSD_STATE = 128
GLA_HEADS = 4
GLA_DK = 128
GLA_DV = 256
GLA_GATE_RANK = 16
GLA_GATE_NORM = 16.0
GDN_QK_HEADS = 8
GDN_V_HEADS = 16
GDN_DK = 128
GDN_DV = 128
SMALL = 16

NT_DIMS = (((1,), (1,)), ((), ()))
TN_DIMS = (((0,), (0,)), ((), ()))


def _cparams(sem):
    return pltpu.CompilerParams(dimension_semantics=sem, vmem_limit_bytes=VMEM_LIMIT_BYTES)


def _bdot(a, b):
    return jnp.dot(a.astype(BF16), b.astype(BF16), preferred_element_type=F32)


def _bdot_nt(a, b):
    return lax.dot_general(a.astype(BF16), b.astype(BF16), NT_DIMS, preferred_element_type=F32)


def _bdot_tn(a, b):
    return lax.dot_general(a.astype(BF16), b.astype(BF16), TN_DIMS, preferred_element_type=F32)


def _split3(x):
    hi = x.astype(BF16)
    r1 = x - hi.astype(F32)
    mid = r1.astype(BF16)
    lo = (r1 - mid.astype(F32)).astype(BF16)
    return hi, mid, lo


def _sel_dot_l(sel, x):
    hi, mid, lo = _split3(x)
    d = lambda p: jnp.dot(sel, p, preferred_element_type=F32)
    return d(hi) + d(mid) + d(lo)


def _sel_dot_r(x, sel):
    hi, mid, lo = _split3(x)
    d = lambda p: jnp.dot(p, sel, preferred_element_type=F32)
    return d(hi) + d(mid) + d(lo)


def _silu(x):
    return x * jax.nn.sigmoid(x)


def _softplus(x):
    return jnp.maximum(x, 0.0) + jnp.log1p(jnp.exp(-jnp.abs(x)))


def _mod_kernel(c_ref, w_ref, b_ref, o_ref):
    s = _silu(c_ref[...])
    o_ref[0] = jnp.dot(s, w_ref[0], precision=HIGHEST, preferred_element_type=F32) + b_ref[0]


def _modulation(c, c_ctx, mod_w, mod_b):
    depth, d, n = mod_w.shape
    cs = jnp.zeros((8, d), F32).at[0].set(c[0]).at[1].set(c_ctx)
    tn = n // 4
    return pl.pallas_call(
        _mod_kernel,
        grid=(depth, n // tn),
        in_specs=[pl.BlockSpec((8, d), lambda i, j: (0, 0)),
                  pl.BlockSpec((1, d, tn), lambda i, j: (i, 0, j)),
                  pl.BlockSpec((1, 1, tn), lambda i, j: (i, 0, j))],
        out_specs=pl.BlockSpec((1, 8, tn), lambda i, j: (i, 0, j)),
        out_shape=jax.ShapeDtypeStruct((depth, 8, n), F32),
        compiler_params=_cparams(("arbitrary", "arbitrary")),
        name="modulation",
    )(cs, mod_w, mod_b.reshape(depth, 1, n))


def _mod_rows(mod_ref, is_ctx, idx, d):
    sl = slice(idx * d, (idx + 1) * d)
    return jnp.where(is_ctx, mod_ref[1:2, sl], mod_ref[0:1, sl])


def _norm_mod(x, nw, shift, scale):
    ms = jnp.mean(x * x, axis=-1, keepdims=True)
    return (x * lax.rsqrt(ms + EPS) * nw) * (1.0 + scale) + shift


def _proj_kernel(x_ref, nw_ref, mod_ref, w_ref, *out_refs, widths, n_ctx_tiles):
    d = x_ref.shape[1]
    is_ctx = pl.program_id(0) < n_ctx_tiles
    h = _norm_mod(x_ref[...], nw_ref[...], _mod_rows(mod_ref, is_ctx, 0, d), _mod_rows(mod_ref, is_ctx, 1, d))
    h = h.astype(BF16)
    off = 0
    for o_ref, wd in zip(out_refs, widths):
        o_ref[...] = jnp.dot(h, w_ref[:, off:off + wd], preferred_element_type=F32)
        off += wd


def _norm_proj(x_all, nw, mod, w, widths, n_ctx_tiles):
    ta, d = x_all.shape
    ntot = sum(widths)
    kern = functools.partial(_proj_kernel, widths=widths, n_ctx_tiles=n_ctx_tiles)
    return pl.pallas_call(
        kern,
        grid=(ta // ROW_TILE,),
        in_specs=[pl.BlockSpec((ROW_TILE, d), lambda i: (i, 0)),
                  pl.BlockSpec((1, d), lambda i: (0, 0)),
                  pl.BlockSpec(mod.shape, lambda i: (0, 0)),
                  pl.BlockSpec((d, ntot), lambda i: (0, 0))],
        out_specs=[pl.BlockSpec((ROW_TILE, wd), lambda i: (i, 0)) for wd in widths],
        out_shape=[jax.ShapeDtypeStruct((ta, wd), F32) for wd in widths],
        compiler_params=_cparams(("arbitrary",)),
        name="norm_proj",
    )(x_all, nw.reshape(1, d), mod, w)


def _halo_specs(width, n_tiles):
    per = ROW_TILE // HALO
    prev = pl.BlockSpec((HALO, width), lambda i: (jnp.maximum(i * per - 1, 0), 0))
    nxt = pl.BlockSpec((HALO, width), lambda i: (jnp.minimum((i + 1) * per, n_tiles * per - 1), 0))
    return prev, nxt


def _conv_silu(cur_ref, prev_ref, next_ref, w_ref, bias, ext_ref, n_tiles):
    i = pl.program_id(0)
    prev_ok = jnp.logical_and(i != 0, i != 1)
    next_ok = jnp.logical_and(i != 0, i != n_tiles - 1)
    ext_ref[0:HALO, :] = jnp.where(prev_ok, prev_ref[...], 0.0)
    ext_ref[HALO:HALO + ROW_TILE, :] = cur_ref[...]
    ext_ref[HALO + ROW_TILE:, :] = jnp.where(next_ok, next_ref[...], 0.0)
    pad = CONV_W // 2
    acc = None
    for j in range(CONV_W):
        term = ext_ref[pl.ds(HALO - pad + j, ROW_TILE), :] * w_ref[j:j + 1, :]
        acc = term if acc is None else acc + term
    if bias is not None:
        acc = acc + bias
    return _silu(acc)


def _even_prep_kernel(xbc_ref, prev_ref, next_ref, small_ref, cw_ref, cb_ref, dtb_ref, aneg_ref,
                      gw_ref, gb_ref, xs_ref, bm_ref, cm_ref, dt_ref, da_ref, gkf_ref, gkb_ref,
                      ext_ref, *, n_tiles):
    y = _conv_silu(xbc_ref, prev_ref, next_ref, cw_ref, cb_ref[...], ext_ref, n_tiles)
    inner = xs_ref.shape[1]
    gs = bm_ref.shape[1]
    xs_ref[...] = y[:, :inner]
    bm_ref[...] = y[:, inner:inner + gs]
    cm_ref[...] = y[:, inner + gs:]
    small = small_ref[...]
    dt = _softplus(small + dtb_ref[...])
    dt_ref[...] = dt
    da_ref[...] = dt * aneg_ref[...]
    for d, o_ref in enumerate((gkf_ref, gkb_ref)):
        pre = jnp.dot(small, gw_ref[d], precision=HIGHEST, preferred_element_type=F32) + gb_ref[d]
        o_ref[...] = -_softplus(-pre) * (1.0 / GLA_GATE_NORM)


def _even_prep(xbc, small, conv_w, conv_b, dt_bias, a_log, gate_w, gate_b):
    ta, cdim = xbc.shape
    n_tiles = ta // ROW_TILE
    inner = SSD_HEADS * SSD_HEAD_DIM
    gs = SSD_GROUPS * SSD_STATE
    gkey = GLA_HEADS * GLA_DK
    cw = jnp.zeros((8, cdim), F32).at[:CONV_W].set(conv_w)
    dtb = jnp.zeros((1, LANES), F32).at[0, :2 * SMALL].set(dt_bias.reshape(-1))
    aneg = jnp.zeros((1, LANES), F32).at[0, :2 * SMALL].set(-jnp.exp(a_log.reshape(-1)))
    gw = jnp.zeros((2, LANES, gkey), F32)
    gw = gw.at[0, 2 * SMALL:3 * SMALL].set(gate_w[0]).at[1, 3 * SMALL:4 * SMALL].set(gate_w[1])
    prev_spec, next_spec = _halo_specs(cdim, n_tiles)
    row = lambda w: pl.BlockSpec((ROW_TILE, w), lambda i: (i, 0))
    full = lambda a: pl.BlockSpec(a.shape, lambda i: (0,) * a.ndim)
    gb = gate_b.reshape(2, 1, gkey)
    return pl.pallas_call(
        functools.partial(_even_prep_kernel, n_tiles=n_tiles),
        grid=(n_tiles,),
        in_specs=[row(cdim), prev_spec, next_spec, row(LANES), full(cw), pl.BlockSpec((1, cdim), lambda i: (0, 0)),
                  full(dtb), full(aneg), full(gw), full(gb)],
        out_specs=[row(inner), row(gs), row(gs), row(LANES), row(LANES), row(gkey), row(gkey)],
        out_shape=[jax.ShapeDtypeStruct((ta, w), F32) for w in (inner, gs, gs, LANES, LANES, gkey, gkey)],
        scratch_shapes=[pltpu.VMEM((ROW_TILE + 2 * HALO, cdim), F32)],
        compiler_params=_cparams(("arbitrary",)),
        name="even_prep",
    )(xbc, xbc, xbc, small, cw, conv_b.reshape(1, cdim), dtb, aneg, gw, gb)


def _scan_consts(rev, heads, lanes_per_head):
    r = jnp.arange(CHUNK)
    tri = (r[:, None] <= r[None, :]) if rev else (r[:, None] >= r[None, :])
    strict = (r[:, None] < r[None, :]) if rev else (r[:, None] > r[None, :])
    eye = r[:, None] == r[None, :]
    expand = (jnp.arange(heads)[:, None] == (jnp.arange(heads * lanes_per_head)[None, :] // lanes_per_head))
    return dict(
        tri=tri.astype(BF16),
        ones=jnp.ones((CHUNK, CHUNK), BF16),
        expand=expand.astype(BF16),
        tri_t=jnp.tile(tri, (1, heads)).astype(F32),
        strict_t=jnp.tile(strict, (1, heads)).astype(F32),
        eye_t=jnp.tile(eye, (1, heads)).astype(F32),
    )


def _pair_mask():
    r = jnp.arange(2 * CHUNK)
    return ((r[:, None] // CHUNK) == (r[None, :] // CHUNK)).astype(F32)


def _level_masks():
    r = jnp.arange(2 * CHUNK)
    same = lambda b: (r[:, None] // b) == (r[None, :] // b)
    sizes = [2 ** m for m in range(CHUNK.bit_length() - 1)]
    return jnp.stack([jnp.logical_and(same(2 * b), jnp.logical_not(same(b))) for b in sizes]).astype(F32)


def _chunk_map(rev, n_chunks, n_ctx_chunks):
    if not rev:
        return lambda i: (i, 0)
    return lambda i: (jnp.where(i < n_ctx_chunks, n_ctx_chunks - 1 - i, n_chunks + n_ctx_chunks - 1 - i), 0)


def _decay_tables(tri_ref, ones_ref, expand_ref, eye_ref, trit_ref, g):
    cum = _sel_dot_l(tri_ref[...], g)
    col = _sel_dot_r(cum, expand_ref[...])
    rowf = _sel_dot_l(ones_ref[...], col * eye_ref[...])
    decay = jnp.where(trit_ref[...] > 0.5, jnp.exp(col - rowf), 0.0)
    return cum, col, decay


def _ssd_kernel(xs_ref, bm_ref, cm_ref, dt_ref, da_ref, tri_ref, ones_ref, expand_ref, eye_ref, trit_ref,
                pair_ref, y_ref, st_ref, *, d, last_row):
    @pl.when(pl.program_id(0) == 0)
    def _():
        st_ref[...] = jnp.zeros(st_ref.shape, F32)

    dt = dt_ref[:, d * SMALL:(d + 1) * SMALL]
    da = da_ref[:, d * SMALL:(d + 1) * SMALL]
    _, col, decay = _decay_tables(tri_ref, ones_ref, expand_ref, eye_ref, trit_ref, da)
    xdt = xs_ref[...] * _sel_dot_r(dt, expand_ref[...])
    last = col[last_row:last_row + 1, :]
    xw = xdt * jnp.exp(last - col)
    chunk_decay = jnp.exp(last)
    eac = jnp.exp(col)
    pair = pair_ref[...]
    gw = (SSD_HEADS // SSD_GROUPS) * SSD_HEAD_DIM
    outs = []
    for g in range(SSD_GROUPS):
        bg = bm_ref[:, g * SSD_STATE:(g + 1) * SSD_STATE].astype(BF16)
        cg = cm_ref[:, g * SSD_STATE:(g + 1) * SSD_STATE].astype(BF16)
        lanes = slice(g * gw, (g + 1) * gw)
        brep = jnp.concatenate([bg] * (SSD_HEADS // SSD_GROUPS), axis=0)
        wm = lax.dot_general(cg, brep, NT_DIMS, preferred_element_type=F32) * decay[:, lanes]
        xg = xdt[:, lanes]
        diag = []
        for j in range(gw // LANES):
            xp = xg[:, j * LANES:(j + 1) * LANES]
            blockdiag = jnp.concatenate([xp, xp], axis=0) * pair
            diag.append(_bdot(wm[:, j * LANES:(j + 1) * LANES], blockdiag))
        state = st_ref[g]
        off = jnp.dot(cg, state.astype(BF16), preferred_element_type=F32) * eac[:, lanes]
        outs.append(jnp.concatenate(diag, axis=1) + off)
        st_ref[g] = state * chunk_decay[:, lanes] + lax.dot_general(
            bg, xw[:, lanes].astype(BF16), TN_DIMS, preferred_element_type=F32)
    y_ref[...] = jnp.concatenate(outs, axis=1)


def _ssd_scan(xs, bm, cm, dt, da, n_ctx_chunks, rev):
    ta, inner = xs.shape
    nc = ta // CHUNK
    cmap = _chunk_map(rev, nc, n_ctx_chunks)
    k = _scan_consts(rev, SSD_HEADS, SSD_HEAD_DIM)
    consts = [k["tri"], k["ones"], k["expand"], k["eye_t"], k["tri_t"], _pair_mask()]
    row = lambda w: pl.BlockSpec((CHUNK, w), cmap)
    full = lambda a: pl.BlockSpec(a.shape, lambda i: (0,) * a.ndim)
    gs = SSD_GROUPS * SSD_STATE
    return pl.pallas_call(
        functools.partial(_ssd_kernel, d=int(rev), last_row=0 if rev else CHUNK - 1),
        grid=(nc,),
        in_specs=[row(inner), row(gs), row(gs), row(LANES), row(LANES)] + [full(a) for a in consts],
        out_specs=row(inner),
        out_shape=jax.ShapeDtypeStruct((ta, inner), F32),
        scratch_shapes=[pltpu.VMEM((SSD_GROUPS, SSD_STATE, inner // SSD_GROUPS), F32)],
        compiler_params=_cparams(("arbitrary",)),
        name="ssd_scan_bwd" if rev else "ssd_scan_fwd",
    )(xs, bm, cm, dt, da, *consts)


GLA_SUB = 16


def _gla_kernel(q_ref, k_ref, v_ref, gk_ref, tri_ref, o_ref, st_ref, *, rev):
    @pl.when(pl.program_id(0) == 0)
    def _():
        st_ref[...] = jnp.zeros(st_ref.shape, F32)

    tri = tri_ref[...]
    rows = lax.broadcasted_iota(jnp.int32, (GLA_SUB, 1), 0)
    last = 0 if rev else CHUNK - 1
    for h in range(GLA_HEADS):
        ksl = slice(h * GLA_DK, (h + 1) * GLA_DK)
        vsl = slice(h * GLA_DV, (h + 1) * GLA_DV)
        gc = _sel_dot_l(tri, gk_ref[:, ksl])
        qh = q_ref[:, ksl] * (GLA_DK ** -0.5)
        kh = k_ref[:, ksl]
        vh = v_ref[:, vsl]
        state = st_ref[h]
        glast = gc[last:last + 1, :]
        inter = _bdot_nt(qh * jnp.exp(gc), state)
        st_ref[h] = state * jnp.exp(glast) + _bdot_tn(vh, kh * jnp.exp(glast - gc))
        blocks = []
        for blk in range(CHUNK // GLA_SUB):
            r0 = blk * GLA_SUB
            qi = qh[r0:r0 + GLA_SUB]
            gi = gc[r0:r0 + GLA_SUB]
            acc = inter[r0:r0 + GLA_SUB]
            if rev:
                before = slice(r0 + GLA_SUB, CHUNK)
                ref_row = r0 + GLA_SUB
            else:
                before = slice(0, r0)
                ref_row = r0 - 1
            if before.stop > before.start:
                gref = gc[ref_row:ref_row + 1]
                att = _bdot_nt(qi * jnp.exp(gi - gref), kh[before] * jnp.exp(gref - gc[before]))
                acc = acc + _bdot(att, vh[before])
            for s in range(GLA_SUB):
                rs = r0 + s
                seen = (rows <= s) if rev else (rows >= s)
                e = jnp.where(seen, jnp.exp(gi - gc[rs:rs + 1]), 0.0)
                a = jnp.sum(qi * kh[rs:rs + 1] * e, axis=-1, keepdims=True)
                acc = acc + a * vh[rs:rs + 1]
            blocks.append(acc)
        o_ref[:, vsl] = jnp.concatenate(blocks, axis=0)


def _gla_scan(q, k, v, gk, n_ctx_chunks, rev):
    ta, kdim = q.shape
    vdim = v.shape[1]
    nc = ta // CHUNK
    cmap = _chunk_map(rev, nc, n_ctx_chunks)
    tri = _scan_consts(rev, 1, 1)["tri"]
    row = lambda w: pl.BlockSpec((CHUNK, w), cmap)
    return pl.pallas_call(
        functools.partial(_gla_kernel, rev=rev),
        grid=(nc,),
        in_specs=[row(kdim), row(kdim), row(vdim), row(kdim), pl.BlockSpec(tri.shape, lambda i: (0, 0))],
        out_specs=row(vdim),
        out_shape=jax.ShapeDtypeStruct((ta, vdim), F32),
        scratch_shapes=[pltpu.VMEM((GLA_HEADS, GLA_DV, GLA_DK), F32)],
        compiler_params=_cparams(("arbitrary",)),
        name="gla_scan_bwd" if rev else "gla_scan_fwd",
    )(q, k, v, gk, tri)


def _group_rms(y, w, width):
    parts = []
    for g in range(y.shape[1] // width):
        yg = y[:, g * width:(g + 1) * width]
        ms = jnp.mean(yg * yg, axis=-1, keepdims=True)
        parts.append(yg * lax.rsqrt(ms + EPS))
    return jnp.concatenate(parts, axis=1) * w


def _even_merge_kernel(yf_ref, yb_ref, xs_ref, z_ref, of_ref, ob_ref, r_ref, dsk_ref, snw_ref, gnw_ref, w_ref, o_ref):
    inner = xs_ref.shape[1]
    y = (yf_ref[...] + yb_ref[...] + dsk_ref[...] * xs_ref[...]) * _silu(z_ref[...])
    y = _group_rms(y, snw_ref[...], inner // SSD_GROUPS)
    o = _group_rms(of_ref[...] + ob_ref[...], gnw_ref[...], GLA_DV) * _silu(r_ref[...])
    o_ref[...] = _bdot(y, w_ref[:inner, :]) + _bdot(o, w_ref[inner:, :])


def _even_merge(yf, yb, xs, z, of, ob, r, d_skip, ssd_nw, gla_nw, w_out):
    ta, inner = xs.shape
    vdim = of.shape[1]
    d = w_out.shape[1]
    dsk = jnp.repeat(d_skip, SSD_HEAD_DIM).reshape(1, inner)
    gnw = jnp.tile(gla_nw, GLA_HEADS).reshape(1, vdim)
    row = lambda w: pl.BlockSpec((ROW_TILE, w), lambda i: (i, 0))
    vec = lambda w: pl.BlockSpec((1, w), lambda i: (0, 0))
    return pl.pallas_call(
        _even_merge_kernel,
        grid=(ta // ROW_TILE,),
        in_specs=[row(inner)] * 4 + [row(vdim)] * 3 + [vec(inner), vec(inner), vec(vdim),
                                                        pl.BlockSpec(w_out.shape, lambda i: (0, 0))],
        out_specs=row(d),
        out_shape=jax.ShapeDtypeStruct((ta, d), F32),
        compiler_params=_cparams(("arbitrary",)),
        name="even_merge",
    )(yf, yb, xs, z, of, ob, r, dsk, ssd_nw.reshape(1, inner), gnw, w_out)


def _odd_prep_kernel(qkv_ref, prev_ref, next_ref, small_ref, cw_ref, dtb_ref, aneg_ref,
                     q_ref, k_ref, v_ref, g_ref, beta_ref, ext_ref, *, n_tiles):
    y = _conv_silu(qkv_ref, prev_ref, next_ref, cw_ref, None, ext_ref, n_tiles)
    kdim = q_ref.shape[1]

    def l2n(a, scale):
        parts = []
        for h in range(kdim // GDN_DK):
            ah = a[:, h * GDN_DK:(h + 1) * GDN_DK]
            parts.append(ah * (lax.rsqrt(jnp.sum(ah * ah, axis=-1, keepdims=True) + EPS) * scale))
        return jnp.concatenate(parts, axis=1)

    q_ref[...] = l2n(y[:, :kdim], GDN_DK ** -0.5)
    k_ref[...] = l2n(y[:, kdim:2 * kdim], 1.0)
    v_ref[...] = y[:, 2 * kdim:]
    small = small_ref[...]
    g_ref[...] = _softplus(small + dtb_ref[...]) * aneg_ref[...]
    beta_ref[...] = jax.nn.sigmoid(small)


def _odd_prep(qkv, small, conv_w, dt_bias, a_log):
    ta, cdim = qkv.shape
    n_tiles = ta // ROW_TILE
    kdim = GDN_QK_HEADS * GDN_DK
    vdim = GDN_V_HEADS * GDN_DV
    cw = jnp.zeros((8, cdim), F32).at[:CONV_W].set(conv_w)
    dtb = jnp.zeros((1, LANES), F32).at[0, :2 * SMALL].set(dt_bias.reshape(-1))
    aneg = jnp.zeros((1, LANES), F32).at[0, :2 * SMALL].set(-jnp.exp(a_log.reshape(-1)))
    prev_spec, next_spec = _halo_specs(cdim, n_tiles)
    row = lambda w: pl.BlockSpec((ROW_TILE, w), lambda i: (i, 0))
    full = lambda a: pl.BlockSpec(a.shape, lambda i: (0,) * a.ndim)
    return pl.pallas_call(
        functools.partial(_odd_prep_kernel, n_tiles=n_tiles),
        grid=(n_tiles,),
        in_specs=[row(cdim), prev_spec, next_spec, row(LANES), full(cw), full(dtb), full(aneg)],
        out_specs=[row(kdim), row(kdim), row(vdim), row(LANES), row(LANES)],
        out_shape=[jax.ShapeDtypeStruct((ta, w), F32) for w in (kdim, kdim, vdim, LANES, LANES)],
        scratch_shapes=[pltpu.VMEM((ROW_TILE + 2 * HALO, cdim), F32)],
        compiler_params=_cparams(("arbitrary",)),
        name="odd_prep",
    )(qkv, qkv, qkv, small, cw, dtb, aneg)


def _gdn_kernel(q_ref, k_ref, v_ref, g_ref, beta_ref, tri_ref, ones_ref, expand_ref, expandx_ref, eye_ref,
                trit_ref, strict_ref, pair_ref, lvl_ref, eye2_ref, o_ref, st_ref, *, d, last_row):
    @pl.when(pl.program_id(0) == 0)
    def _():
        st_ref[...] = jnp.zeros(st_ref.shape, F32)

    g = g_ref[:, d * SMALL:(d + 1) * SMALL]
    beta = beta_ref[:, (2 + d) * SMALL:(3 + d) * SMALL]
    cum, _, decay = _decay_tables(tri_ref, ones_ref, expand_ref, eye_ref, trit_ref, g)
    beta_t = _sel_dot_r(beta, expand_ref[...])
    a_all = beta_t * decay * strict_ref[...]
    cumx = _sel_dot_r(cum, expandx_ref[...])
    betax = _sel_dot_r(beta, expandx_ref[...])
    lastx = cumx[last_row:last_row + 1, :]
    egx = jnp.exp(cumx)
    kdx = jnp.exp(lastx - cumx)
    glx = jnp.exp(lastx)
    pair = pair_ref[...]
    for j in range(GDN_QK_HEADS):
        ksl = slice(j * GDN_DK, (j + 1) * GDN_DK)
        qj = q_ref[:, ksl]
        kj = k_ref[:, ksl]
        kb = kj.astype(BF16)
        krep = jnp.concatenate([kb, kb], axis=0)
        psl = slice(j * 2 * CHUNK, (j + 1) * 2 * CHUNK)
        kk = lax.dot_general(kb, krep, NT_DIMS, preferred_element_type=F32)
        qk = lax.dot_general(qj.astype(BF16), krep, NT_DIMS, preferred_element_type=F32)
        a_pair = kk * a_all[:, psl]
        att = qk * decay[:, psl]
        a_bd = jnp.concatenate([a_pair, a_pair], axis=0) * pair
        rhs = []
        for hh in range(2):
            hsl = slice((2 * j + hh) * GDN_DV, (2 * j + hh + 1) * GDN_DV)
            rhs.append(jnp.concatenate([v_ref[:, hsl] * betax[:, hsl], kj * betax[:, hsl] * egx[:, hsl]], axis=1))
        x = jnp.concatenate(rhs, axis=0)
        tinv = eye2_ref[...] - a_bd * lvl_ref[0]
        for lv in range(1, lvl_ref.shape[0]):
            tinv = tinv - _bdot(_bdot(tinv, a_bd * lvl_ref[lv]), tinv)
        x = _bdot(tinv, x)
        for hh in range(2):
            h = 2 * j + hh
            hsl = slice(h * GDN_DV, (h + 1) * GDN_DV)
            u = x[hh * CHUNK:(hh + 1) * CHUNK, :GDN_DV]
            w = x[hh * CHUNK:(hh + 1) * CHUNK, GDN_DV:]
            state = st_ref[h]
            sb = state.astype(BF16)
            v_new = u - jnp.dot(w.astype(BF16), sb, preferred_element_type=F32)
            o = jnp.dot((qj * egx[:, hsl]).astype(BF16), sb, preferred_element_type=F32)
            o = o + _bdot(att[:, hh * CHUNK:(hh + 1) * CHUNK], v_new)
            st_ref[h] = state * glx[:, hsl] + _bdot_tn(kj * kdx[:, hsl], v_new)
            o_ref[:, hsl] = o


def _gdn_scan(q, k, v, g, beta, n_ctx_chunks, rev):
    ta, kdim = q.shape
    vdim = v.shape[1]
    nc = ta // CHUNK
    cmap = _chunk_map(rev, nc, n_ctx_chunks)
    k64 = _scan_consts(rev, GDN_V_HEADS, CHUNK)
    expandx = _scan_consts(rev, GDN_V_HEADS, GDN_DV)["expand"]
    consts = [k64["tri"], k64["ones"], k64["expand"], expandx, k64["eye_t"], k64["tri_t"], k64["strict_t"], _pair_mask(),
              _level_masks(), jnp.eye(2 * CHUNK, dtype=F32)]
    row = lambda w: pl.BlockSpec((CHUNK, w), cmap)
    full = lambda a: pl.BlockSpec(a.shape, lambda i: (0,) * a.ndim)
    return pl.pallas_call(
        functools.partial(_gdn_kernel, d=int(rev), last_row=0 if rev else CHUNK - 1),
        grid=(nc,),
        in_specs=[row(kdim), row(kdim), row(vdim), row(LANES), row(LANES)] + [full(a) for a in consts],
        out_specs=row(vdim),
        out_shape=jax.ShapeDtypeStruct((ta, vdim), F32),
        scratch_shapes=[pltpu.VMEM((GDN_V_HEADS, GDN_DK, GDN_DV), F32)],
        compiler_params=_cparams(("arbitrary",)),
        name="gdn_scan_bwd" if rev else "gdn_scan_fwd",
    )(q, k, v, g, beta, *consts)


def _odd_merge_kernel(of_ref, ob_ref, z_ref, nw_ref, w_ref, o_ref):
    o = _group_rms(of_ref[...] + ob_ref[...], nw_ref[...], GDN_DV) * _silu(z_ref[...])
    o_ref[...] = _bdot(o, w_ref[...])


def _odd_merge(of, ob, z, gdn_nw, w_out):
    ta, vdim = of.shape
    d = w_out.shape[1]
    nw = jnp.tile(gdn_nw, GDN_V_HEADS).reshape(1, vdim)
    row = lambda w: pl.BlockSpec((ROW_TILE, w), lambda i: (i, 0))
    return pl.pallas_call(
        _odd_merge_kernel,
        grid=(ta // ROW_TILE,),
        in_specs=[row(vdim)] * 3 + [pl.BlockSpec((1, vdim), lambda i: (0, 0)), pl.BlockSpec(w_out.shape, lambda i: (0, 0))],
        out_specs=row(d),
        out_shape=jax.ShapeDtypeStruct((ta, d), F32),
        compiler_params=_cparams(("arbitrary",)),
        name="odd_merge",
    )(of, ob, z, nw, w_out)


def _ffn_kernel(x_ref, y_ref, mod_ref, nw_ref, w1_ref, w3_ref, w2_ref, fw_ref, o_ref, *, n_ctx_tiles, final):
    d = x_ref.shape[1]
    is_ctx = pl.program_id(0) < n_ctx_tiles
    x = x_ref[...] + _mod_rows(mod_ref, is_ctx, 2, d) * y_ref[...]
    h = _norm_mod(x, nw_ref[...], _mod_rows(mod_ref, is_ctx, 3, d), _mod_rows(mod_ref, is_ctx, 4, d)).astype(BF16)
    a = jnp.dot(h, w1_ref[...], preferred_element_type=F32)
    b = jnp.dot(h, w3_ref[...], preferred_element_type=F32)
    out = x + _mod_rows(mod_ref, is_ctx, 5, d) * _bdot(_silu(a) * b, w2_ref[...])
    if final:
        ms = jnp.mean(out * out, axis=-1, keepdims=True)
        out = out * lax.rsqrt(ms + EPS) * fw_ref[...]
    o_ref[...] = out


def _ffn(x_all, y_all, mod, nw, w1, w3, w2, final_w, n_ctx_tiles, final):
    ta, d = x_all.shape
    row = pl.BlockSpec((ROW_TILE, d), lambda i: (i, 0))
    full = lambda a: pl.BlockSpec(a.shape, lambda i: (0,) * a.ndim)
    nw = nw.reshape(1, d)
    fw = final_w.reshape(1, d)
    return pl.pallas_call(
        functools.partial(_ffn_kernel, n_ctx_tiles=n_ctx_tiles, final=final),
        grid=(ta // ROW_TILE,),
        in_specs=[row, row, full(mod), full(nw), full(w1), full(w3), full(w2), full(fw)],
        out_specs=row,
        out_shape=jax.ShapeDtypeStruct((ta, d), F32),
        compiler_params=_cparams(("arbitrary",)),
        name="ffn_final" if final else "ffn",
    )(x_all, y_all, mod, nw, w1, w3, w2, fw)


def _pad_cols(parts, width):
    w = jnp.concatenate(parts, axis=1)
    return jnp.pad(w, ((0, 0), (0, width - w.shape[1])))


def _even_w_in(w):
    inner = SSD_HEADS * SSD_HEAD_DIM
    conv = inner + 2 * SSD_GROUPS * SSD_STATE
    gkey = GLA_HEADS * GLA_DK
    gval = GLA_HEADS * GLA_DV
    sizes = (inner, conv, SMALL, SMALL, gkey, gkey, gval, gval, SMALL, SMALL)
    offs = [0]
    for s in sizes:
        offs.append(offs[-1] + s)
    z, xbc, dtf, dtb, q, k, v, r, glf, glb = (w[:, offs[i]:offs[i + 1]] for i in range(len(sizes)))
    small = _pad_cols([dtf, dtb, glf, glb], LANES)
    widths = (inner, conv, gkey, gkey, gval, gval, LANES)
    return jnp.concatenate([z, xbc, q, k, v, r, small], axis=1).astype(BF16), widths


def _odd_w_in(w):
    kdim = GDN_QK_HEADS * GDN_DK
    vdim = GDN_V_HEADS * GDN_DV
    conv = 2 * kdim + vdim
    small = _pad_cols([w[:, conv + vdim:]], LANES)
    widths = (conv, vdim, LANES)
    return jnp.concatenate([w[:, :conv + vdim], small], axis=1).astype(BF16), widths


def _to_column_major(h):
    t, d = h.shape
    return h.reshape(t // GRID_W, GRID_W, d).transpose(1, 0, 2).reshape(t, d)


def _from_column_major(h):
    t, d = h.shape
    return h.reshape(GRID_W, t // GRID_W, d).transpose(1, 0, 2).reshape(t, d)


def kernel(x, c, ctx, c_ctx, mod_w, mod_b, norm_mix, norm_ffn, ffn_w1, ffn_w3, ffn_w2, ev_w_in, ssd_conv_w, ssd_conv_b, ssd_dt_bias, ssd_a_log, ssd_d, ssd_norm, gla_gate_w, gla_gate_b, gla_norm, ev_w_out, od_w_in, gdn_conv_w, gdn_dt_bias, gdn_a_log, gdn_norm, od_w_out, final_norm):
    batch, seq, d = x.shape
    n_ctx = ctx.shape[1]
    assert batch == 1 and n_ctx == ROW_TILE and seq % ROW_TILE == 0 and seq % GRID_W == 0
    assert mod_w.shape[0] == 2, "one even (SSD + GLA) layer followed by one odd (gated DeltaNet) layer"
    n_ctx_tiles = n_ctx // ROW_TILE
    n_ctx_chunks = n_ctx // CHUNK

    mods = _modulation(c, c_ctx, mod_w, mod_b)
    x_all = jnp.concatenate([ctx[0], x[0]], axis=0)

    w_in, widths = _even_w_in(ev_w_in[0])
    z, xbc, q, k, v, r, small = _norm_proj(x_all, norm_mix[0], mods[0], w_in, widths, n_ctx_tiles)
    xs, bm, cm, dt, da, gk_f, gk_b = _even_prep(xbc, small, ssd_conv_w[0], ssd_conv_b[0], ssd_dt_bias[0],
                                                ssd_a_log[0], gla_gate_w[0], gla_gate_b[0])
    y_f = _ssd_scan(xs, bm, cm, dt, da, n_ctx_chunks, False)
    y_b = _ssd_scan(xs, bm, cm, dt, da, n_ctx_chunks, True)
    o_f = _gla_scan(q, k, v, gk_f, n_ctx_chunks, False)
    o_b = _gla_scan(q, k, v, gk_b, n_ctx_chunks, True)
    mixed = _even_merge(y_f, y_b, xs, z, o_f, o_b, r, ssd_d[0], ssd_norm[0], gla_norm[0], ev_w_out[0].astype(BF16))
    x_all = _ffn(x_all, mixed, mods[0], norm_ffn[0], ffn_w1[0].astype(BF16), ffn_w3[0].astype(BF16),
                 ffn_w2[0].astype(BF16), final_norm, n_ctx_tiles, False)

    x_lat = x_all[n_ctx:]
    x_perm = jnp.concatenate([x_all[:n_ctx], _to_column_major(x_lat)], axis=0)
    w_in, widths = _odd_w_in(od_w_in[0])
    qkv, z, small = _norm_proj(x_perm, norm_mix[1], mods[1], w_in, widths, n_ctx_tiles)
    q, k, v, g, beta = _odd_prep(qkv, small, gdn_conv_w[0], gdn_dt_bias[0], gdn_a_log[0])
    o_f = _gdn_scan(q, k, v, g, beta, n_ctx_chunks, False)
    o_b = _gdn_scan(q, k, v, g, beta, n_ctx_chunks, True)
    mixed = _odd_merge(o_f, o_b, z, gdn_norm[0], od_w_out[0].astype(BF16))
    mixed = _from_column_major(mixed[n_ctx:])
    out = _ffn(x_lat, mixed, mods[1], norm_ffn[1], ffn_w1[1].astype(BF16), ffn_w3[1].astype(BF16),
               ffn_w2[1].astype(BF16), final_norm, 0, True)
    return out[None]
```

```python
import functools

import jax
import jax.numpy as jnp
from jax import lax
from jax.experimental import pallas as pl
from jax.experimental.pallas import tpu as pltpu

F32 = jnp.float32
BF16 = jnp.bfloat16
HIGHEST = lax.Precision.HIGHEST

EPS = 1e-6
CHUNK = 64
GRID_W = 64
CONV_W = 5
ROW_TILE = 256
HALO = 8
LANES = 128
VMEM_LIMIT_BYTES = 56 * 1024 * 1024

SSD_HEADS = 16
SSD_HEAD_DIM = 64
SSD_GROUPS = 2
SSD_STATE = 128
GLA_HEADS = 4
GLA_DK = 128
GLA_DV = 256
GLA_GATE_RANK = 16
GLA_GATE_NORM = 16.0
GDN_QK_HEADS = 8
GDN_V_HEADS = 16
GDN_DK = 128
GDN_DV = 128
SMALL = 16

NT_DIMS = (((1,), (1,)), ((), ()))
TN_DIMS = (((0,), (0,)), ((), ()))


def _cparams(sem):
    return pltpu.CompilerParams(dimension_semantics=sem, vmem_limit_bytes=VMEM_LIMIT_BYTES)


def _bdot(a, b):
    return jnp.dot(a.astype(BF16), b.astype(BF16), preferred_element_type=F32)


def _bdot_nt(a, b):
    return lax.dot_general(a.astype(BF16), b.astype(BF16), NT_DIMS, preferred_element_type=F32)


def _bdot_tn(a, b):
    return lax.dot_general(a.astype(BF16), b.astype(BF16), TN_DIMS, preferred_element_type=F32)


def _split3(x):
    hi = x.astype(BF16)
    r1 = x - hi.astype(F32)
    mid = r1.astype(BF16)
    lo = (r1 - mid.astype(F32)).astype(BF16)
    return hi, mid, lo


def _sel_dot_l(sel3, x):
    return jnp.dot(sel3, jnp.concatenate(_split3(x), axis=0), preferred_element_type=F32)


def _sel_dot_r(x, sel3):
    pieces = jnp.concatenate([p.astype(F32) for p in _split3(x)], axis=1)
    return jnp.dot(pieces.astype(BF16), sel3, preferred_element_type=F32)


def _silu(x):
    return x * jax.nn.sigmoid(x)


def _softplus(x):
    return jnp.maximum(x, 0.0) + jnp.log1p(jnp.exp(-jnp.abs(x)))


def _mod_kernel(c_ref, w_ref, b_ref, o_ref):
    s = _silu(c_ref[...])
    o_ref[0] = jnp.dot(s, w_ref[0], precision=HIGHEST, preferred_element_type=F32) + b_ref[0]


def _modulation(c, c_ctx, mod_w, mod_b):
    depth, d, n = mod_w.shape
    cs = jnp.zeros((8, d), F32).at[0].set(c[0]).at[1].set(c_ctx)
    tn = n // 4
    return pl.pallas_call(
        _mod_kernel,
        grid=(depth, n // tn),
        in_specs=[pl.BlockSpec((8, d), lambda i, j: (0, 0)),
                  pl.BlockSpec((1, d, tn), lambda i, j: (i, 0, j)),
                  pl.BlockSpec((1, 1, tn), lambda i, j: (i, 0, j))],
        out_specs=pl.BlockSpec((1, 8, tn), lambda i, j: (i, 0, j)),
        out_shape=jax.ShapeDtypeStruct((depth, 8, n), F32),
        compiler_params=_cparams(("arbitrary", "arbitrary")),
        name="modulation",
    )(cs, mod_w, mod_b.reshape(depth, 1, n))


def _mod_rows(mod_ref, is_ctx, idx, d):
    sl = slice(idx * d, (idx + 1) * d)
    return jnp.where(is_ctx, mod_ref[1:2, sl], mod_ref[0:1, sl])


def _norm_mod(x, nw, shift, scale):
    ms = jnp.mean(x * x, axis=-1, keepdims=True)
    return (x * lax.rsqrt(ms + EPS) * nw) * (1.0 + scale) + shift


def _proj_kernel(x_ref, nw_ref, mod_ref, w_ref, *out_refs, widths, n_ctx_tiles):
    d = x_ref.shape[1]
    is_ctx = pl.program_id(0) < n_ctx_tiles
    h = _norm_mod(x_ref[...], nw_ref[...], _mod_rows(mod_ref, is_ctx, 0, d), _mod_rows(mod_ref, is_ctx, 1, d))
    h = h.astype(BF16)
    off = 0
    for o_ref, wd in zip(out_refs, widths):
        o_ref[...] = jnp.dot(h, w_ref[:, off:off + wd], preferred_element_type=F32)
        off += wd


def _norm_proj(x_all, nw, mod, w, widths, n_ctx_tiles):
    ta, d = x_all.shape
    ntot = sum(widths)
    kern = functools.partial(_proj_kernel, widths=widths, n_ctx_tiles=n_ctx_tiles)
    return pl.pallas_call(
        kern,
        grid=(ta // ROW_TILE,),
        in_specs=[pl.BlockSpec((ROW_TILE, d), lambda i: (i, 0)),
                  pl.BlockSpec((1, d), lambda i: (0, 0)),
                  pl.BlockSpec(mod.shape, lambda i: (0, 0)),
                  pl.BlockSpec((d, ntot), lambda i: (0, 0))],
        out_specs=[pl.BlockSpec((ROW_TILE, wd), lambda i: (i, 0)) for wd in widths],
        out_shape=[jax.ShapeDtypeStruct((ta, wd), F32) for wd in widths],
        compiler_params=_cparams(("arbitrary",)),
        name="norm_proj",
    )(x_all, nw.reshape(1, d), mod, w)


def _halo_specs(width, n_tiles):
    per = ROW_TILE // HALO
    prev = pl.BlockSpec((HALO, width), lambda i: (jnp.maximum(i * per - 1, 0), 0))
    nxt = pl.BlockSpec((HALO, width), lambda i: (jnp.minimum((i + 1) * per, n_tiles * per - 1), 0))
    return prev, nxt


def _conv_silu(cur_ref, prev_ref, next_ref, w_ref, bias, ext_ref, n_tiles):
    i = pl.program_id(0)
    prev_ok = jnp.logical_and(i != 0, i != 1)
    next_ok = jnp.logical_and(i != 0, i != n_tiles - 1)
    ext_ref[0:HALO, :] = jnp.where(prev_ok, prev_ref[...], 0.0)
    ext_ref[HALO:HALO + ROW_TILE, :] = cur_ref[...]
    ext_ref[HALO + ROW_TILE:, :] = jnp.where(next_ok, next_ref[...], 0.0)
    pad = CONV_W // 2
    acc = None
    for j in range(CONV_W):
        term = ext_ref[pl.ds(HALO - pad + j, ROW_TILE), :] * w_ref[j:j + 1, :]
        acc = term if acc is None else acc + term
    if bias is not None:
        acc = acc + bias
    return _silu(acc)


def _even_prep_kernel(xbc_ref, prev_ref, next_ref, small_ref, cw_ref, cb_ref, dtb_ref, aneg_ref,
                      gw_ref, gb_ref, xs_ref, bm_ref, cm_ref, dt_ref, da_ref, gkf_ref, gkb_ref,
                      ext_ref, *, n_tiles):
    y = _conv_silu(xbc_ref, prev_ref, next_ref, cw_ref, cb_ref[...], ext_ref, n_tiles)
    inner = xs_ref.shape[1]
    gs = bm_ref.shape[1]
    xs_ref[...] = y[:, :inner]
    bm_ref[...] = y[:, inner:inner + gs]
    cm_ref[...] = y[:, inner + gs:]
    small = small_ref[...]
    dt = _softplus(small + dtb_ref[...])
    dt_ref[...] = dt
    da_ref[...] = dt * aneg_ref[...]
    for d, o_ref in enumerate((gkf_ref, gkb_ref)):
        pre = jnp.dot(small, gw_ref[d], precision=HIGHEST, preferred_element_type=F32) + gb_ref[d]
        o_ref[...] = -_softplus(-pre) * (1.0 / GLA_GATE_NORM)


def _even_prep(xbc, small, conv_w, conv_b, dt_bias, a_log, gate_w, gate_b):
    ta, cdim = xbc.shape
    n_tiles = ta // ROW_TILE
    inner = SSD_HEADS * SSD_HEAD_DIM
    gs = SSD_GROUPS * SSD_STATE
    gkey = GLA_HEADS * GLA_DK
    cw = jnp.zeros((8, cdim), F32).at[:CONV_W].set(conv_w)
    dtb = jnp.zeros((1, LANES), F32).at[0, :2 * SMALL].set(dt_bias.reshape(-1))
    aneg = jnp.zeros((1, LANES), F32).at[0, :2 * SMALL].set(-jnp.exp(a_log.reshape(-1)))
    gw = jnp.zeros((2, LANES, gkey), F32)
    gw = gw.at[0, 2 * SMALL:3 * SMALL].set(gate_w[0]).at[1, 3 * SMALL:4 * SMALL].set(gate_w[1])
    prev_spec, next_spec = _halo_specs(cdim, n_tiles)
    row = lambda w: pl.BlockSpec((ROW_TILE, w), lambda i: (i, 0))
    full = lambda a: pl.BlockSpec(a.shape, lambda i: (0,) * a.ndim)
    gb = gate_b.reshape(2, 1, gkey)
    return pl.pallas_call(
        functools.partial(_even_prep_kernel, n_tiles=n_tiles),
        grid=(n_tiles,),
        in_specs=[row(cdim), prev_spec, next_spec, row(LANES), full(cw), pl.BlockSpec((1, cdim), lambda i: (0, 0)),
                  full(dtb), full(aneg), full(gw), full(gb)],
        out_specs=[row(inner), row(gs), row(gs), row(LANES), row(LANES), row(gkey), row(gkey)],
        out_shape=[jax.ShapeDtypeStruct((ta, w), F32) for w in (inner, gs, gs, LANES, LANES, gkey, gkey)],
        scratch_shapes=[pltpu.VMEM((ROW_TILE + 2 * HALO, cdim), F32)],
        compiler_params=_cparams(("arbitrary",)),
        name="even_prep",
    )(xbc, xbc, xbc, small, cw, conv_b.reshape(1, cdim), dtb, aneg, gw, gb)


def _scan_consts(rev, heads, lanes_per_head):
    r = jnp.arange(CHUNK)
    tri = (r[:, None] <= r[None, :]) if rev else (r[:, None] >= r[None, :])
    strict = (r[:, None] < r[None, :]) if rev else (r[:, None] > r[None, :])
    eye = r[:, None] == r[None, :]
    expand = (jnp.arange(heads)[:, None] == (jnp.arange(heads * lanes_per_head)[None, :] // lanes_per_head))
    return dict(
        tri3=jnp.tile(tri, (1, 3)).astype(BF16),
        ones3=jnp.ones((CHUNK, 3 * CHUNK), BF16),
        expand3=jnp.tile(expand, (3, 1)).astype(BF16),
        tri_t=jnp.tile(tri, (1, heads)).astype(F32),
        strict_t=jnp.tile(strict, (1, heads)).astype(F32),
        eye_t=jnp.tile(eye, (1, heads)).astype(F32),
    )


def _pair_mask():
    r = jnp.arange(2 * CHUNK)
    return ((r[:, None] // CHUNK) == (r[None, :] // CHUNK)).astype(F32)


def _level_masks():
    r = jnp.arange(2 * CHUNK)
    same = lambda b: (r[:, None] // b) == (r[None, :] // b)
    sizes = [2 ** m for m in range(CHUNK.bit_length() - 1)]
    return jnp.stack([jnp.logical_and(same(2 * b), jnp.logical_not(same(b))) for b in sizes]).astype(F32)


def _chunk_map(rev, n_chunks, n_ctx_chunks):
    if not rev:
        return lambda i: i
    return lambda i: jnp.where(i < n_ctx_chunks, n_ctx_chunks - 1 - i, n_chunks + n_ctx_chunks - 1 - i)


def _decay_tables(tri3, ones3, expand3, eye_t, tri_t, g):
    cum = _sel_dot_l(tri3, g)
    col = _sel_dot_r(cum, expand3)
    rowf = _sel_dot_l(ones3, col * eye_t)
    decay = jnp.where(tri_t > 0.5, jnp.exp(col - rowf), 0.0)
    return col, decay


SSD_SCAN_INS = 5


def _ssd_kernel(*refs):
    ins = refs[:2 * SSD_SCAN_INS]
    tri_ref, ones_ref, expand_ref, eye_ref, trit_ref, pair_ref, yf_ref, yb_ref, st_ref = refs[2 * SSD_SCAN_INS:]

    @pl.when(pl.program_id(0) == 0)
    def _():
        st_ref[...] = jnp.zeros(st_ref.shape, F32)

    pair = pair_ref[...]
    heads_g = SSD_HEADS // SSD_GROUPS
    gw = heads_g * SSD_HEAD_DIM
    pieces = [(d, g) for d in range(2) for g in range(SSD_GROUPS)]
    gsl = lambda g: slice(g * SSD_STATE, (g + 1) * SSD_STATE)
    lanes = lambda g: slice(g * gw, (g + 1) * gw)
    tabs = []
    for d in range(2):
        xs_ref, _, _, dt_ref, da_ref = ins[SSD_SCAN_INS * d:SSD_SCAN_INS * (d + 1)]
        last_row = 0 if d else CHUNK - 1
        col, decay = _decay_tables(tri_ref[d], ones_ref[...], expand_ref[...], eye_ref[...], trit_ref[d],
                                   da_ref[:, d * SMALL:(d + 1) * SMALL])
        xdt = xs_ref[...] * _sel_dot_r(dt_ref[:, d * SMALL:(d + 1) * SMALL], expand_ref[...])
        last = col[last_row:last_row + 1, :]
        tabs.append(dict(decay=decay, xdt=xdt, xw=xdt * jnp.exp(last - col), chunk_decay=jnp.exp(last), eac=jnp.exp(col)))
    bgs = {(d, g): ins[SSD_SCAN_INS * d + 1][:, gsl(g)].astype(BF16) for d, g in pieces}
    cgs = {(d, g): ins[SSD_SCAN_INS * d + 2][:, gsl(g)].astype(BF16) for d, g in pieces}
    wms = {p: lax.dot_general(cgs[p], jnp.concatenate([bgs[p]] * heads_g, axis=0), NT_DIMS,
                              preferred_element_type=F32) * tabs[p[0]]["decay"][:, lanes(p[1])] for p in pieces}
    offs = {(d, g): jnp.dot(cgs[d, g], st_ref[d, g].astype(BF16), preferred_element_type=F32) for d, g in pieces}
    news = {(d, g): lax.dot_general(bgs[d, g], tabs[d]["xw"][:, lanes(g)].astype(BF16), TN_DIMS,
                                    preferred_element_type=F32) for d, g in pieces}
    diags = {}
    for d, g in pieces:
        xg = tabs[d]["xdt"][:, lanes(g)]
        for j in range(gw // LANES):
            xp = xg[:, j * LANES:(j + 1) * LANES]
            blockdiag = jnp.concatenate([xp, xp], axis=0) * pair
            diags[d, g, j] = _bdot(wms[d, g][:, j * LANES:(j + 1) * LANES], blockdiag)
    for d, g in pieces:
        st_ref[d, g] = st_ref[d, g] * tabs[d]["chunk_decay"][:, lanes(g)] + news[d, g]
    for d, y_ref in enumerate((yf_ref, yb_ref)):
        for g in range(SSD_GROUPS):
            diag = jnp.concatenate([diags[d, g, j] for j in range(gw // LANES)], axis=1)
            y_ref[:, lanes(g)] = diag + offs[d, g] * tabs[d]["eac"][:, lanes(g)]


def _ssd_scan(xs, bm, cm, dt, da, n_ctx_chunks):
    ta, inner = xs.shape
    nc = ta // CHUNK
    fwd = _scan_consts(False, SSD_HEADS, SSD_HEAD_DIM)
    bwd = _scan_consts(True, SSD_HEADS, SSD_HEAD_DIM)
    both = lambda name: jnp.stack([fwd[name], bwd[name]])
    consts = [both("tri3"), fwd["ones3"], fwd["expand3"], fwd["eye_t"], both("tri_t"), _pair_mask()]
    full = lambda a: pl.BlockSpec(a.shape, lambda i: (0,) * a.ndim)
    gs = SSD_GROUPS * SSD_STATE

    def row(rev, w):
        cmap = _chunk_map(rev, nc, n_ctx_chunks)
        return pl.BlockSpec((CHUNK, w), lambda i: (cmap(i), 0))

    dir_specs = lambda rev: [row(rev, inner), row(rev, gs), row(rev, gs), row(rev, LANES), row(rev, LANES)]
    return pl.pallas_call(
        _ssd_kernel,
        grid=(nc,),
        in_specs=dir_specs(False) + dir_specs(True) + [full(a) for a in consts],
        out_specs=[row(False, inner), row(True, inner)],
        out_shape=[jax.ShapeDtypeStruct((ta, inner), F32)] * 2,
        scratch_shapes=[pltpu.VMEM((2, SSD_GROUPS, SSD_STATE, inner // SSD_GROUPS), F32)],
        compiler_params=_cparams(("arbitrary",)),
        name="ssd_scan",
    )(xs, bm, cm, dt, da, xs, bm, cm, dt, da, *consts)


GLA_SUB = 8
GLA_LEVELS = (8, 16, 32)
GLA_SCAN_INS = 4


def _gla_masks():
    r = jnp.arange(CHUNK)
    same = lambda b: (r[:, None] // b) == (r[None, :] // b)
    out = []
    for rev in (False, True):
        strict = (r[:, None] < r[None, :]) if rev else (r[:, None] > r[None, :])
        lv = [same(2 * b) & ~same(b) & strict for b in GLA_LEVELS]
        within = r % GLA_SUB
        dg = []
        for s in range(GLA_SUB):
            seen = (within <= s) if rev else (within >= s)
            dg.append((r[None, :] == (r[:, None] // GLA_SUB) * GLA_SUB + s) & seen[:, None])
        out.append(jnp.stack(lv + dg))
    return jnp.stack(out).astype(F32)


def _block_rows(a, rows, b):
    return jnp.concatenate([jnp.broadcast_to(a[r:r + 1, :], (b, a.shape[1])) for r in rows], axis=0)


def _gla_kernel(*refs):
    ins = refs[:2 * GLA_SCAN_INS]
    tri_ref, mask_ref, of_ref, ob_ref, st_ref = refs[2 * GLA_SCAN_INS:]

    @pl.when(pl.program_id(0) == 0)
    def _():
        st_ref[...] = jnp.zeros(st_ref.shape, F32)

    pieces = [(d, h) for d in range(2) for h in range(GLA_HEADS)]
    ksl = lambda h: slice(h * GLA_DK, (h + 1) * GLA_DK)
    vsl = lambda h: slice(h * GLA_DV, (h + 1) * GLA_DV)
    q_of = lambda d, h: ins[GLA_SCAN_INS * d][:, ksl(h)] * (GLA_DK ** -0.5)
    k_of = lambda d, h: ins[GLA_SCAN_INS * d + 1][:, ksl(h)]
    v_of = lambda d, h: ins[GLA_SCAN_INS * d + 2][:, vsl(h)]
    within = lax.broadcasted_iota(jnp.int32, (CHUNK, 1), 0) % GLA_SUB
    gcs = {(d, h): _sel_dot_l(tri_ref[d], ins[GLA_SCAN_INS * d + 3][:, ksl(h)]) for d, h in pieces}
    inters, news = {}, {}
    for d, h in pieces:
        gc = gcs[d, h]
        last = 0 if d else CHUNK - 1
        glast = gc[last:last + 1, :]
        inters[d, h] = _bdot_nt(q_of(d, h) * jnp.exp(gc), st_ref[d, h])
        news[d, h] = (jnp.exp(glast), _bdot_tn(v_of(d, h), k_of(d, h) * jnp.exp(glast - gc)))
    for d, h in pieces:
        st_ref[d, h] = st_ref[d, h] * news[d, h][0] + news[d, h][1]
    atts = {}
    for d, h in pieces:
        gc, qh, kh = gcs[d, h], q_of(d, h), k_of(d, h)
        att = None
        for li, b in enumerate(GLA_LEVELS):
            nb = CHUNK // b
            if d:
                rq = [min(b * (i + 1), CHUNK - 1) for i in range(nb)]
                rk = [b * i for i in range(nb)]
            else:
                rq = [max(b * i - 1, 0) for i in range(nb)]
                rk = [b * (i + 1) - 1 for i in range(nb)]
            qs = qh * jnp.exp(gc - _block_rows(gc, rq, b))
            ks = kh * jnp.exp(_block_rows(gc, rk, b) - gc)
            part = _bdot_nt(qs, ks) * mask_ref[d, li]
            att = part if att is None else att + part
        for s in range(GLA_SUB):
            rows = [GLA_SUB * i + s for i in range(CHUNK // GLA_SUB)]
            seen = (within <= s) if d else (within >= s)
            e = jnp.where(seen, jnp.exp(gc - _block_rows(gc, rows, GLA_SUB)), 0.0)
            a = jnp.sum(qh * _block_rows(kh, rows, GLA_SUB) * e, axis=-1, keepdims=True)
            att = att + a * mask_ref[d, len(GLA_LEVELS) + s]
        atts[d, h] = att
    for d, o_ref in enumerate((of_ref, ob_ref)):
        for h in range(GLA_HEADS):
            o_ref[:, vsl(h)] = inters[d, h] + _bdot(atts[d, h], v_of(d, h))


def _gla_scan(q, k, v, gk_f, gk_b, n_ctx_chunks):
    ta, kdim = q.shape
    vdim = v.shape[1]
    nc = ta // CHUNK
    tri3 = jnp.stack([_scan_consts(rev, 1, 1)["tri3"] for rev in (False, True)])
    masks = _gla_masks()
    full = lambda a: pl.BlockSpec(a.shape, lambda i: (0,) * a.ndim)

    def row(rev, w):
        cmap = _chunk_map(rev, nc, n_ctx_chunks)
        return pl.BlockSpec((CHUNK, w), lambda i: (cmap(i), 0))

    dir_specs = lambda rev: [row(rev, kdim), row(rev, kdim), row(rev, vdim), row(rev, kdim)]
    return pl.pallas_call(
        _gla_kernel,
        grid=(nc,),
        in_specs=dir_specs(False) + dir_specs(True) + [full(tri3), full(masks)],
        out_specs=[row(False, vdim), row(True, vdim)],
        out_shape=[jax.ShapeDtypeStruct((ta, vdim), F32)] * 2,
        scratch_shapes=[pltpu.VMEM((2, GLA_HEADS, GLA_DV, GLA_DK), F32)],
        compiler_params=_cparams(("arbitrary",)),
        name="gla_scan",
    )(q, k, v, gk_f, q, k, v, gk_b, tri3, masks)


def _group_rms(y, w, width):
    parts = []
    for g in range(y.shape[1] // width):
        yg = y[:, g * width:(g + 1) * width]
        ms = jnp.mean(yg * yg, axis=-1, keepdims=True)
        parts.append(yg * lax.rsqrt(ms + EPS))
    return jnp.concatenate(parts, axis=1) * w


def _even_merge_kernel(yf_ref, yb_ref, xs_ref, z_ref, of_ref, ob_ref, r_ref, dsk_ref, snw_ref, gnw_ref, w_ref, o_ref):
    inner = xs_ref.shape[1]
    y = (yf_ref[...] + yb_ref[...] + dsk_ref[...] * xs_ref[...]) * _silu(z_ref[...])
    y = _group_rms(y, snw_ref[...], inner // SSD_GROUPS)
    o = _group_rms(of_ref[...] + ob_ref[...], gnw_ref[...], GLA_DV) * _silu(r_ref[...])
    o_ref[...] = _bdot(y, w_ref[:inner, :]) + _bdot(o, w_ref[inner:, :])


def _even_merge(yf, yb, xs, z, of, ob, r, d_skip, ssd_nw, gla_nw, w_out):
    ta, inner = xs.shape
    vdim = of.shape[1]
    d = w_out.shape[1]
    dsk = jnp.repeat(d_skip, SSD_HEAD_DIM).reshape(1, inner)
    gnw = jnp.tile(gla_nw, GLA_HEADS).reshape(1, vdim)
    row = lambda w: pl.BlockSpec((ROW_TILE, w), lambda i: (i, 0))
    vec = lambda w: pl.BlockSpec((1, w), lambda i: (0, 0))
    return pl.pallas_call(
        _even_merge_kernel,
        grid=(ta // ROW_TILE,),
        in_specs=[row(inner)] * 4 + [row(vdim)] * 3 + [vec(inner), vec(inner), vec(vdim),
                                                        pl.BlockSpec(w_out.shape, lambda i: (0, 0))],
        out_specs=row(d),
        out_shape=jax.ShapeDtypeStruct((ta, d), F32),
        compiler_params=_cparams(("arbitrary",)),
        name="even_merge",
    )(yf, yb, xs, z, of, ob, r, dsk, ssd_nw.reshape(1, inner), gnw, w_out)


def _odd_prep_kernel(qkv_ref, prev_ref, next_ref, small_ref, cw_ref, dtb_ref, aneg_ref,
                     q_ref, k_ref, v_ref, g_ref, beta_ref, ext_ref, *, n_tiles):
    y = _conv_silu(qkv_ref, prev_ref, next_ref, cw_ref, None, ext_ref, n_tiles)
    kdim = q_ref.shape[1]

    def l2n(a, scale):
        parts = []
        for h in range(kdim // GDN_DK):
            ah = a[:, h * GDN_DK:(h + 1) * GDN_DK]
            parts.append(ah * (lax.rsqrt(jnp.sum(ah * ah, axis=-1, keepdims=True) + EPS) * scale))
        return jnp.concatenate(parts, axis=1)

    q_ref[...] = l2n(y[:, :kdim], GDN_DK ** -0.5)
    k_ref[...] = l2n(y[:, kdim:2 * kdim], 1.0)
    v_ref[...] = y[:, 2 * kdim:]
    small = small_ref[...]
    g_ref[...] = _softplus(small + dtb_ref[...]) * aneg_ref[...]
    beta_ref[...] = jax.nn.sigmoid(small)


def _odd_prep(qkv, small, conv_w, dt_bias, a_log):
    ta, cdim = qkv.shape
    n_tiles = ta // ROW_TILE
    kdim = GDN_QK_HEADS * GDN_DK
    vdim = GDN_V_HEADS * GDN_DV
    cw = jnp.zeros((8, cdim), F32).at[:CONV_W].set(conv_w)
    dtb = jnp.zeros((1, LANES), F32).at[0, :2 * SMALL].set(dt_bias.reshape(-1))
    aneg = jnp.zeros((1, LANES), F32).at[0, :2 * SMALL].set(-jnp.exp(a_log.reshape(-1)))
    prev_spec, next_spec = _halo_specs(cdim, n_tiles)
    row = lambda w: pl.BlockSpec((ROW_TILE, w), lambda i: (i, 0))
    full = lambda a: pl.BlockSpec(a.shape, lambda i: (0,) * a.ndim)
    return pl.pallas_call(
        functools.partial(_odd_prep_kernel, n_tiles=n_tiles),
        grid=(n_tiles,),
        in_specs=[row(cdim), prev_spec, next_spec, row(LANES), full(cw), full(dtb), full(aneg)],
        out_specs=[row(kdim), row(kdim), row(vdim), row(LANES), row(LANES)],
        out_shape=[jax.ShapeDtypeStruct((ta, w), F32) for w in (kdim, kdim, vdim, LANES, LANES)],
        scratch_shapes=[pltpu.VMEM((ROW_TILE + 2 * HALO, cdim), F32)],
        compiler_params=_cparams(("arbitrary",)),
        name="odd_prep",
    )(qkv, qkv, qkv, small, cw, dtb, aneg)


GDN_PREP_OUTS = 6


def _gdn_prep_kernel(q_ref, k_ref, v_ref, g_ref, beta_ref, tri_ref, ones_ref, expand_ref, eye_ref,
                     trit_ref, strict_ref, pair_ref, lvl_ref, eye2_ref, *out_refs):
    pair = pair_ref[...]
    eye2 = eye2_ref[...]
    left = lax.broadcasted_iota(jnp.int32, (CHUNK, 2 * CHUNK), 1) < CHUNK
    n_levels = lvl_ref.shape[0]
    kk_qk = []
    for j in range(GDN_QK_HEADS):
        ksl = slice(j * GDN_DK, (j + 1) * GDN_DK)
        kb = k_ref[:, ksl].astype(BF16)
        krep = jnp.concatenate([kb, kb], axis=0)
        kk_qk.append((lax.dot_general(kb, krep, NT_DIMS, preferred_element_type=F32),
                      lax.dot_general(q_ref[:, ksl].astype(BF16), krep, NT_DIMS, preferred_element_type=F32)))
    for d in range(2):
        u_ref, w_ref, qd_ref, kd_ref, att_ref, gl_ref = out_refs[GDN_PREP_OUTS * d:GDN_PREP_OUTS * (d + 1)]
        last_row = 0 if d else CHUNK - 1
        g = g_ref[:, d * SMALL:(d + 1) * SMALL]
        beta = beta_ref[:, (2 + d) * SMALL:(3 + d) * SMALL]
        col, decay = _decay_tables(tri_ref[d], ones_ref[...], expand_ref[...], eye_ref[...], trit_ref[d], g)
        beta_t = _sel_dot_r(beta, expand_ref[...])
        a_all = beta_t * decay * strict_ref[d]
        a_bds, rhs_all = [], []
        for j in range(GDN_QK_HEADS):
            ksl = slice(j * GDN_DK, (j + 1) * GDN_DK)
            psl = slice(j * 2 * CHUNK, (j + 1) * 2 * CHUNK)
            kk, qk = kk_qk[j]
            att_ref[:, psl] = (qk * decay[:, psl]).astype(BF16)
            a_pair = kk * a_all[:, psl]
            a_bds.append(jnp.concatenate([a_pair, a_pair], axis=0) * pair)
            colp, betap = col[:, psl], beta_t[:, psl]
            colr, betar = pltpu.roll(colp, CHUNK, 1), pltpu.roll(betap, CHUNK, 1)
            kj = k_ref[:, ksl]
            rhs = []
            for hh in range(2):
                hsl = slice((2 * j + hh) * GDN_DV, (2 * j + hh + 1) * GDN_DV)
                cumx = jnp.where(left, colp, colr) if hh == 0 else jnp.where(left, colr, colp)
                betax = jnp.where(left, betap, betar) if hh == 0 else jnp.where(left, betar, betap)
                lastx = cumx[last_row:last_row + 1, :]
                egx = jnp.exp(cumx)
                qd_ref[:, hsl] = (q_ref[:, ksl] * egx).astype(BF16)
                kd_ref[:, hsl] = (kj * jnp.exp(lastx - cumx)).astype(BF16)
                gl_ref[0, :, hsl] = jnp.broadcast_to(jnp.exp(lastx), (gl_ref.shape[1], GDN_DV))
                rhs.append(jnp.concatenate([v_ref[:, hsl] * betax, kj * betax * egx], axis=1))
            rhs_all.append(jnp.concatenate(rhs, axis=0))
        tinvs = [eye2 - a * lvl_ref[0] for a in a_bds]
        for lv in range(1, n_levels):
            tls = [_bdot(t, a * lvl_ref[lv]) for t, a in zip(tinvs, a_bds)]
            tinvs = [t - _bdot(tl, t) for t, tl in zip(tinvs, tls)]
        for j in range(GDN_QK_HEADS):
            x = _bdot(tinvs[j], rhs_all[j])
            for hh in range(2):
                hsl = slice((2 * j + hh) * GDN_DV, (2 * j + hh + 1) * GDN_DV)
                u_ref[:, hsl] = x[hh * CHUNK:(hh + 1) * CHUNK, :GDN_DV]
                w_ref[:, hsl] = x[hh * CHUNK:(hh + 1) * CHUNK, GDN_DV:].astype(BF16)


def _gdn_prep(q, k, v, g, beta):
    ta, kdim = q.shape
    vdim = v.shape[1]
    nc = ta // CHUNK
    fwd = _scan_consts(False, GDN_V_HEADS, CHUNK)
    bwd = _scan_consts(True, GDN_V_HEADS, CHUNK)
    both = lambda name: jnp.stack([fwd[name], bwd[name]])
    consts = [both("tri3"), fwd["ones3"], fwd["expand3"], fwd["eye_t"], both("tri_t"), both("strict_t"),
              _pair_mask(), _level_masks(), jnp.eye(2 * CHUNK, dtype=F32)]
    row = lambda w: pl.BlockSpec((CHUNK, w), lambda i: (i, 0))
    full = lambda a: pl.BlockSpec(a.shape, lambda i: (0,) * a.ndim)
    tdim = GDN_V_HEADS * CHUNK
    one_dir_specs = [row(vdim), row(vdim), row(vdim), row(vdim), row(tdim), pl.BlockSpec((1, 8, vdim), lambda i: (i, 0, 0))]
    one_dir_shapes = [jax.ShapeDtypeStruct((ta, vdim), F32), jax.ShapeDtypeStruct((ta, vdim), BF16),
                      jax.ShapeDtypeStruct((ta, vdim), BF16), jax.ShapeDtypeStruct((ta, vdim), BF16),
                      jax.ShapeDtypeStruct((ta, tdim), BF16), jax.ShapeDtypeStruct((nc, 8, vdim), F32)]
    outs = pl.pallas_call(
        _gdn_prep_kernel,
        grid=(nc,),
        in_specs=[row(kdim), row(kdim), row(vdim), row(LANES), row(LANES)] + [full(a) for a in consts],
        out_specs=one_dir_specs * 2,
        out_shape=one_dir_shapes * 2,
        compiler_params=_cparams(("arbitrary",)),
        name="gdn_prep",
    )(q, k, v, g, beta, *consts)
    return outs[:GDN_PREP_OUTS], outs[GDN_PREP_OUTS:]


def _gdn_recur_kernel(*refs):
    ins, (of_ref, ob_ref, st_ref) = refs[:2 * GDN_PREP_OUTS], refs[2 * GDN_PREP_OUTS:]

    @pl.when(pl.program_id(0) == 0)
    def _():
        st_ref[...] = jnp.zeros(st_ref.shape, F32)

    chains = [(d, h) for d in range(2) for h in range(GDN_V_HEADS)]
    o_refs = (of_ref, ob_ref)
    hsl = lambda h: slice(h * GDN_DV, (h + 1) * GDN_DV)
    u_refs, w_refs, qd_refs, kd_refs, att_refs, gl_refs = (ins[i::GDN_PREP_OUTS] for i in range(GDN_PREP_OUTS))
    ws_qs = [jnp.dot(jnp.concatenate([w_refs[d][:, hsl(h)], qd_refs[d][:, hsl(h)]], axis=0),
                     st_ref[d, h].astype(BF16), preferred_element_type=F32) for d, h in chains]
    v_new = [(u_refs[d][:, hsl(h)] - r[:CHUNK]).astype(BF16) for (d, h), r in zip(chains, ws_qs)]
    for (d, h), r, vn in zip(chains, ws_qs, v_new):
        o_refs[d][:, hsl(h)] = r[CHUNK:] + jnp.dot(att_refs[d][:, h * CHUNK:(h + 1) * CHUNK], vn,
                                                    preferred_element_type=F32)
    for (d, h), vn in zip(chains, v_new):
        st_ref[d, h] = st_ref[d, h] * gl_refs[d][0, 0:1, hsl(h)] + lax.dot_general(
            kd_refs[d][:, hsl(h)], vn, TN_DIMS, preferred_element_type=F32)


def _gdn_recur(prep_f, prep_b, n_ctx_chunks):
    ta, vdim = prep_f[0].shape
    nc = ta // CHUNK
    tdim = GDN_V_HEADS * CHUNK

    def specs(rev):
        cmap = _chunk_map(rev, nc, n_ctx_chunks)
        row = lambda w: pl.BlockSpec((CHUNK, w), lambda i: (cmap(i), 0))
        return [row(vdim), row(vdim), row(vdim), row(vdim), row(tdim),
                pl.BlockSpec((1, 8, vdim), lambda i: (cmap(i), 0, 0))], row(vdim)

    in_f, out_f = specs(False)
    in_b, out_b = specs(True)
    return pl.pallas_call(
        _gdn_recur_kernel,
        grid=(nc,),
        in_specs=in_f + in_b,
        out_specs=[out_f, out_b],
        out_shape=[jax.ShapeDtypeStruct((ta, vdim), F32)] * 2,
        scratch_shapes=[pltpu.VMEM((2, GDN_V_HEADS, GDN_DK, GDN_DV), F32)],
        compiler_params=_cparams(("arbitrary",)),
        name="gdn_recur",
    )(*prep_f, *prep_b)


def _odd_merge_kernel(of_ref, ob_ref, z_ref, nw_ref, w_ref, o_ref):
    o = _group_rms(of_ref[...] + ob_ref[...], nw_ref[...], GDN_DV) * _silu(z_ref[...])
    o_ref[...] = _bdot(o, w_ref[...])


def _odd_merge(of, ob, z, gdn_nw, w_out):
    ta, vdim = of.shape
    d = w_out.shape[1]
    nw = jnp.tile(gdn_nw, GDN_V_HEADS).reshape(1, vdim)
    row = lambda w: pl.BlockSpec((ROW_TILE, w), lambda i: (i, 0))
    return pl.pallas_call(
        _odd_merge_kernel,
        grid=(ta // ROW_TILE,),
        in_specs=[row(vdim)] * 3 + [pl.BlockSpec((1, vdim), lambda i: (0, 0)), pl.BlockSpec(w_out.shape, lambda i: (0, 0))],
        out_specs=row(d),
        out_shape=jax.ShapeDtypeStruct((ta, d), F32),
        compiler_params=_cparams(("arbitrary",)),
        name="odd_merge",
    )(of, ob, z, nw, w_out)


def _ffn_kernel(x_ref, y_ref, mod_ref, nw_ref, w1_ref, w3_ref, w2_ref, fw_ref, o_ref, *, n_ctx_tiles, final):
    d = x_ref.shape[1]
    is_ctx = pl.program_id(0) < n_ctx_tiles
    x = x_ref[...] + _mod_rows(mod_ref, is_ctx, 2, d) * y_ref[...]
    h = _norm_mod(x, nw_ref[...], _mod_rows(mod_ref, is_ctx, 3, d), _mod_rows(mod_ref, is_ctx, 4, d)).astype(BF16)
    a = jnp.dot(h, w1_ref[...], preferred_element_type=F32)
    b = jnp.dot(h, w3_ref[...], preferred_element_type=F32)
    out = x + _mod_rows(mod_ref, is_ctx, 5, d) * _bdot(_silu(a) * b, w2_ref[...])
    if final:
        ms = jnp.mean(out * out, axis=-1, keepdims=True)
        out = out * lax.rsqrt(ms + EPS) * fw_ref[...]
    o_ref[...] = out


def _ffn(x_all, y_all, mod, nw, w1, w3, w2, final_w, n_ctx_tiles, final):
    ta, d = x_all.shape
    row = pl.BlockSpec((ROW_TILE, d), lambda i: (i, 0))
    full = lambda a: pl.BlockSpec(a.shape, lambda i: (0,) * a.ndim)
    nw = nw.reshape(1, d)
    fw = final_w.reshape(1, d)
    return pl.pallas_call(
        functools.partial(_ffn_kernel, n_ctx_tiles=n_ctx_tiles, final=final),
        grid=(ta // ROW_TILE,),
        in_specs=[row, row, full(mod), full(nw), full(w1), full(w3), full(w2), full(fw)],
        out_specs=row,
        out_shape=jax.ShapeDtypeStruct((ta, d), F32),
        compiler_params=_cparams(("arbitrary",)),
        name="ffn_final" if final else "ffn",
    )(x_all, y_all, mod, nw, w1, w3, w2, fw)


def _pad_cols(parts, width):
    w = jnp.concatenate(parts, axis=1)
    return jnp.pad(w, ((0, 0), (0, width - w.shape[1])))


def _even_w_in(w):
    inner = SSD_HEADS * SSD_HEAD_DIM
    conv = inner + 2 * SSD_GROUPS * SSD_STATE
    gkey = GLA_HEADS * GLA_DK
    gval = GLA_HEADS * GLA_DV
    sizes = (inner, conv, SMALL, SMALL, gkey, gkey, gval, gval, SMALL, SMALL)
    offs = [0]
    for s in sizes:
        offs.append(offs[-1] + s)
    z, xbc, dtf, dtb, q, k, v, r, glf, glb = (w[:, offs[i]:offs[i + 1]] for i in range(len(sizes)))
    small = _pad_cols([dtf, dtb, glf, glb], LANES)
    widths = (inner, conv, gkey, gkey, gval, gval, LANES)
    return jnp.concatenate([z, xbc, q, k, v, r, small], axis=1).astype(BF16), widths


def _odd_w_in(w):
    kdim = GDN_QK_HEADS * GDN_DK
    vdim = GDN_V_HEADS * GDN_DV
    conv = 2 * kdim + vdim
    small = _pad_cols([w[:, conv + vdim:]], LANES)
    widths = (conv, vdim, LANES)
    return jnp.concatenate([w[:, :conv + vdim], small], axis=1).astype(BF16), widths


def _to_column_major(h):
    t, d = h.shape
    return h.reshape(t // GRID_W, GRID_W, d).transpose(1, 0, 2).reshape(t, d)


def _from_column_major(h):
    t, d = h.shape
    return h.reshape(GRID_W, t // GRID_W, d).transpose(1, 0, 2).reshape(t, d)


def kernel(x, c, ctx, c_ctx, mod_w, mod_b, norm_mix, norm_ffn, ffn_w1, ffn_w3, ffn_w2, ev_w_in, ssd_conv_w, ssd_conv_b, ssd_dt_bias, ssd_a_log, ssd_d, ssd_norm, gla_gate_w, gla_gate_b, gla_norm, ev_w_out, od_w_in, gdn_conv_w, gdn_dt_bias, gdn_a_log, gdn_norm, od_w_out, final_norm):
    batch, seq, d = x.shape
    n_ctx = ctx.shape[1]
    assert batch == 1 and n_ctx == ROW_TILE and seq % ROW_TILE == 0 and seq % GRID_W == 0
    assert mod_w.shape[0] == 2, "one even (SSD + GLA) layer followed by one odd (gated DeltaNet) layer"
    n_ctx_tiles = n_ctx // ROW_TILE
    n_ctx_chunks = n_ctx // CHUNK

    mods = _modulation(c, c_ctx, mod_w, mod_b)
    x_all = jnp.concatenate([ctx[0], x[0]], axis=0)

    w_in, widths = _even_w_in(ev_w_in[0])
    z, xbc, q, k, v, r, small = _norm_proj(x_all, norm_mix[0], mods[0], w_in, widths, n_ctx_tiles)
    xs, bm, cm, dt, da, gk_f, gk_b = _even_prep(xbc, small, ssd_conv_w[0], ssd_conv_b[0], ssd_dt_bias[0],
                                                ssd_a_log[0], gla_gate_w[0], gla_gate_b[0])
    y_f, y_b = _ssd_scan(xs, bm, cm, dt, da, n_ctx_chunks)
    o_f, o_b = _gla_scan(q, k, v, gk_f, gk_b, n_ctx_chunks)
    mixed = _even_merge(y_f, y_b, xs, z, o_f, o_b, r, ssd_d[0], ssd_norm[0], gla_norm[0], ev_w_out[0].astype(BF16))
    x_all = _ffn(x_all, mixed, mods[0], norm_ffn[0], ffn_w1[0].astype(BF16), ffn_w3[0].astype(BF16),
                 ffn_w2[0].astype(BF16), final_norm, n_ctx_tiles, False)

    x_lat = x_all[n_ctx:]
    x_perm = jnp.concatenate([x_all[:n_ctx], _to_column_major(x_lat)], axis=0)
    w_in, widths = _odd_w_in(od_w_in[0])
    qkv, z, small = _norm_proj(x_perm, norm_mix[1], mods[1], w_in, widths, n_ctx_tiles)
    q, k, v, g, beta = _odd_prep(qkv, small, gdn_conv_w[0], gdn_dt_bias[0], gdn_a_log[0])
    prep_f, prep_b = _gdn_prep(q, k, v, g, beta)
    o_f, o_b = _gdn_recur(prep_f, prep_b, n_ctx_chunks)
    mixed = _odd_merge(o_f, o_b, z, gdn_norm[0], od_w_out[0].astype(BF16))
    mixed = _from_column_major(mixed[n_ctx:])
    out = _ffn(x_lat, mixed, mods[1], norm_ffn[1], ffn_w1[1].astype(BF16), ffn_w3[1].astype(BF16),
               ffn_w2[1].astype(BF16), final_norm, 0, True)
    return out[None]
```

```python
import functools

import jax
import jax.numpy as jnp
from jax import lax
from jax.experimental import pallas as pl
from jax.experimental.pallas import tpu as pltpu

F32 = jnp.float32
BF16 = jnp.bfloat16
HIGHEST = lax.Precision.HIGHEST

EPS = 1e-6
CHUNK = 64
SCAN_CPS = 4
GRID_W = 64
CONV_W = 5
ROW_TILE = 256
HALO = 8
LANES = 128
VMEM_LIMIT_BYTES = 56 * 1024 * 1024

SSD_HEADS = 16
SSD_HEAD_DIM = 64
SSD_GROUPS = 2
SSD_STATE = 128
GLA_HEADS = 4
GLA_DK = 128
GLA_DV = 256
GLA_GATE_RANK = 16
GLA_GATE_NORM = 16.0
GDN_QK_HEADS = 8
GDN_V_HEADS = 16
GDN_DK = 128
GDN_DV = 128
SMALL = 16

NT_DIMS = (((1,), (1,)), ((), ()))
TN_DIMS = (((0,), (0,)), ((), ()))


def _cparams(sem):
    return pltpu.CompilerParams(dimension_semantics=sem, vmem_limit_bytes=VMEM_LIMIT_BYTES)


def _bdot(a, b):
    return jnp.dot(a.astype(BF16), b.astype(BF16), preferred_element_type=F32)


def _bdot_nt(a, b):
    return lax.dot_general(a.astype(BF16), b.astype(BF16), NT_DIMS, preferred_element_type=F32)


def _bdot_tn(a, b):
    return lax.dot_general(a.astype(BF16), b.astype(BF16), TN_DIMS, preferred_element_type=F32)


def _split3(x):
    hi = x.astype(BF16)
    r1 = x - hi.astype(F32)
    mid = r1.astype(BF16)
    lo = (r1 - mid.astype(F32)).astype(BF16)
    return hi, mid, lo


def _sel_dot_l(sel3, x):
    return jnp.dot(sel3, jnp.concatenate(_split3(x), axis=0), preferred_element_type=F32)


def _sel_dot_r(x, sel3):
    pieces = jnp.concatenate([p.astype(F32) for p in _split3(x)], axis=1)
    return jnp.dot(pieces.astype(BF16), sel3, preferred_element_type=F32)


def _silu(x):
    return x * jax.nn.sigmoid(x)


def _softplus(x):
    return jnp.maximum(x, 0.0) + jnp.log1p(jnp.exp(-jnp.abs(x)))


def _mod_kernel(c_ref, w_ref, b_ref, o_ref):
    s = _silu(c_ref[...])
    o_ref[0] = jnp.dot(s, w_ref[0], precision=HIGHEST, preferred_element_type=F32) + b_ref[0]


def _modulation(c, c_ctx, mod_w, mod_b):
    depth, d, n = mod_w.shape
    cs = jnp.zeros((8, d), F32).at[0].set(c[0]).at[1].set(c_ctx)
    tn = n // 4
    return pl.pallas_call(
        _mod_kernel,
        grid=(depth, n // tn),
        in_specs=[pl.BlockSpec((8, d), lambda i, j: (0, 0)),
                  pl.BlockSpec((1, d, tn), lambda i, j: (i, 0, j)),
                  pl.BlockSpec((1, 1, tn), lambda i, j: (i, 0, j))],
        out_specs=pl.BlockSpec((1, 8, tn), lambda i, j: (i, 0, j)),
        out_shape=jax.ShapeDtypeStruct((depth, 8, n), F32),
        compiler_params=_cparams(("arbitrary", "arbitrary")),
        name="modulation",
    )(cs, mod_w, mod_b.reshape(depth, 1, n))


def _mod_rows(mod_ref, is_ctx, idx, d):
    sl = slice(idx * d, (idx + 1) * d)
    return jnp.where(is_ctx, mod_ref[1:2, sl], mod_ref[0:1, sl])


def _norm_mod(x, nw, shift, scale):
    ms = jnp.mean(x * x, axis=-1, keepdims=True)
    return (x * lax.rsqrt(ms + EPS) * nw) * (1.0 + scale) + shift


def _proj_kernel(x_ref, nw_ref, mod_ref, w_ref, *out_refs, widths, n_ctx_tiles):
    d = x_ref.shape[1]
    is_ctx = pl.program_id(0) < n_ctx_tiles
    h = _norm_mod(x_ref[...], nw_ref[...], _mod_rows(mod_ref, is_ctx, 0, d), _mod_rows(mod_ref, is_ctx, 1, d))
    h = h.astype(BF16)
    off = 0
    for o_ref, wd in zip(out_refs, widths):
        o_ref[...] = jnp.dot(h, w_ref[:, off:off + wd], preferred_element_type=F32).astype(o_ref.dtype)
        off += wd


def _norm_proj(x_all, nw, mod, w, widths, dtypes, n_ctx_tiles):
    ta, d = x_all.shape
    ntot = sum(widths)
    kern = functools.partial(_proj_kernel, widths=widths, n_ctx_tiles=n_ctx_tiles)
    return pl.pallas_call(
        kern,
        grid=(ta // ROW_TILE,),
        in_specs=[pl.BlockSpec((ROW_TILE, d), lambda i: (i, 0)),
                  pl.BlockSpec((1, d), lambda i: (0, 0)),
                  pl.BlockSpec(mod.shape, lambda i: (0, 0)),
                  pl.BlockSpec((d, ntot), lambda i: (0, 0))],
        out_specs=[pl.BlockSpec((ROW_TILE, wd), lambda i: (i, 0)) for wd in widths],
        out_shape=[jax.ShapeDtypeStruct((ta, wd), dt) for wd, dt in zip(widths, dtypes)],
        compiler_params=_cparams(("arbitrary",)),
        name="norm_proj",
    )(x_all, nw.reshape(1, d), mod, w)


def _halo_specs(width, n_tiles):
    per = ROW_TILE // HALO
    prev = pl.BlockSpec((HALO, width), lambda i: (jnp.maximum(i * per - 1, 0), 0))
    nxt = pl.BlockSpec((HALO, width), lambda i: (jnp.minimum((i + 1) * per, n_tiles * per - 1), 0))
    return prev, nxt


def _conv_silu(cur_ref, prev_ref, next_ref, w_ref, bias, ext_ref, n_tiles):
    i = pl.program_id(0)
    prev_ok = jnp.logical_and(i != 0, i != 1)
    next_ok = jnp.logical_and(i != 0, i != n_tiles - 1)
    ext_ref[0:HALO, :] = jnp.where(prev_ok, prev_ref[...], 0.0)
    ext_ref[HALO:HALO + ROW_TILE, :] = cur_ref[...]
    ext_ref[HALO + ROW_TILE:, :] = jnp.where(next_ok, next_ref[...], 0.0)
    pad = CONV_W // 2
    acc = None
    for j in range(CONV_W):
        term = ext_ref[pl.ds(HALO - pad + j, ROW_TILE), :] * w_ref[j:j + 1, :]
        acc = term if acc is None else acc + term
    if bias is not None:
        acc = acc + bias
    return _silu(acc)


def _even_prep_kernel(xbc_ref, prev_ref, next_ref, small_ref, cw_ref, cb_ref, dtb_ref, aneg_ref,
                      gw_ref, gb_ref, xs_ref, bm_ref, cm_ref, dt_ref, da_ref, gkf_ref, gkb_ref,
                      ext_ref, *, n_tiles):
    y = _conv_silu(xbc_ref, prev_ref, next_ref, cw_ref, cb_ref[...], ext_ref, n_tiles)
    inner = xs_ref.shape[1]
    gs = bm_ref.shape[1]
    xs_ref[...] = y[:, :inner]
    bm_ref[...] = y[:, inner:inner + gs]
    cm_ref[...] = y[:, inner + gs:]
    small = small_ref[...]
    dt = _softplus(small + dtb_ref[...])
    dt_ref[...] = dt
    da_ref[...] = dt * aneg_ref[...]
    for d, o_ref in enumerate((gkf_ref, gkb_ref)):
        pre = jnp.dot(small, gw_ref[d], precision=HIGHEST, preferred_element_type=F32) + gb_ref[d]
        o_ref[...] = -_softplus(-pre) * (1.0 / GLA_GATE_NORM)


def _even_prep(xbc, small, conv_w, conv_b, dt_bias, a_log, gate_w, gate_b):
    ta, cdim = xbc.shape
    n_tiles = ta // ROW_TILE
    inner = SSD_HEADS * SSD_HEAD_DIM
    gs = SSD_GROUPS * SSD_STATE
    gkey = GLA_HEADS * GLA_DK
    cw = jnp.zeros((8, cdim), F32).at[:CONV_W].set(conv_w)
    dtb = jnp.zeros((1, LANES), F32).at[0, :2 * SMALL].set(dt_bias.reshape(-1))
    aneg = jnp.zeros((1, LANES), F32).at[0, :2 * SMALL].set(-jnp.exp(a_log.reshape(-1)))
    gw = jnp.zeros((2, LANES, gkey), F32)
    gw = gw.at[0, 2 * SMALL:3 * SMALL].set(gate_w[0]).at[1, 3 * SMALL:4 * SMALL].set(gate_w[1])
    prev_spec, next_spec = _halo_specs(cdim, n_tiles)
    row = lambda w: pl.BlockSpec((ROW_TILE, w), lambda i: (i, 0))
    full = lambda a: pl.BlockSpec(a.shape, lambda i: (0,) * a.ndim)
    gb = gate_b.reshape(2, 1, gkey)
    return pl.pallas_call(
        functools.partial(_even_prep_kernel, n_tiles=n_tiles),
        grid=(n_tiles,),
        in_specs=[row(cdim), prev_spec, next_spec, row(LANES), full(cw), pl.BlockSpec((1, cdim), lambda i: (0, 0)),
                  full(dtb), full(aneg), full(gw), full(gb)],
        out_specs=[row(inner), row(gs), row(gs), row(LANES), row(LANES), row(gkey), row(gkey)],
        out_shape=[jax.ShapeDtypeStruct((ta, w), F32) for w in (inner, gs, gs, LANES, LANES, gkey, gkey)],
        scratch_shapes=[pltpu.VMEM((ROW_TILE + 2 * HALO, cdim), F32)],
        compiler_params=_cparams(("arbitrary",)),
        name="even_prep",
    )(xbc, xbc, xbc, small, cw, conv_b.reshape(1, cdim), dtb, aneg, gw, gb)


def _scan_consts(rev, heads, lanes_per_head):
    r = jnp.arange(CHUNK)
    tri = (r[:, None] <= r[None, :]) if rev else (r[:, None] >= r[None, :])
    strict = (r[:, None] < r[None, :]) if rev else (r[:, None] > r[None, :])
    eye = r[:, None] == r[None, :]
    expand = (jnp.arange(heads)[:, None] == (jnp.arange(heads * lanes_per_head)[None, :] // lanes_per_head))
    return dict(
        tri3=jnp.tile(tri, (1, 3)).astype(BF16),
        expand3=jnp.tile(expand, (3, 1)).astype(BF16),
        tri_t=jnp.tile(tri, (1, heads)).astype(F32),
        strict_t=jnp.tile(strict, (1, heads)).astype(F32),
        eye_t=jnp.tile(eye, (1, heads)).astype(F32),
    )


def _pair_mask():
    r = jnp.arange(2 * CHUNK)
    return ((r[:, None] // CHUNK) == (r[None, :] // CHUNK)).astype(F32)


def _level_masks():
    r = jnp.arange(2 * CHUNK)
    same = lambda b: (r[:, None] // b) == (r[None, :] // b)
    sizes = [2 ** m for m in range(CHUNK.bit_length() - 1)]
    return jnp.stack([jnp.logical_and(same(2 * b), jnp.logical_not(same(b))) for b in sizes]).astype(F32)


def _chunk_map(rev, n_chunks, n_ctx_chunks):
    if not rev:
        return lambda i: i
    return lambda i: jnp.where(i < n_ctx_chunks, n_ctx_chunks - 1 - i, n_chunks + n_ctx_chunks - 1 - i)


def _decay_tables(tri3s, expand3, eye_t, tri_ts, gs):
    cums = [_sel_dot_l(tri3, g) for tri3, g in zip(tri3s, gs)]
    cols = [_sel_dot_r(cum, expand3) for cum in cums]
    rowfs = [jnp.sum(col * eye_t, axis=0, keepdims=True) for col in cols]
    decays = [jnp.where(tri_t > 0.5, jnp.exp(col - rowf), 0.0) for tri_t, col, rowf in zip(tri_ts, cols, rowfs)]
    return cols, decays


SSD_SCAN_INS = 5


def _ssd_kernel(*refs, cps):
    ins = refs[:2 * SSD_SCAN_INS]
    tri_ref, expand_ref, eye_ref, trit_ref, pair_ref, yf_ref, yb_ref, st_ref = refs[2 * SSD_SCAN_INS:]

    @pl.when(pl.program_id(0) == 0)
    def _():
        st_ref[...] = jnp.zeros(st_ref.shape, F32)

    pair = pair_ref[...]
    heads_g = SSD_HEADS // SSD_GROUPS
    gw = heads_g * SSD_HEAD_DIM
    pieces = [(d, c, g) for d in range(2) for c in range(cps) for g in range(SSD_GROUPS)]
    gsl = lambda g: slice(g * SSD_STATE, (g + 1) * SSD_STATE)
    lanes = lambda g: slice(g * gw, (g + 1) * gw)
    rows = lambda d, c: slice((cps - 1 - c if d else c) * CHUNK, (cps - c if d else c + 1) * CHUNK)
    chunks = [(d, c) for d in range(2) for c in range(cps)]
    small = lambda d, c, k: ins[SSD_SCAN_INS * d + k][rows(d, c), d * SMALL:(d + 1) * SMALL]
    cols, decays = _decay_tables([tri_ref[d] for d, _ in chunks], expand_ref[...], eye_ref[...],
                                 [trit_ref[d] for d, _ in chunks], [small(d, c, 4) for d, c in chunks])
    dtes = [_sel_dot_r(small(d, c, 3), expand_ref[...]) for d, c in chunks]
    tabs = {}
    for (d, c), col, decay, dte in zip(chunks, cols, decays, dtes):
        last_row = 0 if d else CHUNK - 1
        xdt = ins[SSD_SCAN_INS * d][rows(d, c), :] * dte
        last = col[last_row:last_row + 1, :]
        tabs[d, c] = dict(decay=decay, xdt=xdt, xw=xdt * jnp.exp(last - col), chunk_decay=jnp.exp(last),
                          eac=jnp.exp(col))
    bgs = {(d, c, g): ins[SSD_SCAN_INS * d + 1][rows(d, c), gsl(g)].astype(BF16) for d, c, g in pieces}
    cgs = {(d, c, g): ins[SSD_SCAN_INS * d + 2][rows(d, c), gsl(g)].astype(BF16) for d, c, g in pieces}
    wms = {p: lax.dot_general(cgs[p], jnp.concatenate([bgs[p]] * heads_g, axis=0), NT_DIMS,
                              preferred_element_type=F32) * tabs[p[:2]]["decay"][:, lanes(p[2])] for p in pieces}
    news = {(d, c, g): lax.dot_general(bgs[d, c, g], tabs[d, c]["xw"][:, lanes(g)].astype(BF16), TN_DIMS,
                                       preferred_element_type=F32) for d, c, g in pieces}
    diags = {}
    for d, c, g in pieces:
        xg = tabs[d, c]["xdt"][:, lanes(g)]
        for j in range(gw // LANES):
            xp = xg[:, j * LANES:(j + 1) * LANES]
            blockdiag = jnp.concatenate([xp, xp], axis=0) * pair
            diags[d, c, g, j] = _bdot(wms[d, c, g][:, j * LANES:(j + 1) * LANES], blockdiag)
    states = {}
    for d in range(2):
        for g in range(SSD_GROUPS):
            state = st_ref[d, g]
            for c in range(cps):
                states[d, c, g] = state
                state = state * tabs[d, c]["chunk_decay"][:, lanes(g)] + news[d, c, g]
            st_ref[d, g] = state
    offs = {p: jnp.dot(cgs[p], states[p].astype(BF16), preferred_element_type=F32) for p in pieces}
    for d, y_ref in enumerate((yf_ref, yb_ref)):
        for c in range(cps):
            for g in range(SSD_GROUPS):
                diag = jnp.concatenate([diags[d, c, g, j] for j in range(gw // LANES)], axis=1)
                y_ref[rows(d, c), lanes(g)] = (diag + offs[d, c, g] * tabs[d, c]["eac"][:, lanes(g)]).astype(y_ref.dtype)


def _ssd_scan(xs, bm, cm, dt, da, n_ctx_chunks):
    ta, inner = xs.shape
    cps = SCAN_CPS
    nb = ta // (cps * CHUNK)
    fwd = _scan_consts(False, SSD_HEADS, SSD_HEAD_DIM)
    bwd = _scan_consts(True, SSD_HEADS, SSD_HEAD_DIM)
    both = lambda name: jnp.stack([fwd[name], bwd[name]])
    consts = [both("tri3"), fwd["expand3"], fwd["eye_t"], both("tri_t"), _pair_mask()]
    full = lambda a: pl.BlockSpec(a.shape, lambda i: (0,) * a.ndim)
    gs = SSD_GROUPS * SSD_STATE

    def row(rev, w):
        cmap = _chunk_map(rev, nb, n_ctx_chunks // cps)
        return pl.BlockSpec((cps * CHUNK, w), lambda i: (cmap(i), 0))

    dir_specs = lambda rev: [row(rev, inner), row(rev, gs), row(rev, gs), row(rev, LANES), row(rev, LANES)]
    return pl.pallas_call(
        functools.partial(_ssd_kernel, cps=cps),
        grid=(nb,),
        in_specs=dir_specs(False) + dir_specs(True) + [full(a) for a in consts],
        out_specs=[row(False, inner), row(True, inner)],
        out_shape=[jax.ShapeDtypeStruct((ta, inner), BF16)] * 2,
        scratch_shapes=[pltpu.VMEM((2, SSD_GROUPS, SSD_STATE, inner // SSD_GROUPS), F32)],
        compiler_params=_cparams(("arbitrary",)),
        name="ssd_scan",
    )(xs, bm, cm, dt, da, xs, bm, cm, dt, da, *consts)


GLA_SUB = 8
GLA_LEVELS = (8, 16, 32)
GLA_SCAN_INS = 4


def _gla_masks():
    r = jnp.arange(CHUNK)
    same = lambda b: (r[:, None] // b) == (r[None, :] // b)
    out = []
    for rev in (False, True):
        strict = (r[:, None] < r[None, :]) if rev else (r[:, None] > r[None, :])
        lv = [same(2 * b) & ~same(b) & strict for b in GLA_LEVELS]
        within = r % GLA_SUB
        dg = []
        for s in range(GLA_SUB):
            seen = (within <= s) if rev else (within >= s)
            dg.append((r[None, :] == (r[:, None] // GLA_SUB) * GLA_SUB + s) & seen[:, None])
        out.append(jnp.stack(lv + dg))
    return jnp.stack(out).astype(F32)


def _block_rows(a, rows, b):
    return jnp.concatenate([jnp.broadcast_to(a[r:r + 1, :], (b, a.shape[1])) for r in rows], axis=0)


def _gla_kernel(*refs):
    ins = refs[:2 * GLA_SCAN_INS]
    tri_ref, mask_ref, of_ref, ob_ref, st_ref = refs[2 * GLA_SCAN_INS:]

    @pl.when(pl.program_id(0) == 0)
    def _():
        st_ref[...] = jnp.zeros(st_ref.shape, F32)

    pieces = [(d, h) for d in range(2) for h in range(GLA_HEADS)]
    ksl = lambda h: slice(h * GLA_DK, (h + 1) * GLA_DK)
    vsl = lambda h: slice(h * GLA_DV, (h + 1) * GLA_DV)
    q_of = lambda d, h: ins[GLA_SCAN_INS * d][:, ksl(h)] * (GLA_DK ** -0.5)
    k_of = lambda d, h: ins[GLA_SCAN_INS * d + 1][:, ksl(h)]
    v_of = lambda d, h: ins[GLA_SCAN_INS * d + 2][:, vsl(h)]
    within = lax.broadcasted_iota(jnp.int32, (CHUNK, 1), 0) % GLA_SUB
    gcs = {(d, h): _sel_dot_l(tri_ref[d], ins[GLA_SCAN_INS * d + 3][:, ksl(h)]) for d, h in pieces}
    inters, news = {}, {}
    for d, h in pieces:
        gc = gcs[d, h]
        last = 0 if d else CHUNK - 1
        glast = gc[last:last + 1, :]
        inters[d, h] = _bdot_nt(q_of(d, h) * jnp.exp(gc), st_ref[d, h])
        news[d, h] = (jnp.exp(glast), _bdot_tn(v_of(d, h), k_of(d, h) * jnp.exp(glast - gc)))
    for d, h in pieces:
        st_ref[d, h] = st_ref[d, h] * news[d, h][0] + news[d, h][1]
    atts = {}
    for d, h in pieces:
        gc, qh, kh = gcs[d, h], q_of(d, h), k_of(d, h)
        att = None
        for li, b in enumerate(GLA_LEVELS):
            nb = CHUNK // b
            if d:
                rq = [min(b * (i + 1), CHUNK - 1) for i in range(nb)]
                rk = [b * i for i in range(nb)]
            else:
                rq = [max(b * i - 1, 0) for i in range(nb)]
                rk = [b * (i + 1) - 1 for i in range(nb)]
            qs = qh * jnp.exp(gc - _block_rows(gc, rq, b))
            ks = kh * jnp.exp(_block_rows(gc, rk, b) - gc)
            part = _bdot_nt(qs, ks) * mask_ref[d, li]
            att = part if att is None else att + part
        for s in range(GLA_SUB):
            rows = [GLA_SUB * i + s for i in range(CHUNK // GLA_SUB)]
            seen = (within <= s) if d else (within >= s)
            e = jnp.where(seen, jnp.exp(gc - _block_rows(gc, rows, GLA_SUB)), 0.0)
            a = jnp.sum(qh * _block_rows(kh, rows, GLA_SUB) * e, axis=-1, keepdims=True)
            att = att + a * mask_ref[d, len(GLA_LEVELS) + s]
        atts[d, h] = att
    for d, o_ref in enumerate((of_ref, ob_ref)):
        for h in range(GLA_HEADS):
            o_ref[:, vsl(h)] = (inters[d, h] + _bdot(atts[d, h], v_of(d, h))).astype(o_ref.dtype)


def _gla_scan(q, k, v, gk_f, gk_b, n_ctx_chunks):
    ta, kdim = q.shape
    vdim = v.shape[1]
    nc = ta // CHUNK
    tri3 = jnp.stack([_scan_consts(rev, 1, 1)["tri3"] for rev in (False, True)])
    masks = _gla_masks()
    full = lambda a: pl.BlockSpec(a.shape, lambda i: (0,) * a.ndim)

    def row(rev, w):
        cmap = _chunk_map(rev, nc, n_ctx_chunks)
        return pl.BlockSpec((CHUNK, w), lambda i: (cmap(i), 0))

    dir_specs = lambda rev: [row(rev, kdim), row(rev, kdim), row(rev, vdim), row(rev, kdim)]
    return pl.pallas_call(
        _gla_kernel,
        grid=(nc,),
        in_specs=dir_specs(False) + dir_specs(True) + [full(tri3), full(masks)],
        out_specs=[row(False, vdim), row(True, vdim)],
        out_shape=[jax.ShapeDtypeStruct((ta, vdim), BF16)] * 2,
        scratch_shapes=[pltpu.VMEM((2, GLA_HEADS, GLA_DV, GLA_DK), F32)],
        compiler_params=_cparams(("arbitrary",)),
        name="gla_scan",
    )(q, k, v, gk_f, q, k, v, gk_b, tri3, masks)


def _group_rms(y, w, width):
    parts = []
    for g in range(y.shape[1] // width):
        yg = y[:, g * width:(g + 1) * width]
        ms = jnp.mean(yg * yg, axis=-1, keepdims=True)
        parts.append(yg * lax.rsqrt(ms + EPS))
    return jnp.concatenate(parts, axis=1) * w


def _even_merge_kernel(yf_ref, yb_ref, xs_ref, z_ref, of_ref, ob_ref, r_ref, dsk_ref, snw_ref, gnw_ref, w_ref, o_ref):
    inner = xs_ref.shape[1]
    f32 = lambda ref: ref[...].astype(F32)
    y = (f32(yf_ref) + f32(yb_ref) + dsk_ref[...] * xs_ref[...]) * _silu(f32(z_ref))
    y = _group_rms(y, snw_ref[...], inner // SSD_GROUPS)
    o = _group_rms(f32(of_ref) + f32(ob_ref), gnw_ref[...], GLA_DV) * _silu(f32(r_ref))
    o_ref[...] = _bdot(y, w_ref[:inner, :]) + _bdot(o, w_ref[inner:, :])


def _even_merge(yf, yb, xs, z, of, ob, r, d_skip, ssd_nw, gla_nw, w_out):
    ta, inner = xs.shape
    vdim = of.shape[1]
    d = w_out.shape[1]
    dsk = jnp.repeat(d_skip, SSD_HEAD_DIM).reshape(1, inner)
    gnw = jnp.tile(gla_nw, GLA_HEADS).reshape(1, vdim)
    row = lambda w: pl.BlockSpec((ROW_TILE, w), lambda i: (i, 0))
    vec = lambda w: pl.BlockSpec((1, w), lambda i: (0, 0))
    return pl.pallas_call(
        _even_merge_kernel,
        grid=(ta // ROW_TILE,),
        in_specs=[row(inner)] * 4 + [row(vdim)] * 3 + [vec(inner), vec(inner), vec(vdim),
                                                        pl.BlockSpec(w_out.shape, lambda i: (0, 0))],
        out_specs=row(d),
        out_shape=jax.ShapeDtypeStruct((ta, d), F32),
        compiler_params=_cparams(("arbitrary",)),
        name="even_merge",
    )(yf, yb, xs, z, of, ob, r, dsk, ssd_nw.reshape(1, inner), gnw, w_out)


def _odd_prep_kernel(qkv_ref, prev_ref, next_ref, small_ref, cw_ref, dtb_ref, aneg_ref,
                     q_ref, k_ref, v_ref, g_ref, beta_ref, ext_ref, *, n_tiles):
    kdim = q_ref.shape[1]

    def l2n(a, scale):
        parts = []
        for h in range(kdim // GDN_DK):
            ah = a[:, h * GDN_DK:(h + 1) * GDN_DK]
            parts.append(ah * (lax.rsqrt(jnp.sum(ah * ah, axis=-1, keepdims=True) + EPS) * scale))
        return jnp.concatenate(parts, axis=1)

    y = _conv_silu(qkv_ref, prev_ref, next_ref, cw_ref, None, ext_ref, n_tiles)
    q_ref[...] = l2n(y[:, :kdim], GDN_DK ** -0.5)
    k_ref[...] = l2n(y[:, kdim:2 * kdim], 1.0)
    v_ref[...] = y[:, 2 * kdim:]
    small = small_ref[...]
    g_ref[...] = _softplus(small + dtb_ref[...]) * aneg_ref[...]
    beta_ref[...] = jax.nn.sigmoid(small)


def _odd_prep(qkv, small, conv_w, dt_bias, a_log):
    ta, cdim = qkv.shape
    n_tiles = ta // ROW_TILE
    kdim = GDN_QK_HEADS * GDN_DK
    vdim = GDN_V_HEADS * GDN_DV
    cw = jnp.zeros((8, cdim), F32).at[:CONV_W].set(conv_w)
    dtb = jnp.zeros((1, LANES), F32).at[0, :2 * SMALL].set(dt_bias.reshape(-1))
    aneg = jnp.zeros((1, LANES), F32).at[0, :2 * SMALL].set(-jnp.exp(a_log.reshape(-1)))
    prev_spec, next_spec = _halo_specs(cdim, n_tiles)
    row = lambda w: pl.BlockSpec((ROW_TILE, w), lambda i: (i, 0))
    full = lambda a: pl.BlockSpec(a.shape, lambda i: (0,) * a.ndim)
    return pl.pallas_call(
        functools.partial(_odd_prep_kernel, n_tiles=n_tiles),
        grid=(n_tiles,),
        in_specs=[row(cdim), prev_spec, next_spec, row(LANES), full(cw), full(dtb), full(aneg)],
        out_specs=[row(kdim), row(kdim), row(vdim), row(LANES), row(LANES)],
        out_shape=[jax.ShapeDtypeStruct((ta, w), F32) for w in (kdim, kdim, vdim, LANES, LANES)],
        scratch_shapes=[pltpu.VMEM((ROW_TILE + 2 * HALO, cdim), F32)],
        compiler_params=_cparams(("arbitrary",)),
        name="odd_prep",
    )(qkv, qkv, qkv, small, cw, dtb, aneg)


GDN_SCAN_INS = 5


def _gdn_kernel(*refs):
    ins = refs[:2 * GDN_SCAN_INS]
    (tri_ref, expand_ref, eye_ref, trit_ref, strict_ref, pair_ref, lvl_ref, eye2_ref,
     of_ref, ob_ref, st_ref) = refs[2 * GDN_SCAN_INS:]

    @pl.when(pl.program_id(0) == 0)
    def _():
        st_ref[...] = jnp.zeros(st_ref.shape, F32)

    pair = pair_ref[...]
    eye2 = eye2_ref[...]
    left = lax.broadcasted_iota(jnp.int32, (CHUNK, 2 * CHUNK), 1) < CHUNK
    n_levels = lvl_ref.shape[0]
    dirs = range(2)
    kk_qk = {}
    for d in dirs:
        q_ref, k_ref = ins[GDN_SCAN_INS * d:GDN_SCAN_INS * d + 2]
        for j in range(GDN_QK_HEADS):
            ksl = slice(j * GDN_DK, (j + 1) * GDN_DK)
            kb = k_ref[:, ksl].astype(BF16)
            krep = jnp.concatenate([kb, kb], axis=0)
            kk_qk[d, j] = (lax.dot_general(kb, krep, NT_DIMS, preferred_element_type=F32),
                           lax.dot_general(q_ref[:, ksl].astype(BF16), krep, NT_DIMS, preferred_element_type=F32))
    cols, decays = _decay_tables([tri_ref[d] for d in dirs], expand_ref[...], eye_ref[...],
                                 [trit_ref[d] for d in dirs],
                                 [ins[GDN_SCAN_INS * d + 3][:, d * SMALL:(d + 1) * SMALL] for d in dirs])
    beta_ts = [_sel_dot_r(ins[GDN_SCAN_INS * d + 4][:, (2 + d) * SMALL:(3 + d) * SMALL], expand_ref[...])
               for d in dirs]
    a_bds, rhs_all = [], []
    atts, qds, kds, gls = {}, {}, {}, {}
    for d in dirs:
        q_ref, k_ref, v_ref = ins[GDN_SCAN_INS * d:GDN_SCAN_INS * d + 3]
        last_row = 0 if d else CHUNK - 1
        col, decay, beta_t = cols[d], decays[d], beta_ts[d]
        a_all = beta_t * decay * strict_ref[d]
        for j in range(GDN_QK_HEADS):
            ksl = slice(j * GDN_DK, (j + 1) * GDN_DK)
            psl = slice(j * 2 * CHUNK, (j + 1) * 2 * CHUNK)
            kk, qk = kk_qk[d, j]
            atts[d, j] = (qk * decay[:, psl]).astype(BF16)
            a_pair = kk * a_all[:, psl]
            a_bds.append(jnp.concatenate([a_pair, a_pair], axis=0) * pair)
            colp, betap = col[:, psl], beta_t[:, psl]
            colr, betar = pltpu.roll(colp, CHUNK, 1), pltpu.roll(betap, CHUNK, 1)
            kj = k_ref[:, ksl]
            rhs = []
            for hh in range(2):
                hsl = slice((2 * j + hh) * GDN_DV, (2 * j + hh + 1) * GDN_DV)
                cumx = jnp.where(left, colp, colr) if hh == 0 else jnp.where(left, colr, colp)
                betax = jnp.where(left, betap, betar) if hh == 0 else jnp.where(left, betar, betap)
                lastx = cumx[last_row:last_row + 1, :]
                egx = jnp.exp(cumx)
                qds[d, 2 * j + hh] = (q_ref[:, ksl] * egx).astype(BF16)
                kds[d, 2 * j + hh] = (kj * jnp.exp(lastx - cumx)).astype(BF16)
                gls[d, 2 * j + hh] = jnp.exp(lastx)
                rhs.append(jnp.concatenate([v_ref[:, hsl] * betax, kj * betax * egx], axis=1))
            rhs_all.append(jnp.concatenate(rhs, axis=0))
    tinvs = [eye2 - a * lvl_ref[0] for a in a_bds]
    for lv in range(1, n_levels):
        tls = [_bdot(t, a * lvl_ref[lv]) for t, a in zip(tinvs, a_bds)]
        tinvs = [t - _bdot(tl, t) for t, tl in zip(tinvs, tls)]
    xs = [_bdot(t, r) for t, r in zip(tinvs, rhs_all)]
    chains = [(d, h) for d in dirs for h in range(GDN_V_HEADS)]
    o_refs = (of_ref, ob_ref)
    head_rows = lambda x, h: x[(h % 2) * CHUNK:(h % 2 + 1) * CHUNK]
    x_of = lambda d, h: head_rows(xs[d * GDN_QK_HEADS + h // 2], h)
    ws_qs = [jnp.dot(jnp.concatenate([x_of(d, h)[:, GDN_DV:].astype(BF16), qds[d, h]], axis=0),
                     st_ref[d, h].astype(BF16), preferred_element_type=F32) for d, h in chains]
    v_new = [(x_of(d, h)[:, :GDN_DV] - r[:CHUNK]).astype(BF16) for (d, h), r in zip(chains, ws_qs)]
    for (d, h), r, vn in zip(chains, ws_qs, v_new):
        att = atts[d, h // 2][:, (h % 2) * CHUNK:(h % 2 + 1) * CHUNK]
        o_refs[d][:, h * GDN_DV:(h + 1) * GDN_DV] = (r[CHUNK:] + jnp.dot(att, vn, preferred_element_type=F32)
                                                     ).astype(o_refs[d].dtype)
    for (d, h), vn in zip(chains, v_new):
        st_ref[d, h] = st_ref[d, h] * gls[d, h] + lax.dot_general(kds[d, h], vn, TN_DIMS,
                                                                  preferred_element_type=F32)


def _gdn_scan(q, k, v, g, beta, n_ctx_chunks):
    ta, kdim = q.shape
    vdim = v.shape[1]
    nc = ta // CHUNK
    fwd = _scan_consts(False, GDN_V_HEADS, CHUNK)
    bwd = _scan_consts(True, GDN_V_HEADS, CHUNK)
    both = lambda name: jnp.stack([fwd[name], bwd[name]])
    consts = [both("tri3"), fwd["expand3"], fwd["eye_t"], both("tri_t"), both("strict_t"),
              _pair_mask(), _level_masks(), jnp.eye(2 * CHUNK, dtype=F32)]
    full = lambda a: pl.BlockSpec(a.shape, lambda i: (0,) * a.ndim)

    def row(rev, w):
        cmap = _chunk_map(rev, nc, n_ctx_chunks)
        return pl.BlockSpec((CHUNK, w), lambda i: (cmap(i), 0))

    dir_specs = lambda rev: [row(rev, kdim), row(rev, kdim), row(rev, vdim), row(rev, LANES), row(rev, LANES)]
    return pl.pallas_call(
        _gdn_kernel,
        grid=(nc,),
        in_specs=dir_specs(False) + dir_specs(True) + [full(a) for a in consts],
        out_specs=[row(False, vdim), row(True, vdim)],
        out_shape=[jax.ShapeDtypeStruct((ta, vdim), BF16)] * 2,
        scratch_shapes=[pltpu.VMEM((2, GDN_V_HEADS, GDN_DK, GDN_DV), F32)],
        compiler_params=_cparams(("arbitrary",)),
        name="gdn_scan",
    )(q, k, v, g, beta, q, k, v, g, beta, *consts)


def _odd_merge_kernel(of_ref, ob_ref, z_ref, nw_ref, w_ref, o_ref):
    f32 = lambda ref: ref[...].astype(F32)
    o = _group_rms(f32(of_ref) + f32(ob_ref), nw_ref[...], GDN_DV) * _silu(f32(z_ref))
    o_ref[...] = _bdot(o, w_ref[...])


def _odd_merge(of, ob, z, gdn_nw, w_out):
    ta, vdim = of.shape
    d = w_out.shape[1]
    nw = jnp.tile(gdn_nw, GDN_V_HEADS).reshape(1, vdim)
    row = lambda w: pl.BlockSpec((ROW_TILE, w), lambda i: (i, 0))
    return pl.pallas_call(
        _odd_merge_kernel,
        grid=(ta // ROW_TILE,),
        in_specs=[row(vdim)] * 3 + [pl.BlockSpec((1, vdim), lambda i: (0, 0)), pl.BlockSpec(w_out.shape, lambda i: (0, 0))],
        out_specs=row(d),
        out_shape=jax.ShapeDtypeStruct((ta, d), F32),
        compiler_params=_cparams(("arbitrary",)),
        name="odd_merge",
    )(of, ob, z, nw, w_out)


def _ffn_kernel(x_ref, y_ref, mod_ref, nw_ref, w1_ref, w3_ref, w2_ref, fw_ref, o_ref, *, n_ctx_tiles, final):
    d = x_ref.shape[1]
    is_ctx = pl.program_id(0) < n_ctx_tiles
    x = x_ref[...] + _mod_rows(mod_ref, is_ctx, 2, d) * y_ref[...]
    h = _norm_mod(x, nw_ref[...], _mod_rows(mod_ref, is_ctx, 3, d), _mod_rows(mod_ref, is_ctx, 4, d)).astype(BF16)
    a = jnp.dot(h, w1_ref[...], preferred_element_type=F32)
    b = jnp.dot(h, w3_ref[...], preferred_element_type=F32)
    out = x + _mod_rows(mod_ref, is_ctx, 5, d) * _bdot(_silu(a) * b, w2_ref[...])
    if final:
        ms = jnp.mean(out * out, axis=-1, keepdims=True)
        out = out * lax.rsqrt(ms + EPS) * fw_ref[...]
    o_ref[...] = out


def _ffn(x_all, y_all, mod, nw, w1, w3, w2, final_w, n_ctx_tiles, final):
    ta, d = x_all.shape
    row = pl.BlockSpec((ROW_TILE, d), lambda i: (i, 0))
    full = lambda a: pl.BlockSpec(a.shape, lambda i: (0,) * a.ndim)
    nw = nw.reshape(1, d)
    fw = final_w.reshape(1, d)
    return pl.pallas_call(
        functools.partial(_ffn_kernel, n_ctx_tiles=n_ctx_tiles, final=final),
        grid=(ta // ROW_TILE,),
        in_specs=[row, row, full(mod), full(nw), full(w1), full(w3), full(w2), full(fw)],
        out_specs=row,
        out_shape=jax.ShapeDtypeStruct((ta, d), F32),
        compiler_params=_cparams(("arbitrary",)),
        name="ffn_final" if final else "ffn",
    )(x_all, y_all, mod, nw, w1, w3, w2, fw)


def _pad_cols(parts, width):
    w = jnp.concatenate(parts, axis=1)
    return jnp.pad(w, ((0, 0), (0, width - w.shape[1])))


def _even_w_in(w):
    inner = SSD_HEADS * SSD_HEAD_DIM
    conv = inner + 2 * SSD_GROUPS * SSD_STATE
    gkey = GLA_HEADS * GLA_DK
    gval = GLA_HEADS * GLA_DV
    sizes = (inner, conv, SMALL, SMALL, gkey, gkey, gval, gval, SMALL, SMALL)
    offs = [0]
    for s in sizes:
        offs.append(offs[-1] + s)
    z, xbc, dtf, dtb, q, k, v, r, glf, glb = (w[:, offs[i]:offs[i + 1]] for i in range(len(sizes)))
    small = _pad_cols([dtf, dtb, glf, glb], LANES)
    widths = (inner, conv, gkey, gkey, gval, gval, LANES)
    dtypes = (BF16, F32, F32, F32, F32, BF16, F32)
    return jnp.concatenate([z, xbc, q, k, v, r, small], axis=1).astype(BF16), widths, dtypes


def _odd_w_in(w):
    kdim = GDN_QK_HEADS * GDN_DK
    vdim = GDN_V_HEADS * GDN_DV
    conv = 2 * kdim + vdim
    small = _pad_cols([w[:, conv + vdim:]], LANES)
    widths = (conv, vdim, LANES)
    dtypes = (F32, BF16, F32)
    return jnp.concatenate([w[:, :conv + vdim], small], axis=1).astype(BF16), widths, dtypes


def _to_column_major(h):
    t, d = h.shape
    return h.reshape(t // GRID_W, GRID_W, d).transpose(1, 0, 2).reshape(t, d)


def _from_column_major(h):
    t, d = h.shape
    return h.reshape(GRID_W, t // GRID_W, d).transpose(1, 0, 2).reshape(t, d)


def kernel(x, c, ctx, c_ctx, mod_w, mod_b, norm_mix, norm_ffn, ffn_w1, ffn_w3, ffn_w2, ev_w_in, ssd_conv_w, ssd_conv_b, ssd_dt_bias, ssd_a_log, ssd_d, ssd_norm, gla_gate_w, gla_gate_b, gla_norm, ev_w_out, od_w_in, gdn_conv_w, gdn_dt_bias, gdn_a_log, gdn_norm, od_w_out, final_norm):
    batch, seq, d = x.shape
    n_ctx = ctx.shape[1]
    assert batch == 1 and n_ctx == ROW_TILE and seq % ROW_TILE == 0 and seq % GRID_W == 0
    assert mod_w.shape[0] == 2, "one even (SSD + GLA) layer followed by one odd (gated DeltaNet) layer"
    n_ctx_tiles = n_ctx // ROW_TILE
    n_ctx_chunks = n_ctx // CHUNK

    mods = _modulation(c, c_ctx, mod_w, mod_b)
    x_all = jnp.concatenate([ctx[0], x[0]], axis=0)

    w_in, widths, dtypes = _even_w_in(ev_w_in[0])
    z, xbc, q, k, v, r, small = _norm_proj(x_all, norm_mix[0], mods[0], w_in, widths, dtypes, n_ctx_tiles)
    xs, bm, cm, dt, da, gk_f, gk_b = _even_prep(xbc, small, ssd_conv_w[0], ssd_conv_b[0], ssd_dt_bias[0],
                                                ssd_a_log[0], gla_gate_w[0], gla_gate_b[0])
    y_f, y_b = _ssd_scan(xs, bm, cm, dt, da, n_ctx_chunks)
    o_f, o_b = _gla_scan(q, k, v, gk_f, gk_b, n_ctx_chunks)
    mixed = _even_merge(y_f, y_b, xs, z, o_f, o_b, r, ssd_d[0], ssd_norm[0], gla_norm[0], ev_w_out[0].astype(BF16))
    x_all = _ffn(x_all, mixed, mods[0], norm_ffn[0], ffn_w1[0].astype(BF16), ffn_w3[0].astype(BF16),
                 ffn_w2[0].astype(BF16), final_norm, n_ctx_tiles, False)

    x_lat = x_all[n_ctx:]
    x_perm = jnp.concatenate([x_all[:n_ctx], _to_column_major(x_lat)], axis=0)
    w_in, widths, dtypes = _odd_w_in(od_w_in[0])
    qkv, z, small = _norm_proj(x_perm, norm_mix[1], mods[1], w_in, widths, dtypes, n_ctx_tiles)
    q, k, v, g, beta = _odd_prep(qkv, small, gdn_conv_w[0], gdn_dt_bias[0], gdn_a_log[0])
    o_f, o_b = _gdn_scan(q, k, v, g, beta, n_ctx_chunks)
    mixed = _odd_merge(o_f, o_b, z, gdn_norm[0], od_w_out[0].astype(BF16))
    mixed = _from_column_major(mixed[n_ctx:])
    out = _ffn(x_lat, mixed, mods[1], norm_ffn[1], ffn_w1[1].astype(BF16), ffn_w3[1].astype(BF16),
               ffn_w2[1].astype(BF16), final_norm, 0, True)
    return out[None]
```

```python
import functools

import jax
import jax.numpy as jnp
from jax import lax
from jax.experimental import pallas as pl
from jax.experimental.pallas import tpu as pltpu

F32 = jnp.float32
BF16 = jnp.bfloat16
HIGHEST = lax.Precision.HIGHEST

EPS = 1e-6
LOG2_E = 1.4426950408889634
CHUNK = 64
SCAN_CPS = 4
GLA_CPS = 2
GRID_W = 64
CONV_W = 5
ROW_TILE = 256
HALO = 8
LANES = 128
VMEM_LIMIT_BYTES = 56 * 1024 * 1024

SSD_HEADS = 16
SSD_HEAD_DIM = 64
SSD_GROUPS = 2
SSD_STATE = 128
GLA_HEADS = 4
GLA_DK = 128
GLA_DV = 256
GLA_GATE_RANK = 16
GLA_GATE_NORM = 16.0
GDN_QK_HEADS = 8
GDN_V_HEADS = 16
GDN_DK = 128
GDN_DV = 128
SMALL = 16

NT_DIMS = (((1,), (1,)), ((), ()))
TN_DIMS = (((0,), (0,)), ((), ()))


def _cparams(sem):
    return pltpu.CompilerParams(dimension_semantics=sem, vmem_limit_bytes=VMEM_LIMIT_BYTES)


def _bdot(a, b):
    return jnp.dot(a.astype(BF16), b.astype(BF16), preferred_element_type=F32)


def _bdot_nt(a, b):
    return lax.dot_general(a.astype(BF16), b.astype(BF16), NT_DIMS, preferred_element_type=F32)


def _bdot_tn(a, b):
    return lax.dot_general(a.astype(BF16), b.astype(BF16), TN_DIMS, preferred_element_type=F32)


def _split3(x):
    hi = x.astype(BF16)
    r1 = x - hi.astype(F32)
    mid = r1.astype(BF16)
    lo = (r1 - mid.astype(F32)).astype(BF16)
    return hi, mid, lo


def _sel_dot_l(sel3, x):
    return jnp.dot(sel3, jnp.concatenate(_split3(x), axis=0), preferred_element_type=F32)


def _sel_dot_r(x, sel3):
    pieces = jnp.concatenate([p.astype(F32) for p in _split3(x)], axis=1)
    return jnp.dot(pieces.astype(BF16), sel3, preferred_element_type=F32)


def _silu(x):
    return x * jax.nn.sigmoid(x)


def _softplus(x):
    return jnp.maximum(x, 0.0) + jnp.log1p(jnp.exp(-jnp.abs(x)))


def _mod_kernel(c_ref, w_ref, b_ref, o_ref):
    s = _silu(c_ref[...])
    o_ref[0] = jnp.dot(s, w_ref[0], precision=HIGHEST, preferred_element_type=F32) + b_ref[0]


def _modulation(c, c_ctx, mod_w, mod_b):
    depth, d, n = mod_w.shape
    cs = jnp.zeros((8, d), F32).at[0].set(c[0]).at[1].set(c_ctx)
    tn = n // 4
    return pl.pallas_call(
        _mod_kernel,
        grid=(depth, n // tn),
        in_specs=[pl.BlockSpec((8, d), lambda i, j: (0, 0)),
                  pl.BlockSpec((1, d, tn), lambda i, j: (i, 0, j)),
                  pl.BlockSpec((1, 1, tn), lambda i, j: (i, 0, j))],
        out_specs=pl.BlockSpec((1, 8, tn), lambda i, j: (i, 0, j)),
        out_shape=jax.ShapeDtypeStruct((depth, 8, n), F32),
        compiler_params=_cparams(("arbitrary", "arbitrary")),
        name="modulation",
    )(cs, mod_w, mod_b.reshape(depth, 1, n))


def _mod_rows(mod_ref, is_ctx, idx, d):
    sl = slice(idx * d, (idx + 1) * d)
    return jnp.where(is_ctx, mod_ref[1:2, sl], mod_ref[0:1, sl])


def _norm_mod(x, nw, shift, scale):
    ms = jnp.mean(x * x, axis=-1, keepdims=True)
    return (x * lax.rsqrt(ms + EPS) * nw) * (1.0 + scale) + shift


def _ctx_lat_specs(n_ctx_tiles, d):
    ctx = pl.BlockSpec((ROW_TILE, d), lambda i: (jnp.minimum(i, n_ctx_tiles - 1), 0))
    lat = pl.BlockSpec((ROW_TILE, d), lambda i: (jnp.maximum(i - n_ctx_tiles, 0), 0))
    return ctx, lat


def _proj_kernel(xc_ref, xl_ref, nw_ref, mod_ref, w_ref, *out_refs, widths, n_ctx_tiles):
    d = xl_ref.shape[1]
    is_ctx = pl.program_id(0) < n_ctx_tiles
    x = jnp.where(is_ctx, xc_ref[...], xl_ref[...])
    h = _norm_mod(x, nw_ref[...], _mod_rows(mod_ref, is_ctx, 0, d), _mod_rows(mod_ref, is_ctx, 1, d))
    h = h.astype(BF16)
    off = 0
    for o_ref, wd in zip(out_refs, widths):
        o_ref[...] = jnp.dot(h, w_ref[:, off:off + wd], preferred_element_type=F32).astype(o_ref.dtype)
        off += wd


def _norm_proj(x_ctx, x_lat, nw, mod, w, widths, dtypes):
    d = x_lat.shape[1]
    n_ctx_tiles = x_ctx.shape[0] // ROW_TILE
    ta = x_ctx.shape[0] + x_lat.shape[0]
    ntot = sum(widths)
    kern = functools.partial(_proj_kernel, widths=widths, n_ctx_tiles=n_ctx_tiles)
    return pl.pallas_call(
        kern,
        grid=(ta // ROW_TILE,),
        in_specs=[*_ctx_lat_specs(n_ctx_tiles, d),
                  pl.BlockSpec((1, d), lambda i: (0, 0)),
                  pl.BlockSpec(mod.shape, lambda i: (0, 0)),
                  pl.BlockSpec((d, ntot), lambda i: (0, 0))],
        out_specs=[pl.BlockSpec((ROW_TILE, wd), lambda i: (i, 0)) for wd in widths],
        out_shape=[jax.ShapeDtypeStruct((ta, wd), dt) for wd, dt in zip(widths, dtypes)],
        compiler_params=_cparams(("arbitrary",)),
        name="norm_proj",
    )(x_ctx, x_lat, nw.reshape(1, d), mod, w)


def _halo_specs(width, n_tiles):
    per = ROW_TILE // HALO
    prev = pl.BlockSpec((HALO, width), lambda i: (jnp.maximum(i * per - 1, 0), 0))
    nxt = pl.BlockSpec((HALO, width), lambda i: (jnp.minimum((i + 1) * per, n_tiles * per - 1), 0))
    return prev, nxt


def _conv_silu(cur_ref, prev_ref, next_ref, w_ref, bias, ext_ref, n_tiles):
    i = pl.program_id(0)
    prev_ok = jnp.logical_and(i != 0, i != 1)
    next_ok = jnp.logical_and(i != 0, i != n_tiles - 1)
    ext_ref[0:HALO, :] = jnp.where(prev_ok, prev_ref[...], 0.0)
    ext_ref[HALO:HALO + ROW_TILE, :] = cur_ref[...]
    ext_ref[HALO + ROW_TILE:, :] = jnp.where(next_ok, next_ref[...], 0.0)
    pad = CONV_W // 2
    acc = None
    for j in range(CONV_W):
        term = ext_ref[pl.ds(HALO - pad + j, ROW_TILE), :] * w_ref[j:j + 1, :]
        acc = term if acc is None else acc + term
    if bias is not None:
        acc = acc + bias
    return _silu(acc)


def _even_prep_kernel(xbc_ref, prev_ref, next_ref, small_ref, cw_ref, cb_ref, dtb_ref, aneg_ref,
                      gw_ref, gb_ref, xs_ref, bm_ref, cm_ref, dt_ref, da_ref, gkf_ref, gkb_ref,
                      ext_ref, *, n_tiles):
    y = _conv_silu(xbc_ref, prev_ref, next_ref, cw_ref, cb_ref[...], ext_ref, n_tiles)
    inner = xs_ref.shape[1]
    gs = bm_ref.shape[1]
    xs_ref[...] = y[:, :inner]
    bm_ref[...] = y[:, inner:inner + gs]
    cm_ref[...] = y[:, inner + gs:]
    small = small_ref[...]
    dt = _softplus(small + dtb_ref[...])
    dt_ref[...] = dt
    da_ref[...] = dt * aneg_ref[...]
    for d, o_ref in enumerate((gkf_ref, gkb_ref)):
        pre = jnp.dot(small, gw_ref[d], precision=HIGHEST, preferred_element_type=F32) + gb_ref[d]
        o_ref[...] = -_softplus(-pre) * (1.0 / GLA_GATE_NORM)


def _even_prep(xbc, small, conv_w, conv_b, dt_bias, a_log, gate_w, gate_b):
    ta, cdim = xbc.shape
    n_tiles = ta // ROW_TILE
    inner = SSD_HEADS * SSD_HEAD_DIM
    gs = SSD_GROUPS * SSD_STATE
    gkey = GLA_HEADS * GLA_DK
    cw = jnp.zeros((8, cdim), F32).at[:CONV_W].set(conv_w)
    dtb = jnp.zeros((1, LANES), F32).at[0, :2 * SMALL].set(dt_bias.reshape(-1))
    aneg = jnp.zeros((1, LANES), F32).at[0, :2 * SMALL].set(-jnp.exp(a_log.reshape(-1)))
    gw = jnp.zeros((2, LANES, gkey), F32)
    gw = gw.at[0, 2 * SMALL:3 * SMALL].set(gate_w[0]).at[1, 3 * SMALL:4 * SMALL].set(gate_w[1])
    prev_spec, next_spec = _halo_specs(cdim, n_tiles)
    row = lambda w: pl.BlockSpec((ROW_TILE, w), lambda i: (i, 0))
    full = lambda a: pl.BlockSpec(a.shape, lambda i: (0,) * a.ndim)
    gb = gate_b.reshape(2, 1, gkey)
    return pl.pallas_call(
        functools.partial(_even_prep_kernel, n_tiles=n_tiles),
        grid=(n_tiles,),
        in_specs=[row(cdim), prev_spec, next_spec, row(LANES), full(cw), pl.BlockSpec((1, cdim), lambda i: (0, 0)),
                  full(dtb), full(aneg), full(gw), full(gb)],
        out_specs=[row(inner), row(gs), row(gs), row(LANES), row(LANES), row(gkey), row(gkey)],
        out_shape=[jax.ShapeDtypeStruct((ta, w), F32) for w in (inner, gs, gs, LANES, LANES, gkey, gkey)],
        scratch_shapes=[pltpu.VMEM((ROW_TILE + 2 * HALO, cdim), F32)],
        compiler_params=_cparams(("arbitrary",)),
        name="even_prep",
    )(xbc, xbc, xbc, small, cw, conv_b.reshape(1, cdim), dtb, aneg, gw, gb)


def _scan_consts(rev, heads, lanes_per_head):
    r = jnp.arange(CHUNK)
    tri = (r[:, None] <= r[None, :]) if rev else (r[:, None] >= r[None, :])
    strict = (r[:, None] < r[None, :]) if rev else (r[:, None] > r[None, :])
    eye = r[:, None] == r[None, :]
    expand = (jnp.arange(heads)[:, None] == (jnp.arange(heads * lanes_per_head)[None, :] // lanes_per_head))
    return dict(
        tri3=jnp.tile(tri, (1, 3)).astype(BF16),
        expand3=jnp.tile(expand, (3, 1)).astype(BF16),
        tri_t=jnp.tile(tri, (1, heads)).astype(F32),
        strict_t=jnp.tile(strict, (1, heads)).astype(F32),
        eye_t=jnp.tile(eye, (1, heads)).astype(F32),
    )


def _pair_mask():
    r = jnp.arange(2 * CHUNK)
    return ((r[:, None] // CHUNK) == (r[None, :] // CHUNK)).astype(F32)


def _level_masks():
    r = jnp.arange(2 * CHUNK)
    same = lambda b: (r[:, None] // b) == (r[None, :] // b)
    sizes = [2 ** m for m in range(CHUNK.bit_length() - 1)]
    return jnp.stack([jnp.logical_and(same(2 * b), jnp.logical_not(same(b))) for b in sizes]).astype(F32)


def _chunk_map(rev, n_chunks, n_ctx_chunks):
    if not rev:
        return lambda i: i
    return lambda i: jnp.where(i < n_ctx_chunks, n_ctx_chunks - 1 - i, n_chunks + n_ctx_chunks - 1 - i)


def _decay_tables(tri3s, expand3, eye_t, tri_ts, gs):
    cums = [_sel_dot_l(tri3, g) * LOG2_E for tri3, g in zip(tri3s, gs)]
    cols = [_sel_dot_r(cum, expand3) for cum in cums]
    rowfs = [jnp.sum(col * eye_t, axis=0, keepdims=True) for col in cols]
    decays = [jnp.where(tri_t > 0.5, jnp.exp2(col - rowf), 0.0) for tri_t, col, rowf in zip(tri_ts, cols, rowfs)]
    return cols, decays


SSD_SCAN_INS = 5


def _ssd_kernel(*refs, cps):
    ins = refs[:2 * SSD_SCAN_INS]
    tri_ref, expand_ref, eye_ref, trit_ref, pair_ref, yf_ref, yb_ref, st_ref = refs[2 * SSD_SCAN_INS:]

    @pl.when(pl.program_id(0) == 0)
    def _():
        st_ref[...] = jnp.zeros(st_ref.shape, F32)

    pair = pair_ref[...]
    heads_g = SSD_HEADS // SSD_GROUPS
    gw = heads_g * SSD_HEAD_DIM
    pieces = [(d, c, g) for d in range(2) for c in range(cps) for g in range(SSD_GROUPS)]
    gsl = lambda g: slice(g * SSD_STATE, (g + 1) * SSD_STATE)
    lanes = lambda g: slice(g * gw, (g + 1) * gw)
    rows = lambda d, c: slice((cps - 1 - c if d else c) * CHUNK, (cps - c if d else c + 1) * CHUNK)
    chunks = [(d, c) for d in range(2) for c in range(cps)]
    small = lambda d, c, k: ins[SSD_SCAN_INS * d + k][rows(d, c), d * SMALL:(d + 1) * SMALL]
    cols, decays = _decay_tables([tri_ref[d] for d, _ in chunks], expand_ref[...], eye_ref[...],
                                 [trit_ref[d] for d, _ in chunks], [small(d, c, 4) for d, c in chunks])
    dtes = [_sel_dot_r(small(d, c, 3), expand_ref[...]) for d, c in chunks]
    tabs = {}
    for (d, c), col, decay, dte in zip(chunks, cols, decays, dtes):
        last_row = 0 if d else CHUNK - 1
        xdt = ins[SSD_SCAN_INS * d][rows(d, c), :] * dte
        last = col[last_row:last_row + 1, :]
        tabs[d, c] = dict(decay=decay, xdt=xdt, xw=xdt * jnp.exp2(last - col), chunk_decay=jnp.exp2(last),
                          eac=jnp.exp2(col))
    bgs = {(d, c, g): ins[SSD_SCAN_INS * d + 1][rows(d, c), gsl(g)].astype(BF16) for d, c, g in pieces}
    cgs = {(d, c, g): ins[SSD_SCAN_INS * d + 2][rows(d, c), gsl(g)].astype(BF16) for d, c, g in pieces}
    wms = {p: lax.dot_general(cgs[p], jnp.concatenate([bgs[p]] * heads_g, axis=0), NT_DIMS,
                              preferred_element_type=F32) * tabs[p[:2]]["decay"][:, lanes(p[2])] for p in pieces}
    news = {(d, c, g): lax.dot_general(bgs[d, c, g], tabs[d, c]["xw"][:, lanes(g)].astype(BF16), TN_DIMS,
                                       preferred_element_type=F32) for d, c, g in pieces}
    diags = {}
    for d, c, g in pieces:
        xg = tabs[d, c]["xdt"][:, lanes(g)]
        for j in range(gw // LANES):
            xp = xg[:, j * LANES:(j + 1) * LANES]
            blockdiag = jnp.concatenate([xp, xp], axis=0) * pair
            diags[d, c, g, j] = _bdot(wms[d, c, g][:, j * LANES:(j + 1) * LANES], blockdiag)
    states = {}
    for d in range(2):
        for g in range(SSD_GROUPS):
            state = st_ref[d, g]
            for c in range(cps):
                states[d, c, g] = state
                state = state * tabs[d, c]["chunk_decay"][:, lanes(g)] + news[d, c, g]
            st_ref[d, g] = state
    offs = {p: jnp.dot(cgs[p], states[p].astype(BF16), preferred_element_type=F32) for p in pieces}
    for d, y_ref in enumerate((yf_ref, yb_ref)):
        for c in range(cps):
            for g in range(SSD_GROUPS):
                diag = jnp.concatenate([diags[d, c, g, j] for j in range(gw // LANES)], axis=1)
                y_ref[rows(d, c), lanes(g)] = (diag + offs[d, c, g] * tabs[d, c]["eac"][:, lanes(g)]).astype(y_ref.dtype)


def _ssd_scan(xs, bm, cm, dt, da, n_ctx_chunks):
    ta, inner = xs.shape
    cps = SCAN_CPS
    nb = ta // (cps * CHUNK)
    fwd = _scan_consts(False, SSD_HEADS, SSD_HEAD_DIM)
    bwd = _scan_consts(True, SSD_HEADS, SSD_HEAD_DIM)
    both = lambda name: jnp.stack([fwd[name], bwd[name]])
    consts = [both("tri3"), fwd["expand3"], fwd["eye_t"], both("tri_t"), _pair_mask()]
    full = lambda a: pl.BlockSpec(a.shape, lambda i: (0,) * a.ndim)
    gs = SSD_GROUPS * SSD_STATE

    def row(rev, w):
        cmap = _chunk_map(rev, nb, n_ctx_chunks // cps)
        return pl.BlockSpec((cps * CHUNK, w), lambda i: (cmap(i), 0))

    dir_specs = lambda rev: [row(rev, inner), row(rev, gs), row(rev, gs), row(rev, LANES), row(rev, LANES)]
    return pl.pallas_call(
        functools.partial(_ssd_kernel, cps=cps),
        grid=(nb,),
        in_specs=dir_specs(False) + dir_specs(True) + [full(a) for a in consts],
        out_specs=[row(False, inner), row(True, inner)],
        out_shape=[jax.ShapeDtypeStruct((ta, inner), BF16)] * 2,
        scratch_shapes=[pltpu.VMEM((2, SSD_GROUPS, SSD_STATE, inner // SSD_GROUPS), F32)],
        compiler_params=_cparams(("arbitrary",)),
        name="ssd_scan",
    )(xs, bm, cm, dt, da, xs, bm, cm, dt, da, *consts)


GLA_SUB = 8
GLA_LEVELS = (8, 16, 32)
GLA_SCAN_INS = 4


def _gla_masks():
    r = jnp.arange(CHUNK)
    same = lambda b: (r[:, None] // b) == (r[None, :] // b)
    out = []
    for rev in (False, True):
        strict = (r[:, None] < r[None, :]) if rev else (r[:, None] > r[None, :])
        lv = [same(2 * b) & ~same(b) & strict for b in GLA_LEVELS]
        within = r % GLA_SUB
        dg = []
        for s in range(GLA_SUB):
            seen = (within <= s) if rev else (within >= s)
            dg.append((r[None, :] == (r[:, None] // GLA_SUB) * GLA_SUB + s) & seen[:, None])
        out.append(jnp.stack(lv + dg))
    return jnp.stack(out).astype(F32)


def _block_rows(a, rows, b):
    return jnp.concatenate([jnp.broadcast_to(a[r:r + 1, :], (b, a.shape[1])) for r in rows], axis=0)


def _gla_kernel(*refs, cps):
    ins = refs[:2 * GLA_SCAN_INS]
    tri_ref, mask_ref, of_ref, ob_ref, st_ref = refs[2 * GLA_SCAN_INS:]

    @pl.when(pl.program_id(0) == 0)
    def _():
        st_ref[...] = jnp.zeros(st_ref.shape, F32)

    pieces = [(d, c, h) for d in range(2) for c in range(cps) for h in range(GLA_HEADS)]
    ksl = lambda h: slice(h * GLA_DK, (h + 1) * GLA_DK)
    vsl = lambda h: slice(h * GLA_DV, (h + 1) * GLA_DV)
    rws = lambda d, c: slice((cps - 1 - c if d else c) * CHUNK, (cps - c if d else c + 1) * CHUNK)
    q_of = lambda d, c, h: ins[GLA_SCAN_INS * d][rws(d, c), ksl(h)] * (GLA_DK ** -0.5)
    k_of = lambda d, c, h: ins[GLA_SCAN_INS * d + 1][rws(d, c), ksl(h)]
    v_of = lambda d, c, h: ins[GLA_SCAN_INS * d + 2][rws(d, c), vsl(h)]
    within = lax.broadcasted_iota(jnp.int32, (CHUNK, 1), 0) % GLA_SUB
    gcs = {(d, c, h): _sel_dot_l(tri_ref[d], ins[GLA_SCAN_INS * d + 3][rws(d, c), ksl(h)]) * LOG2_E
           for d, c, h in pieces}
    news = {}
    for d, c, h in pieces:
        gc = gcs[d, c, h]
        last = 0 if d else CHUNK - 1
        glast = gc[last:last + 1, :]
        news[d, c, h] = (jnp.exp2(glast), _bdot_tn(v_of(d, c, h), k_of(d, c, h) * jnp.exp2(glast - gc)))
    states = {}
    for d in range(2):
        for h in range(GLA_HEADS):
            state = st_ref[d, h]
            for c in range(cps):
                states[d, c, h] = state
                state = state * news[d, c, h][0] + news[d, c, h][1]
            st_ref[d, h] = state
    inters = {p: _bdot_nt(q_of(*p) * jnp.exp2(gcs[p]), states[p]) for p in pieces}
    atts = {}
    for d, c, h in pieces:
        gc, qh, kh = gcs[d, c, h], q_of(d, c, h), k_of(d, c, h)
        att = None
        for li, b in enumerate(GLA_LEVELS):
            nb = CHUNK // b
            if d:
                rq = [min(b * (i + 1), CHUNK - 1) for i in range(nb)]
                rk = [b * i for i in range(nb)]
            else:
                rq = [max(b * i - 1, 0) for i in range(nb)]
                rk = [b * (i + 1) - 1 for i in range(nb)]
            qs = qh * jnp.exp2(gc - _block_rows(gc, rq, b))
            ks = kh * jnp.exp2(_block_rows(gc, rk, b) - gc)
            part = _bdot_nt(qs, ks) * mask_ref[d, li]
            att = part if att is None else att + part
        for s in range(GLA_SUB):
            rows = [GLA_SUB * i + s for i in range(CHUNK // GLA_SUB)]
            seen = (within <= s) if d else (within >= s)
            e = jnp.where(seen, jnp.exp2(gc - _block_rows(gc, rows, GLA_SUB)), 0.0)
            a = jnp.sum(qh * _block_rows(kh, rows, GLA_SUB) * e, axis=-1, keepdims=True)
            att = att + a * mask_ref[d, len(GLA_LEVELS) + s]
        atts[d, c, h] = att
    for d, o_ref in enumerate((of_ref, ob_ref)):
        for c in range(cps):
            for h in range(GLA_HEADS):
                o_ref[rws(d, c), vsl(h)] = (inters[d, c, h] + _bdot(atts[d, c, h], v_of(d, c, h))).astype(o_ref.dtype)


def _gla_scan(q, k, v, gk_f, gk_b, n_ctx_chunks):
    ta, kdim = q.shape
    vdim = v.shape[1]
    cps = GLA_CPS
    nb = ta // (cps * CHUNK)
    tri3 = jnp.stack([_scan_consts(rev, 1, 1)["tri3"] for rev in (False, True)])
    masks = _gla_masks()
    full = lambda a: pl.BlockSpec(a.shape, lambda i: (0,) * a.ndim)

    def row(rev, w):
        cmap = _chunk_map(rev, nb, n_ctx_chunks // cps)
        return pl.BlockSpec((cps * CHUNK, w), lambda i: (cmap(i), 0))

    dir_specs = lambda rev: [row(rev, kdim), row(rev, kdim), row(rev, vdim), row(rev, kdim)]
    return pl.pallas_call(
        functools.partial(_gla_kernel, cps=cps),
        grid=(nb,),
        in_specs=dir_specs(False) + dir_specs(True) + [full(tri3), full(masks)],
        out_specs=[row(False, vdim), row(True, vdim)],
        out_shape=[jax.ShapeDtypeStruct((ta, vdim), BF16)] * 2,
        scratch_shapes=[pltpu.VMEM((2, GLA_HEADS, GLA_DV, GLA_DK), F32)],
        compiler_params=_cparams(("arbitrary",)),
        name="gla_scan",
    )(q, k, v, gk_f, q, k, v, gk_b, tri3, masks)


def _group_rms(y, w, width):
    parts = []
    for g in range(y.shape[1] // width):
        yg = y[:, g * width:(g + 1) * width]
        ms = jnp.mean(yg * yg, axis=-1, keepdims=True)
        parts.append(yg * lax.rsqrt(ms + EPS))
    return jnp.concatenate(parts, axis=1) * w


def _even_merge_kernel(yf_ref, yb_ref, xs_ref, z_ref, of_ref, ob_ref, r_ref, dsk_ref, snw_ref, gnw_ref, w_ref, o_ref):
    inner = xs_ref.shape[1]
    f32 = lambda ref: ref[...].astype(F32)
    y = (f32(yf_ref) + f32(yb_ref) + dsk_ref[...] * xs_ref[...]) * _silu(f32(z_ref))
    y = _group_rms(y, snw_ref[...], inner // SSD_GROUPS)
    o = _group_rms(f32(of_ref) + f32(ob_ref), gnw_ref[...], GLA_DV) * _silu(f32(r_ref))
    o_ref[...] = _bdot(y, w_ref[:inner, :]) + _bdot(o, w_ref[inner:, :])


def _even_merge(yf, yb, xs, z, of, ob, r, d_skip, ssd_nw, gla_nw, w_out):
    ta, inner = xs.shape
    vdim = of.shape[1]
    d = w_out.shape[1]
    dsk = jnp.repeat(d_skip, SSD_HEAD_DIM).reshape(1, inner)
    gnw = jnp.tile(gla_nw, GLA_HEADS).reshape(1, vdim)
    row = lambda w: pl.BlockSpec((ROW_TILE, w), lambda i: (i, 0))
    vec = lambda w: pl.BlockSpec((1, w), lambda i: (0, 0))
    return pl.pallas_call(
        _even_merge_kernel,
        grid=(ta // ROW_TILE,),
        in_specs=[row(inner)] * 4 + [row(vdim)] * 3 + [vec(inner), vec(inner), vec(vdim),
                                                        pl.BlockSpec(w_out.shape, lambda i: (0, 0))],
        out_specs=row(d),
        out_shape=jax.ShapeDtypeStruct((ta, d), F32),
        compiler_params=_cparams(("arbitrary",)),
        name="even_merge",
    )(yf, yb, xs, z, of, ob, r, dsk, ssd_nw.reshape(1, inner), gnw, w_out)


def _odd_prep_kernel(qkv_ref, prev_ref, next_ref, small_ref, cw_ref, dtb_ref, aneg_ref,
                     q_ref, k_ref, v_ref, g_ref, beta_ref, ext_ref, *, n_tiles):
    kdim = q_ref.shape[1]

    def l2n(a, scale):
        parts = []
        for h in range(kdim // GDN_DK):
            ah = a[:, h * GDN_DK:(h + 1) * GDN_DK]
            parts.append(ah * (lax.rsqrt(jnp.sum(ah * ah, axis=-1, keepdims=True) + EPS) * scale))
        return jnp.concatenate(parts, axis=1)

    y = _conv_silu(qkv_ref, prev_ref, next_ref, cw_ref, None, ext_ref, n_tiles)
    q_ref[...] = l2n(y[:, :kdim], GDN_DK ** -0.5)
    k_ref[...] = l2n(y[:, kdim:2 * kdim], 1.0)
    v_ref[...] = y[:, 2 * kdim:]
    small = small_ref[...]
    g_ref[...] = _softplus(small + dtb_ref[...]) * aneg_ref[...]
    beta_ref[...] = jax.nn.sigmoid(small)


def _odd_prep(qkv, small, conv_w, dt_bias, a_log):
    ta, cdim = qkv.shape
    n_tiles = ta // ROW_TILE
    kdim = GDN_QK_HEADS * GDN_DK
    vdim = GDN_V_HEADS * GDN_DV
    cw = jnp.zeros((8, cdim), F32).at[:CONV_W].set(conv_w)
    dtb = jnp.zeros((1, LANES), F32).at[0, :2 * SMALL].set(dt_bias.reshape(-1))
    aneg = jnp.zeros((1, LANES), F32).at[0, :2 * SMALL].set(-jnp.exp(a_log.reshape(-1)))
    prev_spec, next_spec = _halo_specs(cdim, n_tiles)
    row = lambda w: pl.BlockSpec((ROW_TILE, w), lambda i: (i, 0))
    full = lambda a: pl.BlockSpec(a.shape, lambda i: (0,) * a.ndim)
    return pl.pallas_call(
        functools.partial(_odd_prep_kernel, n_tiles=n_tiles),
        grid=(n_tiles,),
        in_specs=[row(cdim), prev_spec, next_spec, row(LANES), full(cw), full(dtb), full(aneg)],
        out_specs=[row(kdim), row(kdim), row(vdim), row(LANES), row(LANES)],
        out_shape=[jax.ShapeDtypeStruct((ta, w), F32) for w in (kdim, kdim, vdim, LANES, LANES)],
        scratch_shapes=[pltpu.VMEM((ROW_TILE + 2 * HALO, cdim), F32)],
        compiler_params=_cparams(("arbitrary",)),
        name="odd_prep",
    )(qkv, qkv, qkv, small, cw, dtb, aneg)


GDN_SCAN_INS = 5


def _gdn_kernel(*refs):
    ins = refs[:2 * GDN_SCAN_INS]
    (tri_ref, expand_ref, eye_ref, trit_ref, strict_ref, pair_ref, lvl_ref, eye2_ref,
     of_ref, ob_ref, st_ref) = refs[2 * GDN_SCAN_INS:]

    @pl.when(pl.program_id(0) == 0)
    def _():
        st_ref[...] = jnp.zeros(st_ref.shape, F32)

    pair = pair_ref[...]
    eye2 = eye2_ref[...]
    left = lax.broadcasted_iota(jnp.int32, (CHUNK, 2 * CHUNK), 1) < CHUNK
    n_levels = lvl_ref.shape[0]
    dirs = range(2)
    kk_qk = {}
    for d in dirs:
        q_ref, k_ref = ins[GDN_SCAN_INS * d:GDN_SCAN_INS * d + 2]
        for j in range(GDN_QK_HEADS):
            ksl = slice(j * GDN_DK, (j + 1) * GDN_DK)
            kb = k_ref[:, ksl].astype(BF16)
            krep = jnp.concatenate([kb, kb], axis=0)
            kk_qk[d, j] = (lax.dot_general(kb, krep, NT_DIMS, preferred_element_type=F32),
                           lax.dot_general(q_ref[:, ksl].astype(BF16), krep, NT_DIMS, preferred_element_type=F32))
    cols, decays = _decay_tables([tri_ref[d] for d in dirs], expand_ref[...], eye_ref[...],
                                 [trit_ref[d] for d in dirs],
                                 [ins[GDN_SCAN_INS * d + 3][:, d * SMALL:(d + 1) * SMALL] for d in dirs])
    beta_ts = [_sel_dot_r(ins[GDN_SCAN_INS * d + 4][:, (2 + d) * SMALL:(3 + d) * SMALL], expand_ref[...])
               for d in dirs]
    a_bds, rhs_all = [], []
    atts, qds, kds, gls = {}, {}, {}, {}
    for d in dirs:
        q_ref, k_ref, v_ref = ins[GDN_SCAN_INS * d:GDN_SCAN_INS * d + 3]
        last_row = 0 if d else CHUNK - 1
        col, decay, beta_t = cols[d], decays[d], beta_ts[d]
        a_all = beta_t * decay * strict_ref[d]
        for j in range(GDN_QK_HEADS):
            ksl = slice(j * GDN_DK, (j + 1) * GDN_DK)
            psl = slice(j * 2 * CHUNK, (j + 1) * 2 * CHUNK)
            kk, qk = kk_qk[d, j]
            atts[d, j] = (qk * decay[:, psl]).astype(BF16)
            a_pair = kk * a_all[:, psl]
            a_bds.append(jnp.concatenate([a_pair, a_pair], axis=0) * pair)
            colp, betap = col[:, psl], beta_t[:, psl]
            colr, betar = pltpu.roll(colp, CHUNK, 1), pltpu.roll(betap, CHUNK, 1)
            kj = k_ref[:, ksl]
            rhs = []
            for hh in range(2):
                hsl = slice((2 * j + hh) * GDN_DV, (2 * j + hh + 1) * GDN_DV)
                cumx = jnp.where(left, colp, colr) if hh == 0 else jnp.where(left, colr, colp)
                betax = jnp.where(left, betap, betar) if hh == 0 else jnp.where(left, betar, betap)
                lastx = cumx[last_row:last_row + 1, :]
                egx = jnp.exp2(cumx)
                qds[d, 2 * j + hh] = (q_ref[:, ksl] * egx).astype(BF16)
                kds[d, 2 * j + hh] = (kj * jnp.exp2(lastx - cumx)).astype(BF16)
                gls[d, 2 * j + hh] = jnp.exp2(lastx)
                rhs.append(jnp.concatenate([v_ref[:, hsl] * betax, kj * betax * egx], axis=1))
            rhs_all.append(jnp.concatenate(rhs, axis=0))
    tinvs = [eye2 - a * lvl_ref[0] for a in a_bds]
    chain_dirs = [d for d in dirs for _ in range(GDN_QK_HEADS)]
    for lv in range(1, n_levels):
        b = 2 ** lv
        if b < 8:
            tls = [_bdot(t, a * lvl_ref[lv]) for t, a in zip(tinvs, a_bds)]
            tinvs = [t - _bdot(tl, t) for t, tl in zip(tinvs, tls)]
            continue
        blocks = lambda t: [t[i * b:(i + 1) * b] for i in range(t.shape[0] // b)]
        moving = lambda d, t: jnp.concatenate(blocks(t)[(1 - d)::2], axis=0)
        tls = [_bdot(moving(d, t), a * lvl_ref[lv]) for d, t, a in zip(chain_dirs, tinvs, a_bds)]
        news = [blocks(moving(d, t) - _bdot(tl, t)) for d, t, tl in zip(chain_dirs, tinvs, tls)]
        tinvs = [jnp.concatenate([new[i // 2] if i % 2 == 1 - d else blk for i, blk in enumerate(blocks(t))], axis=0)
                 for d, t, new in zip(chain_dirs, tinvs, news)]
    xs = [_bdot(t, r) for t, r in zip(tinvs, rhs_all)]
    chains = [(d, h) for d in dirs for h in range(GDN_V_HEADS)]
    o_refs = (of_ref, ob_ref)
    head_rows = lambda x, h: x[(h % 2) * CHUNK:(h % 2 + 1) * CHUNK]
    x_of = lambda d, h: head_rows(xs[d * GDN_QK_HEADS + h // 2], h)
    ws_qs = [jnp.dot(jnp.concatenate([x_of(d, h)[:, GDN_DV:].astype(BF16), qds[d, h]], axis=0),
                     st_ref[d, h].astype(BF16), preferred_element_type=F32) for d, h in chains]
    v_new = [(x_of(d, h)[:, :GDN_DV] - r[:CHUNK]).astype(BF16) for (d, h), r in zip(chains, ws_qs)]
    for (d, h), r, vn in zip(chains, ws_qs, v_new):
        att = atts[d, h // 2][:, (h % 2) * CHUNK:(h % 2 + 1) * CHUNK]
        o_refs[d][:, h * GDN_DV:(h + 1) * GDN_DV] = (r[CHUNK:] + jnp.dot(att, vn, preferred_element_type=F32)
                                                     ).astype(o_refs[d].dtype)
    for (d, h), vn in zip(chains, v_new):
        st_ref[d, h] = st_ref[d, h] * gls[d, h] + lax.dot_general(kds[d, h], vn, TN_DIMS,
                                                                  preferred_element_type=F32)


def _gdn_scan(q, k, v, g, beta, n_ctx_chunks):
    ta, kdim = q.shape
    vdim = v.shape[1]
    nc = ta // CHUNK
    fwd = _scan_consts(False, GDN_V_HEADS, CHUNK)
    bwd = _scan_consts(True, GDN_V_HEADS, CHUNK)
    both = lambda name: jnp.stack([fwd[name], bwd[name]])
    consts = [both("tri3"), fwd["expand3"], fwd["eye_t"], both("tri_t"), both("strict_t"),
              _pair_mask(), _level_masks(), jnp.eye(2 * CHUNK, dtype=F32)]
    full = lambda a: pl.BlockSpec(a.shape, lambda i: (0,) * a.ndim)

    def row(rev, w):
        cmap = _chunk_map(rev, nc, n_ctx_chunks)
        return pl.BlockSpec((CHUNK, w), lambda i: (cmap(i), 0))

    dir_specs = lambda rev: [row(rev, kdim), row(rev, kdim), row(rev, vdim), row(rev, LANES), row(rev, LANES)]
    return pl.pallas_call(
        _gdn_kernel,
        grid=(nc,),
        in_specs=dir_specs(False) + dir_specs(True) + [full(a) for a in consts],
        out_specs=[row(False, vdim), row(True, vdim)],
        out_shape=[jax.ShapeDtypeStruct((ta, vdim), BF16)] * 2,
        scratch_shapes=[pltpu.VMEM((2, GDN_V_HEADS, GDN_DK, GDN_DV), F32)],
        compiler_params=_cparams(("arbitrary",)),
        name="gdn_scan",
    )(q, k, v, g, beta, q, k, v, g, beta, *consts)


def _odd_merge_kernel(of_ref, ob_ref, z_ref, nw_ref, w_ref, o_ref):
    f32 = lambda ref: ref[...].astype(F32)
    o = _group_rms(f32(of_ref) + f32(ob_ref), nw_ref[...], GDN_DV) * _silu(f32(z_ref))
    o_ref[...] = _bdot(o, w_ref[...])


def _odd_merge(of, ob, z, gdn_nw, w_out, n_ctx_tiles):
    ta, vdim = of.shape
    d = w_out.shape[1]
    n_lat = ta - n_ctx_tiles * ROW_TILE
    nw = jnp.tile(gdn_nw, GDN_V_HEADS).reshape(1, vdim)
    row_in = pl.BlockSpec((ROW_TILE, vdim), lambda i: (i + n_ctx_tiles, 0))
    return pl.pallas_call(
        _odd_merge_kernel,
        grid=(n_lat // ROW_TILE,),
        in_specs=[row_in] * 3 + [pl.BlockSpec((1, vdim), lambda i: (0, 0)), pl.BlockSpec(w_out.shape, lambda i: (0, 0))],
        out_specs=pl.BlockSpec((ROW_TILE, d), lambda i: (i, 0)),
        out_shape=jax.ShapeDtypeStruct((n_lat, d), F32),
        compiler_params=_cparams(("arbitrary",)),
        name="odd_merge",
    )(of, ob, z, nw, w_out)


def _ffn_kernel(*refs, n_ctx_tiles, final):
    if n_ctx_tiles:
        xc_ref, xl_ref, y_ref, mod_ref, nw_ref, w1_ref, w3_ref, w2_ref, fw_ref, oc_ref, ol_ref = refs
    else:
        xl_ref, y_ref, mod_ref, nw_ref, w1_ref, w3_ref, w2_ref, fw_ref, ol_ref = refs
    d = xl_ref.shape[1]
    is_ctx = pl.program_id(0) < n_ctx_tiles
    x = jnp.where(is_ctx, xc_ref[...], xl_ref[...]) if n_ctx_tiles else xl_ref[...]
    x = x + _mod_rows(mod_ref, is_ctx, 2, d) * y_ref[...]
    h = _norm_mod(x, nw_ref[...], _mod_rows(mod_ref, is_ctx, 3, d), _mod_rows(mod_ref, is_ctx, 4, d)).astype(BF16)
    a = jnp.dot(h, w1_ref[...], preferred_element_type=F32)
    b = jnp.dot(h, w3_ref[...], preferred_element_type=F32)
    out = x + _mod_rows(mod_ref, is_ctx, 5, d) * _bdot(_silu(a) * b, w2_ref[...])
    if final:
        ms = jnp.mean(out * out, axis=-1, keepdims=True)
        out = out * lax.rsqrt(ms + EPS) * fw_ref[...]
    if n_ctx_tiles:
        @pl.when(is_ctx)
        def _():
            oc_ref[...] = out

        @pl.when(jnp.logical_not(is_ctx))
        def _():
            ol_ref[...] = out
    else:
        ol_ref[...] = out


def _ffn(x_ctx, x_lat, y_all, mod, nw, w1, w3, w2, final_w, final):
    d = x_lat.shape[1]
    n_ctx_tiles = 0 if x_ctx is None else x_ctx.shape[0] // ROW_TILE
    n_tiles = n_ctx_tiles + x_lat.shape[0] // ROW_TILE
    row = pl.BlockSpec((ROW_TILE, d), lambda i: (i, 0))
    full = lambda a: pl.BlockSpec(a.shape, lambda i: (0,) * a.ndim)
    nw = nw.reshape(1, d)
    fw = final_w.reshape(1, d)
    params = [mod, nw, w1, w3, w2, fw]
    if n_ctx_tiles:
        x_specs, x_args = list(_ctx_lat_specs(n_ctx_tiles, d)), [x_ctx, x_lat]
        out_specs = list(_ctx_lat_specs(n_ctx_tiles, d))
        out_shape = [jax.ShapeDtypeStruct(x_ctx.shape, F32), jax.ShapeDtypeStruct(x_lat.shape, F32)]
    else:
        x_specs, x_args = [row], [x_lat]
        out_specs, out_shape = [row], [jax.ShapeDtypeStruct(x_lat.shape, F32)]
    outs = pl.pallas_call(
        functools.partial(_ffn_kernel, n_ctx_tiles=n_ctx_tiles, final=final),
        grid=(n_tiles,),
        in_specs=x_specs + [row] + [full(a) for a in params],
        out_specs=out_specs,
        out_shape=out_shape,
        compiler_params=_cparams(("arbitrary",)),
        name="ffn_final" if final else "ffn",
    )(*x_args, y_all, *params)
    return (outs[0], outs[1]) if n_ctx_tiles else (None, outs[0])


def _pad_cols(parts, width):
    w = jnp.concatenate(parts, axis=1)
    return jnp.pad(w, ((0, 0), (0, width - w.shape[1])))


def _even_w_in(w):
    inner = SSD_HEADS * SSD_HEAD_DIM
    conv = inner + 2 * SSD_GROUPS * SSD_STATE
    gkey = GLA_HEADS * GLA_DK
    gval = GLA_HEADS * GLA_DV
    sizes = (inner, conv, SMALL, SMALL, gkey, gkey, gval, gval, SMALL, SMALL)
    offs = [0]
    for s in sizes:
        offs.append(offs[-1] + s)
    z, xbc, dtf, dtb, q, k, v, r, glf, glb = (w[:, offs[i]:offs[i + 1]] for i in range(len(sizes)))
    small = _pad_cols([dtf, dtb, glf, glb], LANES)
    widths = (inner, conv, gkey, gkey, gval, gval, LANES)
    dtypes = (BF16, F32, F32, F32, F32, BF16, F32)
    return jnp.concatenate([z, xbc, q, k, v, r, small], axis=1).astype(BF16), widths, dtypes


def _odd_w_in(w):
    kdim = GDN_QK_HEADS * GDN_DK
    vdim = GDN_V_HEADS * GDN_DV
    conv = 2 * kdim + vdim
    small = _pad_cols([w[:, conv + vdim:]], LANES)
    widths = (conv, vdim, LANES)
    dtypes = (F32, BF16, F32)
    return jnp.concatenate([w[:, :conv + vdim], small], axis=1).astype(BF16), widths, dtypes


def _to_column_major(h):
    t, d = h.shape
    return h.reshape(t // GRID_W, GRID_W, d).transpose(1, 0, 2).reshape(t, d)


def _from_column_major(h):
    t, d = h.shape
    return h.reshape(GRID_W, t // GRID_W, d).transpose(1, 0, 2).reshape(t, d)


def kernel(x, c, ctx, c_ctx, mod_w, mod_b, norm_mix, norm_ffn, ffn_w1, ffn_w3, ffn_w2, ev_w_in, ssd_conv_w, ssd_conv_b, ssd_dt_bias, ssd_a_log, ssd_d, ssd_norm, gla_gate_w, gla_gate_b, gla_norm, ev_w_out, od_w_in, gdn_conv_w, gdn_dt_bias, gdn_a_log, gdn_norm, od_w_out, final_norm):
    batch, seq, d = x.shape
    n_ctx = ctx.shape[1]
    assert batch == 1 and n_ctx == ROW_TILE and seq % ROW_TILE == 0 and seq % GRID_W == 0
    assert mod_w.shape[0] == 2, "one even (SSD + GLA) layer followed by one odd (gated DeltaNet) layer"
    n_ctx_tiles = n_ctx // ROW_TILE
    n_ctx_chunks = n_ctx // CHUNK

    mods = _modulation(c, c_ctx, mod_w, mod_b)
    x_ctx, x_lat = ctx[0], x[0]

    w_in, widths, dtypes = _even_w_in(ev_w_in[0])
    z, xbc, q, k, v, r, small = _norm_proj(x_ctx, x_lat, norm_mix[0], mods[0], w_in, widths, dtypes)
    xs, bm, cm, dt, da, gk_f, gk_b = _even_prep(xbc, small, ssd_conv_w[0], ssd_conv_b[0], ssd_dt_bias[0],
                                                ssd_a_log[0], gla_gate_w[0], gla_gate_b[0])
    y_f, y_b = _ssd_scan(xs, bm, cm, dt, da, n_ctx_chunks)
    o_f, o_b = _gla_scan(q, k, v, gk_f, gk_b, n_ctx_chunks)
    mixed = _even_merge(y_f, y_b, xs, z, o_f, o_b, r, ssd_d[0], ssd_norm[0], gla_norm[0], ev_w_out[0].astype(BF16))
    x_ctx, x_lat = _ffn(x_ctx, x_lat, mixed, mods[0], norm_ffn[0], ffn_w1[0].astype(BF16), ffn_w3[0].astype(BF16),
                        ffn_w2[0].astype(BF16), final_norm, False)

    w_in, widths, dtypes = _odd_w_in(od_w_in[0])
    qkv, z, small = _norm_proj(x_ctx, _to_column_major(x_lat), norm_mix[1], mods[1], w_in, widths, dtypes)
    q, k, v, g, beta = _odd_prep(qkv, small, gdn_conv_w[0], gdn_dt_bias[0], gdn_a_log[0])
    o_f, o_b = _gdn_scan(q, k, v, g, beta, n_ctx_chunks)
    mixed = _odd_merge(o_f, o_b, z, gdn_norm[0], od_w_out[0].astype(BF16), n_ctx_tiles)
    _, out = _ffn(None, x_lat, _from_column_major(mixed), mods[1], norm_ffn[1], ffn_w1[1].astype(BF16),
                  ffn_w3[1].astype(BF16), ffn_w2[1].astype(BF16), final_norm, True)
    return out[None]
```

```python
import functools

import jax
import jax.numpy as jnp
from jax import lax
from jax.experimental import pallas as pl
from jax.experimental.pallas import tpu as pltpu

F32 = jnp.float32
BF16 = jnp.bfloat16
HIGHEST = lax.Precision.HIGHEST

EPS = 1e-6
LOG2_E = 1.4426950408889634
CHUNK = 64
SCAN_CPS = 4
GLA_CPS = 2
GRID_W = 64
CONV_W = 5
ROW_TILE = 256
HALO = 8
LANES = 128
VMEM_LIMIT_BYTES = 56 * 1024 * 1024

SSD_HEADS = 16
SSD_HEAD_DIM = 64
SSD_GROUPS = 2
SSD_STATE = 128
GLA_HEADS = 4
GLA_DK = 128
GLA_DV = 256
GLA_GATE_RANK = 16
GLA_GATE_NORM = 16.0
GDN_QK_HEADS = 8
GDN_V_HEADS = 16
GDN_DK = 128
GDN_DV = 128
SMALL = 16

NT_DIMS = (((1,), (1,)), ((), ()))
TN_DIMS = (((0,), (0,)), ((), ()))


def _cparams(sem):
    return pltpu.CompilerParams(dimension_semantics=sem, vmem_limit_bytes=VMEM_LIMIT_BYTES)


def _bdot(a, b):
    return jnp.dot(a.astype(BF16), b.astype(BF16), preferred_element_type=F32)


def _bdot_nt(a, b):
    return lax.dot_general(a.astype(BF16), b.astype(BF16), NT_DIMS, preferred_element_type=F32)


def _bdot_tn(a, b):
    return lax.dot_general(a.astype(BF16), b.astype(BF16), TN_DIMS, preferred_element_type=F32)


def _split3(x):
    hi = x.astype(BF16)
    r1 = x - hi.astype(F32)
    mid = r1.astype(BF16)
    lo = (r1 - mid.astype(F32)).astype(BF16)
    return hi, mid, lo


def _sel_dot_l(sel3, x):
    return jnp.dot(sel3, jnp.concatenate(_split3(x), axis=0), preferred_element_type=F32)


def _sel_dot_r(x, sel3):
    pieces = jnp.concatenate([p.astype(F32) for p in _split3(x)], axis=1)
    return jnp.dot(pieces.astype(BF16), sel3, preferred_element_type=F32)


def _silu(x):
    return x * jax.nn.sigmoid(x)


def _softplus(x):
    return jnp.maximum(x, 0.0) + jnp.log1p(jnp.exp(-jnp.abs(x)))


def _mod_kernel(c_ref, w_ref, b_ref, o_ref):
    s = _silu(c_ref[...])
    o_ref[0] = jnp.dot(s, w_ref[0], precision=HIGHEST, preferred_element_type=F32) + b_ref[0]


def _modulation(c, c_ctx, mod_w, mod_b):
    depth, d, n = mod_w.shape
    cs = jnp.zeros((8, d), F32).at[0].set(c[0]).at[1].set(c_ctx)
    tn = n // 4
    return pl.pallas_call(
        _mod_kernel,
        grid=(depth, n // tn),
        in_specs=[pl.BlockSpec((8, d), lambda i, j: (0, 0)),
                  pl.BlockSpec((1, d, tn), lambda i, j: (i, 0, j)),
                  pl.BlockSpec((1, 1, tn), lambda i, j: (i, 0, j))],
        out_specs=pl.BlockSpec((1, 8, tn), lambda i, j: (i, 0, j)),
        out_shape=jax.ShapeDtypeStruct((depth, 8, n), F32),
        compiler_params=_cparams(("arbitrary", "arbitrary")),
        name="modulation",
    )(cs, mod_w, mod_b.reshape(depth, 1, n))


def _mod_rows(mod_ref, is_ctx, idx, d):
    sl = slice(idx * d, (idx + 1) * d)
    return jnp.where(is_ctx, mod_ref[1:2, sl], mod_ref[0:1, sl])


def _norm_mod(x, nw, shift, scale):
    ms = jnp.mean(x * x, axis=-1, keepdims=True)
    return (x * lax.rsqrt(ms + EPS) * nw) * (1.0 + scale) + shift


def _ctx_lat_specs(n_ctx_tiles, d):
    ctx = pl.BlockSpec((ROW_TILE, d), lambda i: (jnp.minimum(i, n_ctx_tiles - 1), 0))
    lat = pl.BlockSpec((ROW_TILE, d), lambda i: (jnp.maximum(i - n_ctx_tiles, 0), 0))
    return ctx, lat


def _proj_kernel(xc_ref, xl_ref, nw_ref, mod_ref, w_ref, *out_refs, widths, n_ctx_tiles):
    d = xl_ref.shape[1]
    is_ctx = pl.program_id(0) < n_ctx_tiles
    x = jnp.where(is_ctx, xc_ref[...], xl_ref[...])
    h = _norm_mod(x, nw_ref[...], _mod_rows(mod_ref, is_ctx, 0, d), _mod_rows(mod_ref, is_ctx, 1, d))
    h = h.astype(BF16)
    off = 0
    for o_ref, wd in zip(out_refs, widths):
        o_ref[...] = jnp.dot(h, w_ref[:, off:off + wd], preferred_element_type=F32).astype(o_ref.dtype)
        off += wd


def _norm_proj(x_ctx, x_lat, nw, mod, w, widths, dtypes):
    d = x_lat.shape[1]
    n_ctx_tiles = x_ctx.shape[0] // ROW_TILE
    ta = x_ctx.shape[0] + x_lat.shape[0]
    ntot = sum(widths)
    kern = functools.partial(_proj_kernel, widths=widths, n_ctx_tiles=n_ctx_tiles)
    return pl.pallas_call(
        kern,
        grid=(ta // ROW_TILE,),
        in_specs=[*_ctx_lat_specs(n_ctx_tiles, d),
                  pl.BlockSpec((1, d), lambda i: (0, 0)),
                  pl.BlockSpec(mod.shape, lambda i: (0, 0)),
                  pl.BlockSpec((d, ntot), lambda i: (0, 0))],
        out_specs=[pl.BlockSpec((ROW_TILE, wd), lambda i: (i, 0)) for wd in widths],
        out_shape=[jax.ShapeDtypeStruct((ta, wd), dt) for wd, dt in zip(widths, dtypes)],
        compiler_params=_cparams(("arbitrary",)),
        name="norm_proj",
    )(x_ctx, x_lat, nw.reshape(1, d), mod, w)


def _halo_specs(width, n_tiles):
    per = ROW_TILE // HALO
    prev = pl.BlockSpec((HALO, width), lambda i: (jnp.maximum(i * per - 1, 0), 0))
    nxt = pl.BlockSpec((HALO, width), lambda i: (jnp.minimum((i + 1) * per, n_tiles * per - 1), 0))
    return prev, nxt


CONV_BIAS_ROW = CONV_W


def _conv_weights(conv_w, conv_b):
    cdim = conv_w.shape[1]
    w = jnp.zeros((8, cdim), F32).at[:CONV_W].set(conv_w)
    if conv_b is not None:
        w = w.at[CONV_BIAS_ROW].set(conv_b)
    return w.reshape(8, cdim // LANES, LANES).transpose(1, 0, 2)


def _conv_silu(cur_ref, prev_ref, next_ref, w_ref, ext_ref, out_ref, n_tiles, posts):
    i = pl.program_id(0)
    prev_ok = jnp.logical_and(i != 0, i != 1)
    next_ok = jnp.logical_and(i != 0, i != n_tiles - 1)
    for c in range(ext_ref.shape[0]):
        lanes = slice(c * LANES, (c + 1) * LANES)
        ext_ref[c, 0:HALO, :] = jnp.where(prev_ok, prev_ref[:, lanes], 0.0)
        ext_ref[c, HALO:HALO + ROW_TILE, :] = cur_ref[:, lanes]
        ext_ref[c, HALO + ROW_TILE:, :] = jnp.where(next_ok, next_ref[:, lanes], 0.0)
    pad = CONV_W // 2
    stride = 8
    for lo, hi, post in posts:
        def slab(c, carry, post=post):
            w = w_ref[c]
            starts = [base + r for base in range(0, ROW_TILE, 8 * stride) for r in range(stride)]
            accs = [w[CONV_BIAS_ROW:CONV_BIAS_ROW + 1, :] for _ in starts]
            for j in range(CONV_W):
                taps = [ext_ref[c, pl.ds(HALO - pad + j + s, 8, stride=stride), :] for s in starts]
                accs = [acc + tap * w[j:j + 1, :] for acc, tap in zip(accs, taps)]
            ys = post([_silu(acc) for acc in accs])
            for s, y in zip(starts, ys):
                out_ref[c, pl.ds(s, 8, stride=stride), :] = y
            return carry
        lax.fori_loop(lo, hi, slab, 0)


def _even_prep_kernel(xbc_ref, prev_ref, next_ref, small_ref, cw_ref, dtb_ref, aneg_ref,
                      gw_ref, gb_ref, xs_ref, bm_ref, cm_ref, dt_ref, da_ref, gkf_ref, gkb_ref,
                      ext_ref, conv_ref, *, n_tiles):
    n_slabs = ext_ref.shape[0]
    _conv_silu(xbc_ref, prev_ref, next_ref, cw_ref, ext_ref, conv_ref, n_tiles, [(0, n_slabs, lambda ys: ys)])
    slab = 0
    for o_ref in (xs_ref, bm_ref, cm_ref):
        for c in range(o_ref.shape[1] // LANES):
            o_ref[:, c * LANES:(c + 1) * LANES] = conv_ref[slab]
            slab += 1
    small = small_ref[...]
    dt = _softplus(small + dtb_ref[...])
    dt_ref[...] = dt
    da_ref[...] = dt * aneg_ref[...]
    for d, o_ref in enumerate((gkf_ref, gkb_ref)):
        pre = jnp.dot(small, gw_ref[d], precision=HIGHEST, preferred_element_type=F32) + gb_ref[d]
        o_ref[...] = -_softplus(-pre) * (1.0 / GLA_GATE_NORM)


def _even_prep(xbc, small, conv_w, conv_b, dt_bias, a_log, gate_w, gate_b):
    ta, cdim = xbc.shape
    n_tiles = ta // ROW_TILE
    inner = SSD_HEADS * SSD_HEAD_DIM
    gs = SSD_GROUPS * SSD_STATE
    gkey = GLA_HEADS * GLA_DK
    cw = _conv_weights(conv_w, conv_b)
    dtb = jnp.zeros((1, LANES), F32).at[0, :2 * SMALL].set(dt_bias.reshape(-1))
    aneg = jnp.zeros((1, LANES), F32).at[0, :2 * SMALL].set(-jnp.exp(a_log.reshape(-1)))
    gw = jnp.zeros((2, LANES, gkey), F32)
    gw = gw.at[0, 2 * SMALL:3 * SMALL].set(gate_w[0]).at[1, 3 * SMALL:4 * SMALL].set(gate_w[1])
    prev_spec, next_spec = _halo_specs(cdim, n_tiles)
    row = lambda w: pl.BlockSpec((ROW_TILE, w), lambda i: (i, 0))
    full = lambda a: pl.BlockSpec(a.shape, lambda i: (0,) * a.ndim)
    gb = gate_b.reshape(2, 1, gkey)
    return pl.pallas_call(
        functools.partial(_even_prep_kernel, n_tiles=n_tiles),
        grid=(n_tiles,),
        in_specs=[row(cdim), prev_spec, next_spec, row(LANES), full(cw), full(dtb), full(aneg), full(gw), full(gb)],
        out_specs=[row(inner), row(gs), row(gs), row(LANES), row(LANES), row(gkey), row(gkey)],
        out_shape=[jax.ShapeDtypeStruct((ta, w), F32) for w in (inner, gs, gs, LANES, LANES, gkey, gkey)],
        scratch_shapes=[pltpu.VMEM((cdim // LANES, ROW_TILE + 2 * HALO, LANES), F32),
                        pltpu.VMEM((cdim // LANES, ROW_TILE, LANES), F32)],
        compiler_params=_cparams(("arbitrary",)),
        name="even_prep",
    )(xbc, xbc, xbc, small, cw, dtb, aneg, gw, gb)


def _scan_consts(rev, heads, lanes_per_head):
    r = jnp.arange(CHUNK)
    tri = (r[:, None] <= r[None, :]) if rev else (r[:, None] >= r[None, :])
    strict = (r[:, None] < r[None, :]) if rev else (r[:, None] > r[None, :])
    eye = r[:, None] == r[None, :]
    expand = (jnp.arange(heads)[:, None] == (jnp.arange(heads * lanes_per_head)[None, :] // lanes_per_head))
    return dict(
        tri3=jnp.tile(tri, (1, 3)).astype(BF16),
        expand3=jnp.tile(expand, (3, 1)).astype(BF16),
        tri_t=jnp.tile(tri, (1, heads)).astype(F32),
        strict_t=jnp.tile(strict, (1, heads)).astype(F32),
        eye_t=jnp.tile(eye, (1, heads)).astype(F32),
    )


def _pair_mask():
    r = jnp.arange(2 * CHUNK)
    return ((r[:, None] // CHUNK) == (r[None, :] // CHUNK)).astype(F32)


def _level_masks():
    r = jnp.arange(2 * CHUNK)
    same = lambda b: (r[:, None] // b) == (r[None, :] // b)
    sizes = [2 ** m for m in range(CHUNK.bit_length() - 1)]
    return jnp.stack([jnp.logical_and(same(2 * b), jnp.logical_not(same(b))) for b in sizes]).astype(F32)


def _chunk_map(rev, n_chunks, n_ctx_chunks):
    if not rev:
        return lambda i: i
    return lambda i: jnp.where(i < n_ctx_chunks, n_ctx_chunks - 1 - i, n_chunks + n_ctx_chunks - 1 - i)


def _decay_tables(tri3s, expand3, eye_t, tri_ts, gs):
    cums = [_sel_dot_l(tri3, g) * LOG2_E for tri3, g in zip(tri3s, gs)]
    cols = [_sel_dot_r(cum, expand3) for cum in cums]
    rowfs = [jnp.sum(col * eye_t, axis=0, keepdims=True) for col in cols]
    decays = [jnp.where(tri_t > 0.5, jnp.exp2(col - rowf), 0.0) for tri_t, col, rowf in zip(tri_ts, cols, rowfs)]
    return cols, decays


SSD_SCAN_INS = 5


def _ssd_kernel(*refs, cps):
    ins = refs[:2 * SSD_SCAN_INS]
    tri_ref, expand_ref, eye_ref, trit_ref, pair_ref, yf_ref, yb_ref, st_ref = refs[2 * SSD_SCAN_INS:]

    @pl.when(pl.program_id(0) == 0)
    def _():
        st_ref[...] = jnp.zeros(st_ref.shape, F32)

    pair = pair_ref[...]
    heads_g = SSD_HEADS // SSD_GROUPS
    gw = heads_g * SSD_HEAD_DIM
    pieces = [(d, c, g) for d in range(2) for c in range(cps) for g in range(SSD_GROUPS)]
    gsl = lambda g: slice(g * SSD_STATE, (g + 1) * SSD_STATE)
    lanes = lambda g: slice(g * gw, (g + 1) * gw)
    rows = lambda d, c: slice((cps - 1 - c if d else c) * CHUNK, (cps - c if d else c + 1) * CHUNK)
    chunks = [(d, c) for d in range(2) for c in range(cps)]
    small = lambda d, c, k: ins[SSD_SCAN_INS * d + k][rows(d, c), d * SMALL:(d + 1) * SMALL]
    cols, decays = _decay_tables([tri_ref[d] for d, _ in chunks], expand_ref[...], eye_ref[...],
                                 [trit_ref[d] for d, _ in chunks], [small(d, c, 4) for d, c in chunks])
    dtes = [_sel_dot_r(small(d, c, 3), expand_ref[...]) for d, c in chunks]
    tabs = {}
    for (d, c), col, decay, dte in zip(chunks, cols, decays, dtes):
        last_row = 0 if d else CHUNK - 1
        xdt = ins[SSD_SCAN_INS * d][rows(d, c), :] * dte
        last = col[last_row:last_row + 1, :]
        tabs[d, c] = dict(decay=decay, xdt=xdt, xw=xdt * jnp.exp2(last - col), chunk_decay=jnp.exp2(last),
                          eac=jnp.exp2(col))
    bgs = {(d, c, g): ins[SSD_SCAN_INS * d + 1][rows(d, c), gsl(g)].astype(BF16) for d, c, g in pieces}
    cgs = {(d, c, g): ins[SSD_SCAN_INS * d + 2][rows(d, c), gsl(g)].astype(BF16) for d, c, g in pieces}
    wms = {p: lax.dot_general(cgs[p], jnp.concatenate([bgs[p]] * heads_g, axis=0), NT_DIMS,
                              preferred_element_type=F32) * tabs[p[:2]]["decay"][:, lanes(p[2])] for p in pieces}
    news = {(d, c, g): lax.dot_general(bgs[d, c, g], tabs[d, c]["xw"][:, lanes(g)].astype(BF16), TN_DIMS,
                                       preferred_element_type=F32) for d, c, g in pieces}
    diags = {}
    for d, c, g in pieces:
        xg = tabs[d, c]["xdt"][:, lanes(g)]
        for j in range(gw // LANES):
            xp = xg[:, j * LANES:(j + 1) * LANES]
            blockdiag = jnp.concatenate([xp, xp], axis=0) * pair
            diags[d, c, g, j] = _bdot(wms[d, c, g][:, j * LANES:(j + 1) * LANES], blockdiag)
    states = {}
    for d in range(2):
        for g in range(SSD_GROUPS):
            state = st_ref[d, g]
            for c in range(cps):
                states[d, c, g] = state
                state = state * tabs[d, c]["chunk_decay"][:, lanes(g)] + news[d, c, g]
            st_ref[d, g] = state
    offs = {p: jnp.dot(cgs[p], states[p].astype(BF16), preferred_element_type=F32) for p in pieces}
    for d, y_ref in enumerate((yf_ref, yb_ref)):
        for c in range(cps):
            for g in range(SSD_GROUPS):
                diag = jnp.concatenate([diags[d, c, g, j] for j in range(gw // LANES)], axis=1)
                y_ref[rows(d, c), lanes(g)] = (diag + offs[d, c, g] * tabs[d, c]["eac"][:, lanes(g)]).astype(y_ref.dtype)


def _ssd_scan(xs, bm, cm, dt, da, n_ctx_chunks):
    ta, inner = xs.shape
    cps = SCAN_CPS
    nb = ta // (cps * CHUNK)
    fwd = _scan_consts(False, SSD_HEADS, SSD_HEAD_DIM)
    bwd = _scan_consts(True, SSD_HEADS, SSD_HEAD_DIM)
    both = lambda name: jnp.stack([fwd[name], bwd[name]])
    consts = [both("tri3"), fwd["expand3"], fwd["eye_t"], both("tri_t"), _pair_mask()]
    full = lambda a: pl.BlockSpec(a.shape, lambda i: (0,) * a.ndim)
    gs = SSD_GROUPS * SSD_STATE

    def row(rev, w):
        cmap = _chunk_map(rev, nb, n_ctx_chunks // cps)
        return pl.BlockSpec((cps * CHUNK, w), lambda i: (cmap(i), 0))

    dir_specs = lambda rev: [row(rev, inner), row(rev, gs), row(rev, gs), row(rev, LANES), row(rev, LANES)]
    return pl.pallas_call(
        functools.partial(_ssd_kernel, cps=cps),
        grid=(nb,),
        in_specs=dir_specs(False) + dir_specs(True) + [full(a) for a in consts],
        out_specs=[row(False, inner), row(True, inner)],
        out_shape=[jax.ShapeDtypeStruct((ta, inner), BF16)] * 2,
        scratch_shapes=[pltpu.VMEM((2, SSD_GROUPS, SSD_STATE, inner // SSD_GROUPS), F32)],
        compiler_params=_cparams(("arbitrary",)),
        name="ssd_scan",
    )(xs, bm, cm, dt, da, xs, bm, cm, dt, da, *consts)


GLA_SUB = 8
GLA_LEVELS = (8, 16, 32)
GLA_SCAN_INS = 4


def _gla_masks():
    r = jnp.arange(CHUNK)
    same = lambda b: (r[:, None] // b) == (r[None, :] // b)
    out = []
    for rev in (False, True):
        strict = (r[:, None] < r[None, :]) if rev else (r[:, None] > r[None, :])
        lv = [same(2 * b) & ~same(b) & strict for b in GLA_LEVELS]
        within = r % GLA_SUB
        dg = []
        for s in range(GLA_SUB):
            seen = (within <= s) if rev else (within >= s)
            dg.append((r[None, :] == (r[:, None] // GLA_SUB) * GLA_SUB + s) & seen[:, None])
        out.append(jnp.stack(lv + dg))
    return jnp.stack(out).astype(F32)


def _block_rows(a, rows, b):
    return jnp.concatenate([jnp.broadcast_to(a[r:r + 1, :], (b, a.shape[1])) for r in rows], axis=0)


def _gla_kernel(*refs, cps):
    ins = refs[:2 * GLA_SCAN_INS]
    tri_ref, mask_ref, of_ref, ob_ref, st_ref = refs[2 * GLA_SCAN_INS:]

    @pl.when(pl.program_id(0) == 0)
    def _():
        st_ref[...] = jnp.zeros(st_ref.shape, F32)

    pieces = [(d, c, h) for d in range(2) for c in range(cps) for h in range(GLA_HEADS)]
    ksl = lambda h: slice(h * GLA_DK, (h + 1) * GLA_DK)
    vsl = lambda h: slice(h * GLA_DV, (h + 1) * GLA_DV)
    rws = lambda d, c: slice((cps - 1 - c if d else c) * CHUNK, (cps - c if d else c + 1) * CHUNK)
    q_of = lambda d, c, h: ins[GLA_SCAN_INS * d][rws(d, c), ksl(h)] * (GLA_DK ** -0.5)
    k_of = lambda d, c, h: ins[GLA_SCAN_INS * d + 1][rws(d, c), ksl(h)]
    v_of = lambda d, c, h: ins[GLA_SCAN_INS * d + 2][rws(d, c), vsl(h)]
    within = lax.broadcasted_iota(jnp.int32, (CHUNK, 1), 0) % GLA_SUB
    gcs = {(d, c, h): _sel_dot_l(tri_ref[d], ins[GLA_SCAN_INS * d + 3][rws(d, c), ksl(h)]) * LOG2_E
           for d, c, h in pieces}
    news = {}
    for d, c, h in pieces:
        gc = gcs[d, c, h]
        last = 0 if d else CHUNK - 1
        glast = gc[last:last + 1, :]
        news[d, c, h] = (jnp.exp2(glast), _bdot_tn(v_of(d, c, h), k_of(d, c, h) * jnp.exp2(glast - gc)))
    states = {}
    for d in range(2):
        for h in range(GLA_HEADS):
            state = st_ref[d, h]
            for c in range(cps):
                states[d, c, h] = state
                state = state * news[d, c, h][0] + news[d, c, h][1]
            st_ref[d, h] = state
    inters = {p: _bdot_nt(q_of(*p) * jnp.exp2(gcs[p]), states[p]) for p in pieces}
    atts = {}
    for d, c, h in pieces:
        gc, qh, kh = gcs[d, c, h], q_of(d, c, h), k_of(d, c, h)
        att = None
        for li, b in enumerate(GLA_LEVELS):
            nb = CHUNK // b
            if d:
                rq = [min(b * (i + 1), CHUNK - 1) for i in range(nb)]
                rk = [b * i for i in range(nb)]
            else:
                rq = [max(b * i - 1, 0) for i in range(nb)]
                rk = [b * (i + 1) - 1 for i in range(nb)]
            qs = qh * jnp.exp2(gc - _block_rows(gc, rq, b))
            ks = kh * jnp.exp2(_block_rows(gc, rk, b) - gc)
            part = _bdot_nt(qs, ks) * mask_ref[d, li]
            att = part if att is None else att + part
        for s in range(GLA_SUB):
            rows = [GLA_SUB * i + s for i in range(CHUNK // GLA_SUB)]
            seen = (within <= s) if d else (within >= s)
            e = jnp.where(seen, jnp.exp2(gc - _block_rows(gc, rows, GLA_SUB)), 0.0)
            a = jnp.sum(qh * _block_rows(kh, rows, GLA_SUB) * e, axis=-1, keepdims=True)
            att = att + a * mask_ref[d, len(GLA_LEVELS) + s]
        atts[d, c, h] = att
    for d, o_ref in enumerate((of_ref, ob_ref)):
        for c in range(cps):
            for h in range(GLA_HEADS):
                o_ref[rws(d, c), vsl(h)] = (inters[d, c, h] + _bdot(atts[d, c, h], v_of(d, c, h))).astype(o_ref.dtype)


def _gla_scan(q, k, v, gk_f, gk_b, n_ctx_chunks):
    ta, kdim = q.shape
    vdim = v.shape[1]
    cps = GLA_CPS
    nb = ta // (cps * CHUNK)
    tri3 = jnp.stack([_scan_consts(rev, 1, 1)["tri3"] for rev in (False, True)])
    masks = _gla_masks()
    full = lambda a: pl.BlockSpec(a.shape, lambda i: (0,) * a.ndim)

    def row(rev, w):
        cmap = _chunk_map(rev, nb, n_ctx_chunks // cps)
        return pl.BlockSpec((cps * CHUNK, w), lambda i: (cmap(i), 0))

    dir_specs = lambda rev: [row(rev, kdim), row(rev, kdim), row(rev, vdim), row(rev, kdim)]
    return pl.pallas_call(
        functools.partial(_gla_kernel, cps=cps),
        grid=(nb,),
        in_specs=dir_specs(False) + dir_specs(True) + [full(tri3), full(masks)],
        out_specs=[row(False, vdim), row(True, vdim)],
        out_shape=[jax.ShapeDtypeStruct((ta, vdim), BF16)] * 2,
        scratch_shapes=[pltpu.VMEM((2, GLA_HEADS, GLA_DV, GLA_DK), F32)],
        compiler_params=_cparams(("arbitrary",)),
        name="gla_scan",
    )(q, k, v, gk_f, q, k, v, gk_b, tri3, masks)


def _group_rms(y, w, width):
    parts = []
    for g in range(y.shape[1] // width):
        yg = y[:, g * width:(g + 1) * width]
        ms = jnp.mean(yg * yg, axis=-1, keepdims=True)
        parts.append(yg * lax.rsqrt(ms + EPS))
    return jnp.concatenate(parts, axis=1) * w


def _even_merge_kernel(yf_ref, yb_ref, xs_ref, z_ref, of_ref, ob_ref, r_ref, dsk_ref, snw_ref, gnw_ref, w_ref, o_ref):
    inner = xs_ref.shape[1]
    f32 = lambda ref: ref[...].astype(F32)
    y = (f32(yf_ref) + f32(yb_ref) + dsk_ref[...] * xs_ref[...]) * _silu(f32(z_ref))
    y = _group_rms(y, snw_ref[...], inner // SSD_GROUPS)
    o = _group_rms(f32(of_ref) + f32(ob_ref), gnw_ref[...], GLA_DV) * _silu(f32(r_ref))
    o_ref[...] = _bdot(y, w_ref[:inner, :]) + _bdot(o, w_ref[inner:, :])


def _even_merge(yf, yb, xs, z, of, ob, r, d_skip, ssd_nw, gla_nw, w_out):
    ta, inner = xs.shape
    vdim = of.shape[1]
    d = w_out.shape[1]
    dsk = jnp.repeat(d_skip, SSD_HEAD_DIM).reshape(1, inner)
    gnw = jnp.tile(gla_nw, GLA_HEADS).reshape(1, vdim)
    row = lambda w: pl.BlockSpec((ROW_TILE, w), lambda i: (i, 0))
    vec = lambda w: pl.BlockSpec((1, w), lambda i: (0, 0))
    return pl.pallas_call(
        _even_merge_kernel,
        grid=(ta // ROW_TILE,),
        in_specs=[row(inner)] * 4 + [row(vdim)] * 3 + [vec(inner), vec(inner), vec(vdim),
                                                        pl.BlockSpec(w_out.shape, lambda i: (0, 0))],
        out_specs=row(d),
        out_shape=jax.ShapeDtypeStruct((ta, d), F32),
        compiler_params=_cparams(("arbitrary",)),
        name="even_merge",
    )(yf, yb, xs, z, of, ob, r, dsk, ssd_nw.reshape(1, inner), gnw, w_out)


def _odd_prep_kernel(qkv_ref, prev_ref, next_ref, small_ref, cw_ref, dtb_ref, aneg_ref,
                     q_ref, k_ref, v_ref, g_ref, beta_ref, ext_ref, conv_ref, *, n_tiles):
    assert GDN_DK == LANES
    kslabs = q_ref.shape[1] // LANES
    n_slabs = ext_ref.shape[0]
    def l2n(scale):
        def post(groups):
            sums = [jnp.sum(a * a, axis=-1, keepdims=True) for a in groups]
            return [a * (lax.rsqrt(s + EPS) * scale) for a, s in zip(groups, sums)]
        return post

    _conv_silu(qkv_ref, prev_ref, next_ref, cw_ref, ext_ref, conv_ref, n_tiles,
               [(0, kslabs, l2n(GDN_DK ** -0.5)), (kslabs, 2 * kslabs, l2n(1.0)), (2 * kslabs, n_slabs, lambda ys: ys)])
    slab = 0
    for o_ref in (q_ref, k_ref, v_ref):
        for c in range(o_ref.shape[1] // LANES):
            o_ref[:, c * LANES:(c + 1) * LANES] = conv_ref[slab]
            slab += 1
    small = small_ref[...]
    g_ref[...] = _softplus(small + dtb_ref[...]) * aneg_ref[...]
    beta_ref[...] = jax.nn.sigmoid(small)


def _odd_prep(qkv, small, conv_w, dt_bias, a_log):
    ta, cdim = qkv.shape
    n_tiles = ta // ROW_TILE
    kdim = GDN_QK_HEADS * GDN_DK
    vdim = GDN_V_HEADS * GDN_DV
    cw = _conv_weights(conv_w, None)
    dtb = jnp.zeros((1, LANES), F32).at[0, :2 * SMALL].set(dt_bias.reshape(-1))
    aneg = jnp.zeros((1, LANES), F32).at[0, :2 * SMALL].set(-jnp.exp(a_log.reshape(-1)))
    prev_spec, next_spec = _halo_specs(cdim, n_tiles)
    row = lambda w: pl.BlockSpec((ROW_TILE, w), lambda i: (i, 0))
    full = lambda a: pl.BlockSpec(a.shape, lambda i: (0,) * a.ndim)
    return pl.pallas_call(
        functools.partial(_odd_prep_kernel, n_tiles=n_tiles),
        grid=(n_tiles,),
        in_specs=[row(cdim), prev_spec, next_spec, row(LANES), full(cw), full(dtb), full(aneg)],
        out_specs=[row(kdim), row(kdim), row(vdim), row(LANES), row(LANES)],
        out_shape=[jax.ShapeDtypeStruct((ta, w), F32) for w in (kdim, kdim, vdim, LANES, LANES)],
        scratch_shapes=[pltpu.VMEM((cdim // LANES, ROW_TILE + 2 * HALO, LANES), F32),
                        pltpu.VMEM((cdim // LANES, ROW_TILE, LANES), F32)],
        compiler_params=_cparams(("arbitrary",)),
        name="odd_prep",
    )(qkv, qkv, qkv, small, cw, dtb, aneg)


GDN_SCAN_INS = 5


def _gdn_kernel(*refs):
    ins = refs[:2 * GDN_SCAN_INS]
    (tri_ref, expand_ref, eye_ref, trit_ref, strict_ref, pair_ref, lvl_ref, eye2_ref,
     of_ref, ob_ref, st_ref) = refs[2 * GDN_SCAN_INS:]

    @pl.when(pl.program_id(0) == 0)
    def _():
        st_ref[...] = jnp.zeros(st_ref.shape, F32)

    pair = pair_ref[...]
    eye2 = eye2_ref[...]
    left = lax.broadcasted_iota(jnp.int32, (CHUNK, 2 * CHUNK), 1) < CHUNK
    n_levels = lvl_ref.shape[0]
    dirs = range(2)
    kk_qk = {}
    for d in dirs:
        q_ref, k_ref = ins[GDN_SCAN_INS * d:GDN_SCAN_INS * d + 2]
        for j in range(GDN_QK_HEADS):
            ksl = slice(j * GDN_DK, (j + 1) * GDN_DK)
            kb = k_ref[:, ksl].astype(BF16)
            krep = jnp.concatenate([kb, kb], axis=0)
            kk_qk[d, j] = (lax.dot_general(kb, krep, NT_DIMS, preferred_element_type=F32),
                           lax.dot_general(q_ref[:, ksl].astype(BF16), krep, NT_DIMS, preferred_element_type=F32))
    cols, decays = _decay_tables([tri_ref[d] for d in dirs], expand_ref[...], eye_ref[...],
                                 [trit_ref[d] for d in dirs],
                                 [ins[GDN_SCAN_INS * d + 3][:, d * SMALL:(d + 1) * SMALL] for d in dirs])
    beta_ts = [_sel_dot_r(ins[GDN_SCAN_INS * d + 4][:, (2 + d) * SMALL:(3 + d) * SMALL], expand_ref[...])
               for d in dirs]
    a_bds, rhs_all = [], []
    atts, qds, kds, gls = {}, {}, {}, {}
    for d in dirs:
        q_ref, k_ref, v_ref = ins[GDN_SCAN_INS * d:GDN_SCAN_INS * d + 3]
        last_row = 0 if d else CHUNK - 1
        col, decay, beta_t = cols[d], decays[d], beta_ts[d]
        a_all = beta_t * decay * strict_ref[d]
        for j in range(GDN_QK_HEADS):
            ksl = slice(j * GDN_DK, (j + 1) * GDN_DK)
            psl = slice(j * 2 * CHUNK, (j + 1) * 2 * CHUNK)
            kk, qk = kk_qk[d, j]
            atts[d, j] = (qk * decay[:, psl]).astype(BF16)
            a_pair = kk * a_all[:, psl]
            a_bds.append(jnp.concatenate([a_pair, a_pair], axis=0) * pair)
            colp, betap = col[:, psl], beta_t[:, psl]
            colr, betar = pltpu.roll(colp, CHUNK, 1), pltpu.roll(betap, CHUNK, 1)
            kj = k_ref[:, ksl]
            rhs = []
            for hh in range(2):
                hsl = slice((2 * j + hh) * GDN_DV, (2 * j + hh + 1) * GDN_DV)
                cumx = jnp.where(left, colp, colr) if hh == 0 else jnp.where(left, colr, colp)
                betax = jnp.where(left, betap, betar) if hh == 0 else jnp.where(left, betar, betap)
                lastx = cumx[last_row:last_row + 1, :]
                egx = jnp.exp2(cumx)
                qds[d, 2 * j + hh] = (q_ref[:, ksl] * egx).astype(BF16)
                kds[d, 2 * j + hh] = (kj * jnp.exp2(lastx - cumx)).astype(BF16)
                gls[d, 2 * j + hh] = jnp.exp2(lastx)
                rhs.append(jnp.concatenate([v_ref[:, hsl] * betax, kj * betax * egx], axis=1))
            rhs_all.append(jnp.concatenate(rhs, axis=0))
    tinvs = [(eye2 - a * lvl_ref[0]).astype(BF16) for a in a_bds]
    a16s = [a.astype(BF16) for a in a_bds]
    chain_dirs = [d for d in dirs for _ in range(GDN_QK_HEADS)]
    dot16 = lambda x, y: jnp.dot(x, y, preferred_element_type=F32)
    for lv in range(1, n_levels):
        b = 2 ** lv
        mask = lvl_ref[lv].astype(BF16)
        if b < 16:
            tls = [dot16(t, a * mask).astype(BF16) for t, a in zip(tinvs, a16s)]
            tinvs = [t - dot16(tl, t).astype(BF16) for t, tl in zip(tinvs, tls)]
            continue
        blocks = lambda t: [t[i * b:(i + 1) * b] for i in range(t.shape[0] // b)]
        moving = lambda d, t: jnp.concatenate(blocks(t)[(1 - d)::2], axis=0)
        tls = [dot16(moving(d, t), a * mask).astype(BF16) for d, t, a in zip(chain_dirs, tinvs, a16s)]
        news = [blocks(moving(d, t) - dot16(tl, t).astype(BF16)) for d, t, tl in zip(chain_dirs, tinvs, tls)]
        tinvs = [jnp.concatenate([new[i // 2] if i % 2 == 1 - d else blk for i, blk in enumerate(blocks(t))], axis=0)
                 for d, t, new in zip(chain_dirs, tinvs, news)]
    xs = [dot16(t, r.astype(BF16)) for t, r in zip(tinvs, rhs_all)]
    chains = [(d, h) for d in dirs for h in range(GDN_V_HEADS)]
    o_refs = (of_ref, ob_ref)
    head_rows = lambda x, h: x[(h % 2) * CHUNK:(h % 2 + 1) * CHUNK]
    x_of = lambda d, h: head_rows(xs[d * GDN_QK_HEADS + h // 2], h)
    ws_qs = [jnp.dot(jnp.concatenate([x_of(d, h)[:, GDN_DV:].astype(BF16), qds[d, h]], axis=0),
                     st_ref[d, h].astype(BF16), preferred_element_type=F32) for d, h in chains]
    v_new = [(x_of(d, h)[:, :GDN_DV] - r[:CHUNK]).astype(BF16) for (d, h), r in zip(chains, ws_qs)]
    for (d, h), r, vn in zip(chains, ws_qs, v_new):
        att = atts[d, h // 2][:, (h % 2) * CHUNK:(h % 2 + 1) * CHUNK]
        o_refs[d][:, h * GDN_DV:(h + 1) * GDN_DV] = (r[CHUNK:] + jnp.dot(att, vn, preferred_element_type=F32)
                                                     ).astype(o_refs[d].dtype)
    for (d, h), vn in zip(chains, v_new):
        st_ref[d, h] = st_ref[d, h] * gls[d, h] + lax.dot_general(kds[d, h], vn, TN_DIMS,
                                                                  preferred_element_type=F32)


def _gdn_scan(q, k, v, g, beta, n_ctx_chunks):
    ta, kdim = q.shape
    vdim = v.shape[1]
    nc = ta // CHUNK
    fwd = _scan_consts(False, GDN_V_HEADS, CHUNK)
    bwd = _scan_consts(True, GDN_V_HEADS, CHUNK)
    both = lambda name: jnp.stack([fwd[name], bwd[name]])
    consts = [both("tri3"), fwd["expand3"], fwd["eye_t"], both("tri_t"), both("strict_t"),
              _pair_mask(), _level_masks(), jnp.eye(2 * CHUNK, dtype=F32)]
    full = lambda a: pl.BlockSpec(a.shape, lambda i: (0,) * a.ndim)

    def row(rev, w):
        cmap = _chunk_map(rev, nc, n_ctx_chunks)
        return pl.BlockSpec((CHUNK, w), lambda i: (cmap(i), 0))

    dir_specs = lambda rev: [row(rev, kdim), row(rev, kdim), row(rev, vdim), row(rev, LANES), row(rev, LANES)]
    return pl.pallas_call(
        _gdn_kernel,
        grid=(nc,),
        in_specs=dir_specs(False) + dir_specs(True) + [full(a) for a in consts],
        out_specs=[row(False, vdim), row(True, vdim)],
        out_shape=[jax.ShapeDtypeStruct((ta, vdim), BF16)] * 2,
        scratch_shapes=[pltpu.VMEM((2, GDN_V_HEADS, GDN_DK, GDN_DV), F32)],
        compiler_params=_cparams(("arbitrary",)),
        name="gdn_scan",
    )(q, k, v, g, beta, q, k, v, g, beta, *consts)


def _odd_merge_kernel(of_ref, ob_ref, z_ref, nw_ref, w_ref, o_ref):
    f32 = lambda ref: ref[...].astype(F32)
    o = _group_rms(f32(of_ref) + f32(ob_ref), nw_ref[...], GDN_DV) * _silu(f32(z_ref))
    o_ref[...] = _bdot(o, w_ref[...])


def _odd_merge(of, ob, z, gdn_nw, w_out, n_ctx_tiles):
    ta, vdim = of.shape
    d = w_out.shape[1]
    n_lat = ta - n_ctx_tiles * ROW_TILE
    nw = jnp.tile(gdn_nw, GDN_V_HEADS).reshape(1, vdim)
    row_in = pl.BlockSpec((ROW_TILE, vdim), lambda i: (i + n_ctx_tiles, 0))
    return pl.pallas_call(
        _odd_merge_kernel,
        grid=(n_lat // ROW_TILE,),
        in_specs=[row_in] * 3 + [pl.BlockSpec((1, vdim), lambda i: (0, 0)), pl.BlockSpec(w_out.shape, lambda i: (0, 0))],
        out_specs=pl.BlockSpec((ROW_TILE, d), lambda i: (i, 0)),
        out_shape=jax.ShapeDtypeStruct((n_lat, d), F32),
        compiler_params=_cparams(("arbitrary",)),
        name="odd_merge",
    )(of, ob, z, nw, w_out)


def _ffn_kernel(*refs, n_ctx_tiles, final):
    if n_ctx_tiles:
        xc_ref, xl_ref, y_ref, mod_ref, nw_ref, w1_ref, w3_ref, w2_ref, fw_ref, oc_ref, ol_ref = refs
    else:
        xl_ref, y_ref, mod_ref, nw_ref, w1_ref, w3_ref, w2_ref, fw_ref, ol_ref = refs
    d = xl_ref.shape[1]
    is_ctx = pl.program_id(0) < n_ctx_tiles
    x = jnp.where(is_ctx, xc_ref[...], xl_ref[...]) if n_ctx_tiles else xl_ref[...]
    x = x + _mod_rows(mod_ref, is_ctx, 2, d) * y_ref[...]
    h = _norm_mod(x, nw_ref[...], _mod_rows(mod_ref, is_ctx, 3, d), _mod_rows(mod_ref, is_ctx, 4, d)).astype(BF16)
    a = jnp.dot(h, w1_ref[...], preferred_element_type=F32)
    b = jnp.dot(h, w3_ref[...], preferred_element_type=F32)
    out = x + _mod_rows(mod_ref, is_ctx, 5, d) * _bdot(_silu(a) * b, w2_ref[...])
    if final:
        ms = jnp.mean(out * out, axis=-1, keepdims=True)
        out = out * lax.rsqrt(ms + EPS) * fw_ref[...]
    if n_ctx_tiles:
        @pl.when(is_ctx)
        def _():
            oc_ref[...] = out

        @pl.when(jnp.logical_not(is_ctx))
        def _():
            ol_ref[...] = out
    else:
        ol_ref[...] = out


def _ffn(x_ctx, x_lat, y_all, mod, nw, w1, w3, w2, final_w, final):
    d = x_lat.shape[1]
    n_ctx_tiles = 0 if x_ctx is None else x_ctx.shape[0] // ROW_TILE
    n_tiles = n_ctx_tiles + x_lat.shape[0] // ROW_TILE
    row = pl.BlockSpec((ROW_TILE, d), lambda i: (i, 0))
    full = lambda a: pl.BlockSpec(a.shape, lambda i: (0,) * a.ndim)
    nw = nw.reshape(1, d)
    fw = final_w.reshape(1, d)
    params = [mod, nw, w1, w3, w2, fw]
    if n_ctx_tiles:
        x_specs, x_args = list(_ctx_lat_specs(n_ctx_tiles, d)), [x_ctx, x_lat]
        out_specs = list(_ctx_lat_specs(n_ctx_tiles, d))
        out_shape = [jax.ShapeDtypeStruct(x_ctx.shape, F32), jax.ShapeDtypeStruct(x_lat.shape, F32)]
    else:
        x_specs, x_args = [row], [x_lat]
        out_specs, out_shape = [row], [jax.ShapeDtypeStruct(x_lat.shape, F32)]
    outs = pl.pallas_call(
        functools.partial(_ffn_kernel, n_ctx_tiles=n_ctx_tiles, final=final),
        grid=(n_tiles,),
        in_specs=x_specs + [row] + [full(a) for a in params],
        out_specs=out_specs,
        out_shape=out_shape,
        compiler_params=_cparams(("arbitrary",)),
        name="ffn_final" if final else "ffn",
    )(*x_args, y_all, *params)
    return (outs[0], outs[1]) if n_ctx_tiles else (None, outs[0])


def _pad_cols(parts, width):
    w = jnp.concatenate(parts, axis=1)
    return jnp.pad(w, ((0, 0), (0, width - w.shape[1])))


def _even_w_in(w):
    inner = SSD_HEADS * SSD_HEAD_DIM
    conv = inner + 2 * SSD_GROUPS * SSD_STATE
    gkey = GLA_HEADS * GLA_DK
    gval = GLA_HEADS * GLA_DV
    sizes = (inner, conv, SMALL, SMALL, gkey, gkey, gval, gval, SMALL, SMALL)
    offs = [0]
    for s in sizes:
        offs.append(offs[-1] + s)
    z, xbc, dtf, dtb, q, k, v, r, glf, glb = (w[:, offs[i]:offs[i + 1]] for i in range(len(sizes)))
    small = _pad_cols([dtf, dtb, glf, glb], LANES)
    widths = (inner, conv, gkey, gkey, gval, gval, LANES)
    dtypes = (BF16, F32, F32, F32, F32, BF16, F32)
    return jnp.concatenate([z, xbc, q, k, v, r, small], axis=1).astype(BF16), widths, dtypes


def _odd_w_in(w):
    kdim = GDN_QK_HEADS * GDN_DK
    vdim = GDN_V_HEADS * GDN_DV
    conv = 2 * kdim + vdim
    small = _pad_cols([w[:, conv + vdim:]], LANES)
    widths = (conv, vdim, LANES)
    dtypes = (F32, BF16, F32)
    return jnp.concatenate([w[:, :conv + vdim], small], axis=1).astype(BF16), widths, dtypes


def _to_column_major(h):
    t, d = h.shape
    return h.reshape(t // GRID_W, GRID_W, d).transpose(1, 0, 2).reshape(t, d)


def _from_column_major(h):
    t, d = h.shape
    return h.reshape(GRID_W, t // GRID_W, d).transpose(1, 0, 2).reshape(t, d)


def kernel(x, c, ctx, c_ctx, mod_w, mod_b, norm_mix, norm_ffn, ffn_w1, ffn_w3, ffn_w2, ev_w_in, ssd_conv_w, ssd_conv_b, ssd_dt_bias, ssd_a_log, ssd_d, ssd_norm, gla_gate_w, gla_gate_b, gla_norm, ev_w_out, od_w_in, gdn_conv_w, gdn_dt_bias, gdn_a_log, gdn_norm, od_w_out, final_norm):
    batch, seq, d = x.shape
    n_ctx = ctx.shape[1]
    assert batch == 1 and n_ctx == ROW_TILE and seq % ROW_TILE == 0 and seq % GRID_W == 0
    assert mod_w.shape[0] == 2, "one even (SSD + GLA) layer followed by one odd (gated DeltaNet) layer"
    n_ctx_tiles = n_ctx // ROW_TILE
    n_ctx_chunks = n_ctx // CHUNK

    mods = _modulation(c, c_ctx, mod_w, mod_b)
    x_ctx, x_lat = ctx[0], x[0]

    w_in, widths, dtypes = _even_w_in(ev_w_in[0])
    z, xbc, q, k, v, r, small = _norm_proj(x_ctx, x_lat, norm_mix[0], mods[0], w_in, widths, dtypes)
    xs, bm, cm, dt, da, gk_f, gk_b = _even_prep(xbc, small, ssd_conv_w[0], ssd_conv_b[0], ssd_dt_bias[0],
                                                ssd_a_log[0], gla_gate_w[0], gla_gate_b[0])
    y_f, y_b = _ssd_scan(xs, bm, cm, dt, da, n_ctx_chunks)
    o_f, o_b = _gla_scan(q, k, v, gk_f, gk_b, n_ctx_chunks)
    mixed = _even_merge(y_f, y_b, xs, z, o_f, o_b, r, ssd_d[0], ssd_norm[0], gla_norm[0], ev_w_out[0].astype(BF16))
    x_ctx, x_lat = _ffn(x_ctx, x_lat, mixed, mods[0], norm_ffn[0], ffn_w1[0].astype(BF16), ffn_w3[0].astype(BF16),
                        ffn_w2[0].astype(BF16), final_norm, False)

    w_in, widths, dtypes = _odd_w_in(od_w_in[0])
    qkv, z, small = _norm_proj(x_ctx, _to_column_major(x_lat), norm_mix[1], mods[1], w_in, widths, dtypes)
    q, k, v, g, beta = _odd_prep(qkv, small, gdn_conv_w[0], gdn_dt_bias[0], gdn_a_log[0])
    o_f, o_b = _gdn_scan(q, k, v, g, beta, n_ctx_chunks)
    mixed = _odd_merge(o_f, o_b, z, gdn_norm[0], od_w_out[0].astype(BF16), n_ctx_tiles)
    _, out = _ffn(None, x_lat, _from_column_major(mixed), mods[1], norm_ffn[1], ffn_w1[1].astype(BF16),
                  ffn_w3[1].astype(BF16), ffn_w2[1].astype(BF16), final_norm, True)
    return out[None]
```

```python
import functools

import jax
import jax.numpy as jnp
from jax import lax
from jax.experimental import pallas as pl
from jax.experimental.pallas import tpu as pltpu

F32 = jnp.float32
BF16 = jnp.bfloat16
HIGHEST = lax.Precision.HIGHEST

EPS = 1e-6
LOG2_E = 1.4426950408889634
CHUNK = 64
SCAN_CPS = 4
GLA_CPS = 2
GDN_CPS = 2
GRID_W = 64
CONV_W = 5
ROW_TILE = 256
HALO = 8
LANES = 128
VMEM_LIMIT_BYTES = 56 * 1024 * 1024

SSD_HEADS = 16
SSD_HEAD_DIM = 64
SSD_GROUPS = 2
SSD_STATE = 128
GLA_HEADS = 4
GLA_DK = 128
GLA_DV = 256
GLA_GATE_RANK = 16
GLA_GATE_NORM = 16.0
GDN_QK_HEADS = 8
GDN_V_HEADS = 16
GDN_DK = 128
GDN_DV = 128
SMALL = 16

NT_DIMS = (((1,), (1,)), ((), ()))
TN_DIMS = (((0,), (0,)), ((), ()))


def _cparams(sem):
    return pltpu.CompilerParams(dimension_semantics=sem, vmem_limit_bytes=VMEM_LIMIT_BYTES)


def _bdot(a, b):
    return jnp.dot(a.astype(BF16), b.astype(BF16), preferred_element_type=F32)


def _bdot_nt(a, b):
    return lax.dot_general(a.astype(BF16), b.astype(BF16), NT_DIMS, preferred_element_type=F32)


def _bdot_tn(a, b):
    return lax.dot_general(a.astype(BF16), b.astype(BF16), TN_DIMS, preferred_element_type=F32)


def _split3(x):
    hi = x.astype(BF16)
    r1 = x - hi.astype(F32)
    mid = r1.astype(BF16)
    lo = (r1 - mid.astype(F32)).astype(BF16)
    return hi, mid, lo


def _sel_dot_l(sel3, x):
    return jnp.dot(sel3, jnp.concatenate(_split3(x), axis=0), preferred_element_type=F32)


def _sel_dot_r(x, sel3):
    pieces = jnp.concatenate([p.astype(F32) for p in _split3(x)], axis=1)
    return jnp.dot(pieces.astype(BF16), sel3, preferred_element_type=F32)


def _silu(x):
    return x * jax.nn.sigmoid(x)


def _softplus(x):
    return jnp.maximum(x, 0.0) + jnp.log1p(jnp.exp(-jnp.abs(x)))


def _mod_kernel(c_ref, w_ref, b_ref, o_ref):
    s = _silu(c_ref[...])
    o_ref[0] = jnp.dot(s, w_ref[0], precision=HIGHEST, preferred_element_type=F32) + b_ref[0]


def _modulation(c, c_ctx, mod_w, mod_b):
    depth, d, n = mod_w.shape
    cs = jnp.zeros((8, d), F32).at[0].set(c[0]).at[1].set(c_ctx)
    tn = n // 4
    return pl.pallas_call(
        _mod_kernel,
        grid=(depth, n // tn),
        in_specs=[pl.BlockSpec((8, d), lambda i, j: (0, 0)),
                  pl.BlockSpec((1, d, tn), lambda i, j: (i, 0, j)),
                  pl.BlockSpec((1, 1, tn), lambda i, j: (i, 0, j))],
        out_specs=pl.BlockSpec((1, 8, tn), lambda i, j: (i, 0, j)),
        out_shape=jax.ShapeDtypeStruct((depth, 8, n), F32),
        compiler_params=_cparams(("arbitrary", "arbitrary")),
        name="modulation",
    )(cs, mod_w, mod_b.reshape(depth, 1, n))


def _mod_rows(mod_ref, is_ctx, idx, d):
    sl = slice(idx * d, (idx + 1) * d)
    return jnp.where(is_ctx, mod_ref[1:2, sl], mod_ref[0:1, sl])


def _norm_mod(x, nw, shift, scale):
    ms = jnp.mean(x * x, axis=-1, keepdims=True)
    return (x * lax.rsqrt(ms + EPS) * nw) * (1.0 + scale) + shift


def _ctx_lat_specs(n_ctx_tiles, d):
    ctx = pl.BlockSpec((ROW_TILE, d), lambda i: (jnp.minimum(i, n_ctx_tiles - 1), 0))
    lat = pl.BlockSpec((ROW_TILE, d), lambda i: (jnp.maximum(i - n_ctx_tiles, 0), 0))
    return ctx, lat


def _proj_kernel(xc_ref, xl_ref, nw_ref, mod_ref, w_ref, *out_refs, widths, n_ctx_tiles):
    d = xl_ref.shape[1]
    is_ctx = pl.program_id(0) < n_ctx_tiles
    x = jnp.where(is_ctx, xc_ref[...], xl_ref[...])
    h = _norm_mod(x, nw_ref[...], _mod_rows(mod_ref, is_ctx, 0, d), _mod_rows(mod_ref, is_ctx, 1, d))
    h = h.astype(BF16)
    off = 0
    for o_ref, wd in zip(out_refs, widths):
        o_ref[...] = jnp.dot(h, w_ref[:, off:off + wd], preferred_element_type=F32).astype(o_ref.dtype)
        off += wd


def _norm_proj(x_ctx, x_lat, nw, mod, w, widths, dtypes):
    d = x_lat.shape[1]
    n_ctx_tiles = x_ctx.shape[0] // ROW_TILE
    ta = x_ctx.shape[0] + x_lat.shape[0]
    ntot = sum(widths)
    kern = functools.partial(_proj_kernel, widths=widths, n_ctx_tiles=n_ctx_tiles)
    return pl.pallas_call(
        kern,
        grid=(ta // ROW_TILE,),
        in_specs=[*_ctx_lat_specs(n_ctx_tiles, d),
                  pl.BlockSpec((1, d), lambda i: (0, 0)),
                  pl.BlockSpec(mod.shape, lambda i: (0, 0)),
                  pl.BlockSpec((d, ntot), lambda i: (0, 0))],
        out_specs=[pl.BlockSpec((ROW_TILE, wd), lambda i: (i, 0)) for wd in widths],
        out_shape=[jax.ShapeDtypeStruct((ta, wd), dt) for wd, dt in zip(widths, dtypes)],
        compiler_params=_cparams(("arbitrary",)),
        name="norm_proj",
    )(x_ctx, x_lat, nw.reshape(1, d), mod, w)


def _halo_specs(width, n_tiles):
    per = ROW_TILE // HALO
    prev = pl.BlockSpec((HALO, width), lambda i: (jnp.maximum(i * per - 1, 0), 0))
    nxt = pl.BlockSpec((HALO, width), lambda i: (jnp.minimum((i + 1) * per, n_tiles * per - 1), 0))
    return prev, nxt


CONV_BIAS_ROW = CONV_W


def _conv_weights(conv_w, conv_b):
    cdim = conv_w.shape[1]
    w = jnp.zeros((8, cdim), F32).at[:CONV_W].set(conv_w)
    if conv_b is not None:
        w = w.at[CONV_BIAS_ROW].set(conv_b)
    return w.reshape(8, cdim // LANES, LANES).transpose(1, 0, 2)


def _conv_silu(cur_ref, prev_ref, next_ref, w_ref, ext_ref, out_ref, n_tiles, posts):
    i = pl.program_id(0)
    prev_ok = jnp.logical_and(i != 0, i != 1)
    next_ok = jnp.logical_and(i != 0, i != n_tiles - 1)
    for c in range(ext_ref.shape[0]):
        lanes = slice(c * LANES, (c + 1) * LANES)
        ext_ref[c, 0:HALO, :] = jnp.where(prev_ok, prev_ref[:, lanes], 0.0)
        ext_ref[c, HALO:HALO + ROW_TILE, :] = cur_ref[:, lanes]
        ext_ref[c, HALO + ROW_TILE:, :] = jnp.where(next_ok, next_ref[:, lanes], 0.0)
    pad = CONV_W // 2
    stride = 8
    for lo, hi, post in posts:
        def slab(c, carry, post=post):
            w = w_ref[c]
            starts = [base + r for base in range(0, ROW_TILE, 8 * stride) for r in range(stride)]
            accs = [w[CONV_BIAS_ROW:CONV_BIAS_ROW + 1, :] for _ in starts]
            for j in range(CONV_W):
                taps = [ext_ref[c, pl.ds(HALO - pad + j + s, 8, stride=stride), :] for s in starts]
                accs = [acc + tap * w[j:j + 1, :] for acc, tap in zip(accs, taps)]
            ys = post([_silu(acc) for acc in accs])
            for s, y in zip(starts, ys):
                out_ref[c, pl.ds(s, 8, stride=stride), :] = y
            return carry
        lax.fori_loop(lo, hi, slab, 0)


def _even_prep_kernel(xbc_ref, prev_ref, next_ref, small_ref, cw_ref, dtb_ref, aneg_ref,
                      gw_ref, gb_ref, xs_ref, bm_ref, cm_ref, dt_ref, da_ref, gkf_ref, gkb_ref,
                      ext_ref, conv_ref, *, n_tiles):
    n_slabs = ext_ref.shape[0]
    _conv_silu(xbc_ref, prev_ref, next_ref, cw_ref, ext_ref, conv_ref, n_tiles, [(0, n_slabs, lambda ys: ys)])
    slab = 0
    for o_ref in (xs_ref, bm_ref, cm_ref):
        for c in range(o_ref.shape[1] // LANES):
            o_ref[:, c * LANES:(c + 1) * LANES] = conv_ref[slab]
            slab += 1
    small = small_ref[...]
    dt = _softplus(small + dtb_ref[...])
    dt_ref[...] = dt
    da_ref[...] = dt * aneg_ref[...]
    for d, o_ref in enumerate((gkf_ref, gkb_ref)):
        pre = jnp.dot(small, gw_ref[d], precision=HIGHEST, preferred_element_type=F32) + gb_ref[d]
        o_ref[...] = -_softplus(-pre) * (1.0 / GLA_GATE_NORM)


def _even_prep(xbc, small, conv_w, conv_b, dt_bias, a_log, gate_w, gate_b):
    ta, cdim = xbc.shape
    n_tiles = ta // ROW_TILE
    inner = SSD_HEADS * SSD_HEAD_DIM
    gs = SSD_GROUPS * SSD_STATE
    gkey = GLA_HEADS * GLA_DK
    cw = _conv_weights(conv_w, conv_b)
    dtb = jnp.zeros((1, LANES), F32).at[0, :2 * SMALL].set(dt_bias.reshape(-1))
    aneg = jnp.zeros((1, LANES), F32).at[0, :2 * SMALL].set(-jnp.exp(a_log.reshape(-1)))
    gw = jnp.zeros((2, LANES, gkey), F32)
    gw = gw.at[0, 2 * SMALL:3 * SMALL].set(gate_w[0]).at[1, 3 * SMALL:4 * SMALL].set(gate_w[1])
    prev_spec, next_spec = _halo_specs(cdim, n_tiles)
    row = lambda w: pl.BlockSpec((ROW_TILE, w), lambda i: (i, 0))
    full = lambda a: pl.BlockSpec(a.shape, lambda i: (0,) * a.ndim)
    gb = gate_b.reshape(2, 1, gkey)
    return pl.pallas_call(
        functools.partial(_even_prep_kernel, n_tiles=n_tiles),
        grid=(n_tiles,),
        in_specs=[row(cdim), prev_spec, next_spec, row(LANES), full(cw), full(dtb), full(aneg), full(gw), full(gb)],
        out_specs=[row(inner), row(gs), row(gs), row(LANES), row(LANES), row(gkey), row(gkey)],
        out_shape=[jax.ShapeDtypeStruct((ta, w), F32) for w in (inner, gs, gs, LANES, LANES, gkey, gkey)],
        scratch_shapes=[pltpu.VMEM((cdim // LANES, ROW_TILE + 2 * HALO, LANES), F32),
                        pltpu.VMEM((cdim // LANES, ROW_TILE, LANES), F32)],
        compiler_params=_cparams(("arbitrary",)),
        name="even_prep",
    )(xbc, xbc, xbc, small, cw, dtb, aneg, gw, gb)


def _scan_consts(rev, heads, lanes_per_head):
    r = jnp.arange(CHUNK)
    tri = (r[:, None] <= r[None, :]) if rev else (r[:, None] >= r[None, :])
    strict = (r[:, None] < r[None, :]) if rev else (r[:, None] > r[None, :])
    eye = r[:, None] == r[None, :]
    expand = (jnp.arange(heads)[:, None] == (jnp.arange(heads * lanes_per_head)[None, :] // lanes_per_head))
    return dict(
        tri3=jnp.tile(tri, (1, 3)).astype(BF16),
        expand3=jnp.tile(expand, (3, 1)).astype(BF16),
        tri_t=jnp.tile(tri, (1, heads)).astype(F32),
        strict_t=jnp.tile(strict, (1, heads)).astype(F32),
        eye_t=jnp.tile(eye, (1, heads)).astype(F32),
    )


def _pair_mask():
    r = jnp.arange(2 * CHUNK)
    return ((r[:, None] // CHUNK) == (r[None, :] // CHUNK)).astype(F32)


def _level_masks():
    r = jnp.arange(2 * CHUNK)
    same = lambda b: (r[:, None] // b) == (r[None, :] // b)
    sizes = [2 ** m for m in range(CHUNK.bit_length() - 1)]
    return jnp.stack([jnp.logical_and(same(2 * b), jnp.logical_not(same(b))) for b in sizes]).astype(F32)


def _chunk_map(rev, n_chunks, n_ctx_chunks):
    if not rev:
        return lambda i: i
    return lambda i: jnp.where(i < n_ctx_chunks, n_ctx_chunks - 1 - i, n_chunks + n_ctx_chunks - 1 - i)


def _decay_tables(tri3s, expand3, eye_t, tri_ts, gs):
    cums = [_sel_dot_l(tri3, g) * LOG2_E for tri3, g in zip(tri3s, gs)]
    cols = [_sel_dot_r(cum, expand3) for cum in cums]
    rowfs = [jnp.sum(col * eye_t, axis=0, keepdims=True) for col in cols]
    decays = [jnp.where(tri_t > 0.5, jnp.exp2(col - rowf), 0.0) for tri_t, col, rowf in zip(tri_ts, cols, rowfs)]
    return cols, decays


SSD_SCAN_INS = 5


def _ssd_kernel(*refs, cps):
    ins = refs[:2 * SSD_SCAN_INS]
    tri_ref, expand_ref, eye_ref, trit_ref, pair_ref, yf_ref, yb_ref, st_ref = refs[2 * SSD_SCAN_INS:]

    @pl.when(pl.program_id(0) == 0)
    def _():
        st_ref[...] = jnp.zeros(st_ref.shape, F32)

    pair = pair_ref[...]
    heads_g = SSD_HEADS // SSD_GROUPS
    gw = heads_g * SSD_HEAD_DIM
    pieces = [(d, c, g) for d in range(2) for c in range(cps) for g in range(SSD_GROUPS)]
    gsl = lambda g: slice(g * SSD_STATE, (g + 1) * SSD_STATE)
    lanes = lambda g: slice(g * gw, (g + 1) * gw)
    rows = lambda d, c: slice((cps - 1 - c if d else c) * CHUNK, (cps - c if d else c + 1) * CHUNK)
    chunks = [(d, c) for d in range(2) for c in range(cps)]
    small = lambda d, c, k: ins[SSD_SCAN_INS * d + k][rows(d, c), d * SMALL:(d + 1) * SMALL]
    cols, decays = _decay_tables([tri_ref[d] for d, _ in chunks], expand_ref[...], eye_ref[...],
                                 [trit_ref[d] for d, _ in chunks], [small(d, c, 4) for d, c in chunks])
    dtes = [_sel_dot_r(small(d, c, 3), expand_ref[...]) for d, c in chunks]
    tabs = {}
    for (d, c), col, decay, dte in zip(chunks, cols, decays, dtes):
        last_row = 0 if d else CHUNK - 1
        xdt = ins[SSD_SCAN_INS * d][rows(d, c), :] * dte
        last = col[last_row:last_row + 1, :]
        tabs[d, c] = dict(decay=decay, xdt=xdt, xw=xdt * jnp.exp2(last - col), chunk_decay=jnp.exp2(last),
                          eac=jnp.exp2(col))
    bgs = {(d, c, g): ins[SSD_SCAN_INS * d + 1][rows(d, c), gsl(g)].astype(BF16) for d, c, g in pieces}
    cgs = {(d, c, g): ins[SSD_SCAN_INS * d + 2][rows(d, c), gsl(g)].astype(BF16) for d, c, g in pieces}
    wms = {p: lax.dot_general(cgs[p], jnp.concatenate([bgs[p]] * heads_g, axis=0), NT_DIMS,
                              preferred_element_type=F32) * tabs[p[:2]]["decay"][:, lanes(p[2])] for p in pieces}
    news = {(d, c, g): lax.dot_general(bgs[d, c, g], tabs[d, c]["xw"][:, lanes(g)].astype(BF16), TN_DIMS,
                                       preferred_element_type=F32) for d, c, g in pieces}
    diags = {}
    for d, c, g in pieces:
        xg = tabs[d, c]["xdt"][:, lanes(g)]
        for j in range(gw // LANES):
            xp = xg[:, j * LANES:(j + 1) * LANES]
            blockdiag = jnp.concatenate([xp, xp], axis=0) * pair
            diags[d, c, g, j] = _bdot(wms[d, c, g][:, j * LANES:(j + 1) * LANES], blockdiag)
    states = {}
    for d in range(2):
        for g in range(SSD_GROUPS):
            state = st_ref[d, g]
            for c in range(cps):
                states[d, c, g] = state
                state = state * tabs[d, c]["chunk_decay"][:, lanes(g)] + news[d, c, g]
            st_ref[d, g] = state
    offs = {p: jnp.dot(cgs[p], states[p].astype(BF16), preferred_element_type=F32) for p in pieces}
    for d, y_ref in enumerate((yf_ref, yb_ref)):
        for c in range(cps):
            for g in range(SSD_GROUPS):
                diag = jnp.concatenate([diags[d, c, g, j] for j in range(gw // LANES)], axis=1)
                y_ref[rows(d, c), lanes(g)] = (diag + offs[d, c, g] * tabs[d, c]["eac"][:, lanes(g)]).astype(y_ref.dtype)


def _ssd_scan(xs, bm, cm, dt, da, n_ctx_chunks):
    ta, inner = xs.shape
    cps = SCAN_CPS
    nb = ta // (cps * CHUNK)
    fwd = _scan_consts(False, SSD_HEADS, SSD_HEAD_DIM)
    bwd = _scan_consts(True, SSD_HEADS, SSD_HEAD_DIM)
    both = lambda name: jnp.stack([fwd[name], bwd[name]])
    consts = [both("tri3"), fwd["expand3"], fwd["eye_t"], both("tri_t"), _pair_mask()]
    full = lambda a: pl.BlockSpec(a.shape, lambda i: (0,) * a.ndim)
    gs = SSD_GROUPS * SSD_STATE

    def row(rev, w):
        cmap = _chunk_map(rev, nb, n_ctx_chunks // cps)
        return pl.BlockSpec((cps * CHUNK, w), lambda i: (cmap(i), 0))

    dir_specs = lambda rev: [row(rev, inner), row(rev, gs), row(rev, gs), row(rev, LANES), row(rev, LANES)]
    return pl.pallas_call(
        functools.partial(_ssd_kernel, cps=cps),
        grid=(nb,),
        in_specs=dir_specs(False) + dir_specs(True) + [full(a) for a in consts],
        out_specs=[row(False, inner), row(True, inner)],
        out_shape=[jax.ShapeDtypeStruct((ta, inner), BF16)] * 2,
        scratch_shapes=[pltpu.VMEM((2, SSD_GROUPS, SSD_STATE, inner // SSD_GROUPS), F32)],
        compiler_params=_cparams(("arbitrary",)),
        name="ssd_scan",
    )(xs, bm, cm, dt, da, xs, bm, cm, dt, da, *consts)


GLA_SUB = 8
GLA_LEVELS = (8, 16, 32)
GLA_SCAN_INS = 4


def _gla_masks():
    r = jnp.arange(CHUNK)
    same = lambda b: (r[:, None] // b) == (r[None, :] // b)
    out = []
    for rev in (False, True):
        strict = (r[:, None] < r[None, :]) if rev else (r[:, None] > r[None, :])
        lv = [same(2 * b) & ~same(b) & strict for b in GLA_LEVELS]
        within = r % GLA_SUB
        dg = []
        for s in range(GLA_SUB):
            seen = (within <= s) if rev else (within >= s)
            dg.append((r[None, :] == (r[:, None] // GLA_SUB) * GLA_SUB + s) & seen[:, None])
        out.append(jnp.stack(lv + dg))
    return jnp.stack(out).astype(F32)


def _block_rows(a, rows, b):
    return jnp.concatenate([jnp.broadcast_to(a[r:r + 1, :], (b, a.shape[1])) for r in rows], axis=0)


def _gla_kernel(*refs, cps):
    ins = refs[:2 * GLA_SCAN_INS]
    tri_ref, mask_ref, of_ref, ob_ref, st_ref = refs[2 * GLA_SCAN_INS:]

    @pl.when(pl.program_id(0) == 0)
    def _():
        st_ref[...] = jnp.zeros(st_ref.shape, F32)

    pieces = [(d, c, h) for d in range(2) for c in range(cps) for h in range(GLA_HEADS)]
    ksl = lambda h: slice(h * GLA_DK, (h + 1) * GLA_DK)
    vsl = lambda h: slice(h * GLA_DV, (h + 1) * GLA_DV)
    rws = lambda d, c: slice((cps - 1 - c if d else c) * CHUNK, (cps - c if d else c + 1) * CHUNK)
    q_of = lambda d, c, h: ins[GLA_SCAN_INS * d][rws(d, c), ksl(h)] * (GLA_DK ** -0.5)
    k_of = lambda d, c, h: ins[GLA_SCAN_INS * d + 1][rws(d, c), ksl(h)]
    v_of = lambda d, c, h: ins[GLA_SCAN_INS * d + 2][rws(d, c), vsl(h)]
    within = lax.broadcasted_iota(jnp.int32, (CHUNK, 1), 0) % GLA_SUB
    gcs = {(d, c, h): _sel_dot_l(tri_ref[d], ins[GLA_SCAN_INS * d + 3][rws(d, c), ksl(h)]) * LOG2_E
           for d, c, h in pieces}
    news = {}
    for d, c, h in pieces:
        gc = gcs[d, c, h]
        last = 0 if d else CHUNK - 1
        glast = gc[last:last + 1, :]
        news[d, c, h] = (jnp.exp2(glast), _bdot_tn(v_of(d, c, h), k_of(d, c, h) * jnp.exp2(glast - gc)))
    states = {}
    for d in range(2):
        for h in range(GLA_HEADS):
            state = st_ref[d, h]
            for c in range(cps):
                states[d, c, h] = state
                state = state * news[d, c, h][0] + news[d, c, h][1]
            st_ref[d, h] = state
    inters = {p: _bdot_nt(q_of(*p) * jnp.exp2(gcs[p]), states[p]) for p in pieces}
    atts = {}
    for d, c, h in pieces:
        gc, qh, kh = gcs[d, c, h], q_of(d, c, h), k_of(d, c, h)
        att = None
        for li, b in enumerate(GLA_LEVELS):
            nb = CHUNK // b
            if d:
                rq = [min(b * (i + 1), CHUNK - 1) for i in range(nb)]
                rk = [b * i for i in range(nb)]
            else:
                rq = [max(b * i - 1, 0) for i in range(nb)]
                rk = [b * (i + 1) - 1 for i in range(nb)]
            qs = qh * jnp.exp2(gc - _block_rows(gc, rq, b))
            ks = kh * jnp.exp2(_block_rows(gc, rk, b) - gc)
            part = _bdot_nt(qs, ks) * mask_ref[d, li]
            att = part if att is None else att + part
        for s in range(GLA_SUB):
            rows = [GLA_SUB * i + s for i in range(CHUNK // GLA_SUB)]
            seen = (within <= s) if d else (within >= s)
            e = jnp.where(seen, jnp.exp2(gc - _block_rows(gc, rows, GLA_SUB)), 0.0)
            a = jnp.sum(qh * _block_rows(kh, rows, GLA_SUB) * e, axis=-1, keepdims=True)
            att = att + a * mask_ref[d, len(GLA_LEVELS) + s]
        atts[d, c, h] = att
    for d, o_ref in enumerate((of_ref, ob_ref)):
        for c in range(cps):
            for h in range(GLA_HEADS):
                o_ref[rws(d, c), vsl(h)] = (inters[d, c, h] + _bdot(atts[d, c, h], v_of(d, c, h))).astype(o_ref.dtype)


def _gla_scan(q, k, v, gk_f, gk_b, n_ctx_chunks):
    ta, kdim = q.shape
    vdim = v.shape[1]
    cps = GLA_CPS
    nb = ta // (cps * CHUNK)
    tri3 = jnp.stack([_scan_consts(rev, 1, 1)["tri3"] for rev in (False, True)])
    masks = _gla_masks()
    full = lambda a: pl.BlockSpec(a.shape, lambda i: (0,) * a.ndim)

    def row(rev, w):
        cmap = _chunk_map(rev, nb, n_ctx_chunks // cps)
        return pl.BlockSpec((cps * CHUNK, w), lambda i: (cmap(i), 0))

    dir_specs = lambda rev: [row(rev, kdim), row(rev, kdim), row(rev, vdim), row(rev, kdim)]
    return pl.pallas_call(
        functools.partial(_gla_kernel, cps=cps),
        grid=(nb,),
        in_specs=dir_specs(False) + dir_specs(True) + [full(tri3), full(masks)],
        out_specs=[row(False, vdim), row(True, vdim)],
        out_shape=[jax.ShapeDtypeStruct((ta, vdim), BF16)] * 2,
        scratch_shapes=[pltpu.VMEM((2, GLA_HEADS, GLA_DV, GLA_DK), F32)],
        compiler_params=_cparams(("arbitrary",)),
        name="gla_scan",
    )(q, k, v, gk_f, q, k, v, gk_b, tri3, masks)


def _group_rms(y, w, width):
    parts = []
    for g in range(y.shape[1] // width):
        yg = y[:, g * width:(g + 1) * width]
        ms = jnp.mean(yg * yg, axis=-1, keepdims=True)
        parts.append(yg * lax.rsqrt(ms + EPS))
    return jnp.concatenate(parts, axis=1) * w


def _even_merge_kernel(yf_ref, yb_ref, xs_ref, z_ref, of_ref, ob_ref, r_ref, dsk_ref, snw_ref, gnw_ref, w_ref, o_ref):
    inner = xs_ref.shape[1]
    f32 = lambda ref: ref[...].astype(F32)
    y = (f32(yf_ref) + f32(yb_ref) + dsk_ref[...] * xs_ref[...]) * _silu(f32(z_ref))
    y = _group_rms(y, snw_ref[...], inner // SSD_GROUPS)
    o = _group_rms(f32(of_ref) + f32(ob_ref), gnw_ref[...], GLA_DV) * _silu(f32(r_ref))
    o_ref[...] = _bdot(y, w_ref[:inner, :]) + _bdot(o, w_ref[inner:, :])


def _even_merge(yf, yb, xs, z, of, ob, r, d_skip, ssd_nw, gla_nw, w_out):
    ta, inner = xs.shape
    vdim = of.shape[1]
    d = w_out.shape[1]
    dsk = jnp.repeat(d_skip, SSD_HEAD_DIM).reshape(1, inner)
    gnw = jnp.tile(gla_nw, GLA_HEADS).reshape(1, vdim)
    row = lambda w: pl.BlockSpec((ROW_TILE, w), lambda i: (i, 0))
    vec = lambda w: pl.BlockSpec((1, w), lambda i: (0, 0))
    return pl.pallas_call(
        _even_merge_kernel,
        grid=(ta // ROW_TILE,),
        in_specs=[row(inner)] * 4 + [row(vdim)] * 3 + [vec(inner), vec(inner), vec(vdim),
                                                        pl.BlockSpec(w_out.shape, lambda i: (0, 0))],
        out_specs=row(d),
        out_shape=jax.ShapeDtypeStruct((ta, d), F32),
        compiler_params=_cparams(("arbitrary",)),
        name="even_merge",
    )(yf, yb, xs, z, of, ob, r, dsk, ssd_nw.reshape(1, inner), gnw, w_out)


def _odd_prep_kernel(qkv_ref, prev_ref, next_ref, small_ref, cw_ref, dtb_ref, aneg_ref,
                     q_ref, k_ref, v_ref, g_ref, beta_ref, ext_ref, conv_ref, *, n_tiles):
    assert GDN_DK == LANES
    kslabs = q_ref.shape[1] // LANES
    n_slabs = ext_ref.shape[0]
    def l2n(scale):
        def post(groups):
            sums = [jnp.sum(a * a, axis=-1, keepdims=True) for a in groups]
            return [a * (lax.rsqrt(s + EPS) * scale) for a, s in zip(groups, sums)]
        return post

    _conv_silu(qkv_ref, prev_ref, next_ref, cw_ref, ext_ref, conv_ref, n_tiles,
               [(0, kslabs, l2n(GDN_DK ** -0.5)), (kslabs, 2 * kslabs, l2n(1.0)), (2 * kslabs, n_slabs, lambda ys: ys)])
    slab = 0
    for o_ref in (q_ref, k_ref, v_ref):
        for c in range(o_ref.shape[1] // LANES):
            o_ref[:, c * LANES:(c + 1) * LANES] = conv_ref[slab]
            slab += 1
    small = small_ref[...]
    g_ref[...] = _softplus(small + dtb_ref[...]) * aneg_ref[...]
    beta_ref[...] = jax.nn.sigmoid(small)


def _odd_prep(qkv, small, conv_w, dt_bias, a_log):
    ta, cdim = qkv.shape
    n_tiles = ta // ROW_TILE
    kdim = GDN_QK_HEADS * GDN_DK
    vdim = GDN_V_HEADS * GDN_DV
    cw = _conv_weights(conv_w, None)
    dtb = jnp.zeros((1, LANES), F32).at[0, :2 * SMALL].set(dt_bias.reshape(-1))
    aneg = jnp.zeros((1, LANES), F32).at[0, :2 * SMALL].set(-jnp.exp(a_log.reshape(-1)))
    prev_spec, next_spec = _halo_specs(cdim, n_tiles)
    row = lambda w: pl.BlockSpec((ROW_TILE, w), lambda i: (i, 0))
    full = lambda a: pl.BlockSpec(a.shape, lambda i: (0,) * a.ndim)
    return pl.pallas_call(
        functools.partial(_odd_prep_kernel, n_tiles=n_tiles),
        grid=(n_tiles,),
        in_specs=[row(cdim), prev_spec, next_spec, row(LANES), full(cw), full(dtb), full(aneg)],
        out_specs=[row(kdim), row(kdim), row(vdim), row(LANES), row(LANES)],
        out_shape=[jax.ShapeDtypeStruct((ta, w), F32) for w in (kdim, kdim, vdim, LANES, LANES)],
        scratch_shapes=[pltpu.VMEM((cdim // LANES, ROW_TILE + 2 * HALO, LANES), F32),
                        pltpu.VMEM((cdim // LANES, ROW_TILE, LANES), F32)],
        compiler_params=_cparams(("arbitrary",)),
        name="odd_prep",
    )(qkv, qkv, qkv, small, cw, dtb, aneg)


GDN_SCAN_INS = 5


def _interleave(phases):
    phases = list(phases)
    while phases:
        for ph in list(phases):
            if next(ph, StopIteration) is StopIteration:
                phases.remove(ph)


def _gdn_kernel(*refs, cps):
    ins = refs[:2 * GDN_SCAN_INS]
    (tri_ref, expand_ref, eye_ref, trit_ref, strict_ref, pair_ref, lvl_ref, eye2_ref,
     of_ref, ob_ref, st_ref) = refs[2 * GDN_SCAN_INS:]

    @pl.when(pl.program_id(0) == 0)
    def _():
        st_ref[...] = jnp.zeros(st_ref.shape, F32)

    pair = pair_ref[...]
    eye2 = eye2_ref[...]
    left = lax.broadcasted_iota(jnp.int32, (CHUNK, 2 * CHUNK), 1) < CHUNK
    n_levels = lvl_ref.shape[0]
    dirs = range(2)
    o_refs = (of_ref, ob_ref)
    dot16 = lambda x, y: jnp.dot(x, y, preferred_element_type=F32)
    rows = lambda d, c: slice((cps - 1 - c if d else c) * CHUNK, (cps - c if d else c + 1) * CHUNK)
    pairs = [(d, j) for d in dirs for j in range(GDN_QK_HEADS)]
    chains = [(d, h) for d in dirs for h in range(GDN_V_HEADS)]

    keys = [(c, d) for c in range(cps) for d in dirs]
    cols, decays = _decay_tables([tri_ref[d] for _, d in keys], expand_ref[...], eye_ref[...],
                                 [trit_ref[d] for _, d in keys],
                                 [ins[GDN_SCAN_INS * d + 3][rows(d, c), d * SMALL:(d + 1) * SMALL] for c, d in keys])
    beta_ts = [_sel_dot_r(ins[GDN_SCAN_INS * d + 4][rows(d, c), (2 + d) * SMALL:(3 + d) * SMALL], expand_ref[...])
               for c, d in keys]
    tabs = {k: t for k, t in zip(keys, zip(cols, decays, beta_ts))}

    def build(c, out):
        for d in dirs:
            q_ref, k_ref, v_ref = ins[GDN_SCAN_INS * d:GDN_SCAN_INS * d + 3]
            rw = rows(d, c)
            last_row = 0 if d else CHUNK - 1
            col, decay, beta_t = tabs[c, d]
            a_all = beta_t * decay * strict_ref[d]
            for j in range(GDN_QK_HEADS):
                ksl = slice(j * GDN_DK, (j + 1) * GDN_DK)
                psl = slice(j * 2 * CHUNK, (j + 1) * 2 * CHUNK)
                kj = k_ref[rw, ksl]
                kb = kj.astype(BF16)
                krep = jnp.concatenate([kb, kb], axis=0)
                kk = lax.dot_general(kb, krep, NT_DIMS, preferred_element_type=F32)
                qk = lax.dot_general(q_ref[rw, ksl].astype(BF16), krep, NT_DIMS, preferred_element_type=F32)
                out["att"][d, j] = (qk * decay[:, psl]).astype(BF16)
                a_pair = kk * a_all[:, psl]
                a_bd = jnp.concatenate([a_pair, a_pair], axis=0) * pair
                out["tinv"][d, j] = (eye2 - a_bd * lvl_ref[0]).astype(BF16)
                out["a16"][d, j] = a_bd.astype(BF16)
                colp, betap = col[:, psl], beta_t[:, psl]
                colr, betar = pltpu.roll(colp, CHUNK, 1), pltpu.roll(betap, CHUNK, 1)
                rhs = []
                for hh in range(2):
                    h = 2 * j + hh
                    hsl = slice(h * GDN_DV, (h + 1) * GDN_DV)
                    cumx = jnp.where(left, colp, colr) if hh == 0 else jnp.where(left, colr, colp)
                    betax = jnp.where(left, betap, betar) if hh == 0 else jnp.where(left, betar, betap)
                    lastx = cumx[last_row:last_row + 1, :]
                    egx = jnp.exp2(cumx)
                    out["qd"][d, h] = (q_ref[rw, ksl] * egx).astype(BF16)
                    out["kd"][d, h] = (kj * jnp.exp2(lastx - cumx)).astype(BF16)
                    out["gl"][d, h] = jnp.exp2(lastx)
                    rhs.append(jnp.concatenate([v_ref[rw, hsl] * betax, kj * betax * egx], axis=1))
                out["rhs"][d, j] = jnp.concatenate(rhs, axis=0).astype(BF16)
                yield

    def invert(out):
        tinv, a16 = out["tinv"], out["a16"]
        for lv in range(1, n_levels):
            b = 2 ** lv
            mask = lvl_ref[lv].astype(BF16)
            if b < 16:
                tls = {p: dot16(tinv[p], a16[p] * mask).astype(BF16) for p in pairs}
                yield
                for p in pairs:
                    tinv[p] = tinv[p] - dot16(tls[p], tinv[p]).astype(BF16)
                yield
                continue
            blocks = lambda t: [t[i * b:(i + 1) * b] for i in range(t.shape[0] // b)]
            moving = lambda d, t: jnp.concatenate(blocks(t)[(1 - d)::2], axis=0)
            tls = {p: dot16(moving(p[0], tinv[p]), a16[p] * mask).astype(BF16) for p in pairs}
            yield
            for p in pairs:
                new = blocks(moving(p[0], tinv[p]) - dot16(tls[p], tinv[p]).astype(BF16))
                tinv[p] = jnp.concatenate([new[i // 2] if i % 2 == 1 - p[0] else blk
                                           for i, blk in enumerate(blocks(tinv[p]))], axis=0)
            yield
        out["x"] = {p: dot16(tinv[p], out["rhs"][p]) for p in pairs}
        yield

    def recur(c, out, states):
        x_of = lambda d, h: out["x"][d, h // 2][(h % 2) * CHUNK:(h % 2 + 1) * CHUNK]
        ws_qs = {(d, h): dot16(jnp.concatenate([x_of(d, h)[:, GDN_DV:].astype(BF16), out["qd"][d, h]], axis=0),
                               states[d, h].astype(BF16)) for d, h in chains}
        yield
        v_new = {(d, h): (x_of(d, h)[:, :GDN_DV] - ws_qs[d, h][:CHUNK]).astype(BF16) for d, h in chains}
        for d, h in chains:
            att = out["att"][d, h // 2][:, (h % 2) * CHUNK:(h % 2 + 1) * CHUNK]
            o_refs[d][rows(d, c), h * GDN_DV:(h + 1) * GDN_DV] = (
                ws_qs[d, h][CHUNK:] + dot16(att, v_new[d, h])).astype(o_refs[d].dtype)
        yield
        for d, h in chains:
            states[d, h] = states[d, h] * out["gl"][d, h] + lax.dot_general(
                out["kd"][d, h], v_new[d, h], TN_DIMS, preferred_element_type=F32)
        yield

    fields = ("att", "tinv", "a16", "rhs", "qd", "kd", "gl")
    outs = [{f: {} for f in fields} for _ in range(cps)]
    states = {(d, h): st_ref[d, h] for d, h in chains}
    _interleave([build(0, outs[0])])
    for c in range(cps):
        phases = [invert(outs[c])]
        if c + 1 < cps:
            phases.append(build(c + 1, outs[c + 1]))
        if c > 0:
            phases.append(recur(c - 1, outs[c - 1], states))
        _interleave(phases)
    _interleave([recur(cps - 1, outs[cps - 1], states)])
    for d, h in chains:
        st_ref[d, h] = states[d, h]


def _gdn_scan(q, k, v, g, beta, n_ctx_chunks):
    ta, kdim = q.shape
    vdim = v.shape[1]
    cps = GDN_CPS
    nb = ta // (cps * CHUNK)
    fwd = _scan_consts(False, GDN_V_HEADS, CHUNK)
    bwd = _scan_consts(True, GDN_V_HEADS, CHUNK)
    both = lambda name: jnp.stack([fwd[name], bwd[name]])
    consts = [both("tri3"), fwd["expand3"], fwd["eye_t"], both("tri_t"), both("strict_t"),
              _pair_mask(), _level_masks(), jnp.eye(2 * CHUNK, dtype=F32)]
    full = lambda a: pl.BlockSpec(a.shape, lambda i: (0,) * a.ndim)

    def row(rev, w):
        cmap = _chunk_map(rev, nb, n_ctx_chunks // cps)
        return pl.BlockSpec((cps * CHUNK, w), lambda i: (cmap(i), 0))

    dir_specs = lambda rev: [row(rev, kdim), row(rev, kdim), row(rev, vdim), row(rev, LANES), row(rev, LANES)]
    return pl.pallas_call(
        functools.partial(_gdn_kernel, cps=cps),
        grid=(nb,),
        in_specs=dir_specs(False) + dir_specs(True) + [full(a) for a in consts],
        out_specs=[row(False, vdim), row(True, vdim)],
        out_shape=[jax.ShapeDtypeStruct((ta, vdim), BF16)] * 2,
        scratch_shapes=[pltpu.VMEM((2, GDN_V_HEADS, GDN_DK, GDN_DV), F32)],
        compiler_params=_cparams(("arbitrary",)),
        name="gdn_scan",
    )(q, k, v, g, beta, q, k, v, g, beta, *consts)


def _odd_merge_kernel(of_ref, ob_ref, z_ref, nw_ref, w_ref, o_ref):
    f32 = lambda ref: ref[...].astype(F32)
    o = _group_rms(f32(of_ref) + f32(ob_ref), nw_ref[...], GDN_DV) * _silu(f32(z_ref))
    o_ref[...] = _bdot(o, w_ref[...])


def _odd_merge(of, ob, z, gdn_nw, w_out, n_ctx_tiles):
    ta, vdim = of.shape
    d = w_out.shape[1]
    n_lat = ta - n_ctx_tiles * ROW_TILE
    nw = jnp.tile(gdn_nw, GDN_V_HEADS).reshape(1, vdim)
    row_in = pl.BlockSpec((ROW_TILE, vdim), lambda i: (i + n_ctx_tiles, 0))
    return pl.pallas_call(
        _odd_merge_kernel,
        grid=(n_lat // ROW_TILE,),
        in_specs=[row_in] * 3 + [pl.BlockSpec((1, vdim), lambda i: (0, 0)), pl.BlockSpec(w_out.shape, lambda i: (0, 0))],
        out_specs=pl.BlockSpec((ROW_TILE, d), lambda i: (i, 0)),
        out_shape=jax.ShapeDtypeStruct((n_lat, d), F32),
        compiler_params=_cparams(("arbitrary",)),
        name="odd_merge",
    )(of, ob, z, nw, w_out)


def _ffn_kernel(*refs, n_ctx_tiles, final):
    if n_ctx_tiles:
        xc_ref, xl_ref, y_ref, mod_ref, nw_ref, w1_ref, w3_ref, w2_ref, fw_ref, oc_ref, ol_ref = refs
    else:
        xl_ref, y_ref, mod_ref, nw_ref, w1_ref, w3_ref, w2_ref, fw_ref, ol_ref = refs
    d = xl_ref.shape[1]
    is_ctx = pl.program_id(0) < n_ctx_tiles
    x = jnp.where(is_ctx, xc_ref[...], xl_ref[...]) if n_ctx_tiles else xl_ref[...]
    x = x + _mod_rows(mod_ref, is_ctx, 2, d) * y_ref[...]
    h = _norm_mod(x, nw_ref[...], _mod_rows(mod_ref, is_ctx, 3, d), _mod_rows(mod_ref, is_ctx, 4, d)).astype(BF16)
    a = jnp.dot(h, w1_ref[...], preferred_element_type=F32)
    b = jnp.dot(h, w3_ref[...], preferred_element_type=F32)
    out = x + _mod_rows(mod_ref, is_ctx, 5, d) * _bdot(_silu(a) * b, w2_ref[...])
    if final:
        ms = jnp.mean(out * out, axis=-1, keepdims=True)
        out = out * lax.rsqrt(ms + EPS) * fw_ref[...]
    if n_ctx_tiles:
        @pl.when(is_ctx)
        def _():
            oc_ref[...] = out

        @pl.when(jnp.logical_not(is_ctx))
        def _():
            ol_ref[...] = out
    else:
        ol_ref[...] = out


def _ffn(x_ctx, x_lat, y_all, mod, nw, w1, w3, w2, final_w, final):
    d = x_lat.shape[1]
    n_ctx_tiles = 0 if x_ctx is None else x_ctx.shape[0] // ROW_TILE
    n_tiles = n_ctx_tiles + x_lat.shape[0] // ROW_TILE
    row = pl.BlockSpec((ROW_TILE, d), lambda i: (i, 0))
    full = lambda a: pl.BlockSpec(a.shape, lambda i: (0,) * a.ndim)
    nw = nw.reshape(1, d)
    fw = final_w.reshape(1, d)
    params = [mod, nw, w1, w3, w2, fw]
    if n_ctx_tiles:
        x_specs, x_args = list(_ctx_lat_specs(n_ctx_tiles, d)), [x_ctx, x_lat]
        out_specs = list(_ctx_lat_specs(n_ctx_tiles, d))
        out_shape = [jax.ShapeDtypeStruct(x_ctx.shape, F32), jax.ShapeDtypeStruct(x_lat.shape, F32)]
    else:
        x_specs, x_args = [row], [x_lat]
        out_specs, out_shape = [row], [jax.ShapeDtypeStruct(x_lat.shape, F32)]
    outs = pl.pallas_call(
        functools.partial(_ffn_kernel, n_ctx_tiles=n_ctx_tiles, final=final),
        grid=(n_tiles,),
        in_specs=x_specs + [row] + [full(a) for a in params],
        out_specs=out_specs,
        out_shape=out_shape,
        compiler_params=_cparams(("arbitrary",)),
        name="ffn_final" if final else "ffn",
    )(*x_args, y_all, *params)
    return (outs[0], outs[1]) if n_ctx_tiles else (None, outs[0])


def _pad_cols(parts, width):
    w = jnp.concatenate(parts, axis=1)
    return jnp.pad(w, ((0, 0), (0, width - w.shape[1])))


def _even_w_in(w):
    inner = SSD_HEADS * SSD_HEAD_DIM
    conv = inner + 2 * SSD_GROUPS * SSD_STATE
    gkey = GLA_HEADS * GLA_DK
    gval = GLA_HEADS * GLA_DV
    sizes = (inner, conv, SMALL, SMALL, gkey, gkey, gval, gval, SMALL, SMALL)
    offs = [0]
    for s in sizes:
        offs.append(offs[-1] + s)
    z, xbc, dtf, dtb, q, k, v, r, glf, glb = (w[:, offs[i]:offs[i + 1]] for i in range(len(sizes)))
    small = _pad_cols([dtf, dtb, glf, glb], LANES)
    widths = (inner, conv, gkey, gkey, gval, gval, LANES)
    dtypes = (BF16, F32, F32, F32, F32, BF16, F32)
    return jnp.concatenate([z, xbc, q, k, v, r, small], axis=1).astype(BF16), widths, dtypes


def _odd_w_in(w):
    kdim = GDN_QK_HEADS * GDN_DK
    vdim = GDN_V_HEADS * GDN_DV
    conv = 2 * kdim + vdim
    small = _pad_cols([w[:, conv + vdim:]], LANES)
    widths = (conv, vdim, LANES)
    dtypes = (F32, BF16, F32)
    return jnp.concatenate([w[:, :conv + vdim], small], axis=1).astype(BF16), widths, dtypes


def _to_column_major(h):
    t, d = h.shape
    return h.reshape(t // GRID_W, GRID_W, d).transpose(1, 0, 2).reshape(t, d)


def _from_column_major(h):
    t, d = h.shape
    return h.reshape(GRID_W, t // GRID_W, d).transpose(1, 0, 2).reshape(t, d)


def kernel(x, c, ctx, c_ctx, mod_w, mod_b, norm_mix, norm_ffn, ffn_w1, ffn_w3, ffn_w2, ev_w_in, ssd_conv_w, ssd_conv_b, ssd_dt_bias, ssd_a_log, ssd_d, ssd_norm, gla_gate_w, gla_gate_b, gla_norm, ev_w_out, od_w_in, gdn_conv_w, gdn_dt_bias, gdn_a_log, gdn_norm, od_w_out, final_norm):
    batch, seq, d = x.shape
    n_ctx = ctx.shape[1]
    assert batch == 1 and n_ctx == ROW_TILE and seq % ROW_TILE == 0 and seq % GRID_W == 0
    assert mod_w.shape[0] == 2, "one even (SSD + GLA) layer followed by one odd (gated DeltaNet) layer"
    n_ctx_tiles = n_ctx // ROW_TILE
    n_ctx_chunks = n_ctx // CHUNK

    mods = _modulation(c, c_ctx, mod_w, mod_b)
    x_ctx, x_lat = ctx[0], x[0]

    w_in, widths, dtypes = _even_w_in(ev_w_in[0])
    z, xbc, q, k, v, r, small = _norm_proj(x_ctx, x_lat, norm_mix[0], mods[0], w_in, widths, dtypes)
    xs, bm, cm, dt, da, gk_f, gk_b = _even_prep(xbc, small, ssd_conv_w[0], ssd_conv_b[0], ssd_dt_bias[0],
                                                ssd_a_log[0], gla_gate_w[0], gla_gate_b[0])
    y_f, y_b = _ssd_scan(xs, bm, cm, dt, da, n_ctx_chunks)
    o_f, o_b = _gla_scan(q, k, v, gk_f, gk_b, n_ctx_chunks)
    mixed = _even_merge(y_f, y_b, xs, z, o_f, o_b, r, ssd_d[0], ssd_norm[0], gla_norm[0], ev_w_out[0].astype(BF16))
    x_ctx, x_lat = _ffn(x_ctx, x_lat, mixed, mods[0], norm_ffn[0], ffn_w1[0].astype(BF16), ffn_w3[0].astype(BF16),
                        ffn_w2[0].astype(BF16), final_norm, False)

    w_in, widths, dtypes = _odd_w_in(od_w_in[0])
    qkv, z, small = _norm_proj(x_ctx, _to_column_major(x_lat), norm_mix[1], mods[1], w_in, widths, dtypes)
    q, k, v, g, beta = _odd_prep(qkv, small, gdn_conv_w[0], gdn_dt_bias[0], gdn_a_log[0])
    o_f, o_b = _gdn_scan(q, k, v, g, beta, n_ctx_chunks)
    mixed = _odd_merge(o_f, o_b, z, gdn_norm[0], od_w_out[0].astype(BF16), n_ctx_tiles)
    _, out = _ffn(None, x_lat, _from_column_major(mixed), mods[1], norm_ffn[1], ffn_w1[1].astype(BF16),
                  ffn_w3[1].astype(BF16), ffn_w2[1].astype(BF16), final_norm, True)
    return out[None]
```

```python
import functools

import jax
import jax.numpy as jnp
from jax import lax
from jax.experimental import pallas as pl
from jax.experimental.pallas import tpu as pltpu

F32 = jnp.float32
BF16 = jnp.bfloat16
HIGHEST = lax.Precision.HIGHEST

EPS = 1e-6
LOG2_E = 1.4426950408889634
CHUNK = 64
SCAN_CPS = 4
GLA_CPS = 2
GDN_CPS = 2
GRID_W = 64
CONV_W = 5
ROW_TILE = 256
HALO = 8
LANES = 128
VMEM_LIMIT_BYTES = 56 * 1024 * 1024

SSD_HEADS = 16
SSD_HEAD_DIM = 64
SSD_GROUPS = 2
SSD_STATE = 128
GLA_HEADS = 4
GLA_DK = 128
GLA_DV = 256
GLA_GATE_RANK = 16
GLA_GATE_NORM = 16.0
GDN_QK_HEADS = 8
GDN_V_HEADS = 16
GDN_DK = 128
GDN_DV = 128
SMALL = 16

NT_DIMS = (((1,), (1,)), ((), ()))
TN_DIMS = (((0,), (0,)), ((), ()))


def _cparams(sem):
    return pltpu.CompilerParams(dimension_semantics=sem, vmem_limit_bytes=VMEM_LIMIT_BYTES)


def _bdot(a, b):
    return jnp.dot(a.astype(BF16), b.astype(BF16), preferred_element_type=F32)


def _bdot_nt(a, b):
    return lax.dot_general(a.astype(BF16), b.astype(BF16), NT_DIMS, preferred_element_type=F32)


def _bdot_tn(a, b):
    return lax.dot_general(a.astype(BF16), b.astype(BF16), TN_DIMS, preferred_element_type=F32)


def _split3(x):
    hi = x.astype(BF16)
    r1 = x - hi.astype(F32)
    mid = r1.astype(BF16)
    lo = (r1 - mid.astype(F32)).astype(BF16)
    return hi, mid, lo


def _sel_dot_l(sel3, x):
    return jnp.dot(sel3, jnp.concatenate(_split3(x), axis=0), preferred_element_type=F32)


def _sel_dot_r(x, sel3):
    pieces = jnp.concatenate([p.astype(F32) for p in _split3(x)], axis=1)
    return jnp.dot(pieces.astype(BF16), sel3, preferred_element_type=F32)


def _silu(x):
    return x * jax.nn.sigmoid(x)


def _softplus(x):
    return jnp.maximum(x, 0.0) + jnp.log1p(jnp.exp(-jnp.abs(x)))


def _mod_kernel(c_ref, w_ref, b_ref, o_ref):
    s = _silu(c_ref[...])
    o_ref[0] = jnp.dot(s, w_ref[0], precision=HIGHEST, preferred_element_type=F32) + b_ref[0]


def _modulation(c, c_ctx, mod_w, mod_b):
    depth, d, n = mod_w.shape
    cs = jnp.zeros((8, d), F32).at[0].set(c[0]).at[1].set(c_ctx)
    tn = n // 4
    return pl.pallas_call(
        _mod_kernel,
        grid=(depth, n // tn),
        in_specs=[pl.BlockSpec((8, d), lambda i, j: (0, 0)),
                  pl.BlockSpec((1, d, tn), lambda i, j: (i, 0, j)),
                  pl.BlockSpec((1, 1, tn), lambda i, j: (i, 0, j))],
        out_specs=pl.BlockSpec((1, 8, tn), lambda i, j: (i, 0, j)),
        out_shape=jax.ShapeDtypeStruct((depth, 8, n), F32),
        compiler_params=_cparams(("arbitrary", "arbitrary")),
        name="modulation",
    )(cs, mod_w, mod_b.reshape(depth, 1, n))


def _mod_rows(mod_ref, is_ctx, idx, d):
    sl = slice(idx * d, (idx + 1) * d)
    return jnp.where(is_ctx, mod_ref[1:2, sl], mod_ref[0:1, sl])


def _norm_mod(x, nw, shift, scale):
    ms = jnp.mean(x * x, axis=-1, keepdims=True)
    return (x * lax.rsqrt(ms + EPS) * nw) * (1.0 + scale) + shift


def _ctx_lat_specs(n_ctx_tiles, d):
    ctx = pl.BlockSpec((ROW_TILE, d), lambda i: (jnp.minimum(i, n_ctx_tiles - 1), 0))
    lat = pl.BlockSpec((ROW_TILE, d), lambda i: (jnp.maximum(i - n_ctx_tiles, 0), 0))
    return ctx, lat


def _proj_kernel(xc_ref, xl_ref, nw_ref, mod_ref, w_ref, *out_refs, widths, n_ctx_tiles):
    d = xl_ref.shape[1]
    is_ctx = pl.program_id(0) < n_ctx_tiles
    x = jnp.where(is_ctx, xc_ref[...], xl_ref[...])
    h = _norm_mod(x, nw_ref[...], _mod_rows(mod_ref, is_ctx, 0, d), _mod_rows(mod_ref, is_ctx, 1, d))
    h = h.astype(BF16)
    off = 0
    for o_ref, wd in zip(out_refs, widths):
        o_ref[...] = jnp.dot(h, w_ref[:, off:off + wd], preferred_element_type=F32).astype(o_ref.dtype)
        off += wd


def _norm_proj(x_ctx, x_lat, nw, mod, w, widths, dtypes):
    d = x_lat.shape[1]
    n_ctx_tiles = x_ctx.shape[0] // ROW_TILE
    ta = x_ctx.shape[0] + x_lat.shape[0]
    ntot = sum(widths)
    kern = functools.partial(_proj_kernel, widths=widths, n_ctx_tiles=n_ctx_tiles)
    return pl.pallas_call(
        kern,
        grid=(ta // ROW_TILE,),
        in_specs=[*_ctx_lat_specs(n_ctx_tiles, d),
                  pl.BlockSpec((1, d), lambda i: (0, 0)),
                  pl.BlockSpec(mod.shape, lambda i: (0, 0)),
                  pl.BlockSpec((d, ntot), lambda i: (0, 0))],
        out_specs=[pl.BlockSpec((ROW_TILE, wd), lambda i: (i, 0)) for wd in widths],
        out_shape=[jax.ShapeDtypeStruct((ta, wd), dt) for wd, dt in zip(widths, dtypes)],
        compiler_params=_cparams(("arbitrary",)),
        name="norm_proj",
    )(x_ctx, x_lat, nw.reshape(1, d), mod, w)


def _halo_specs(width, n_tiles):
    per = ROW_TILE // HALO
    prev = pl.BlockSpec((HALO, width), lambda i: (jnp.maximum(i * per - 1, 0), 0))
    nxt = pl.BlockSpec((HALO, width), lambda i: (jnp.minimum((i + 1) * per, n_tiles * per - 1), 0))
    return prev, nxt


CONV_BIAS_ROW = CONV_W


def _conv_weights(conv_w, conv_b):
    cdim = conv_w.shape[1]
    w = jnp.zeros((8, cdim), F32).at[:CONV_W].set(conv_w)
    if conv_b is not None:
        w = w.at[CONV_BIAS_ROW].set(conv_b)
    return w.reshape(8, cdim // LANES, LANES).transpose(1, 0, 2)


def _conv_silu(cur_ref, prev_ref, next_ref, w_ref, ext_ref, out_ref, n_tiles, n_iter, slabs):
    i = pl.program_id(0)
    prev_ok = jnp.logical_and(i != 0, i != 1)
    next_ok = jnp.logical_and(i != 0, i != n_tiles - 1)
    for c in range(ext_ref.shape[0]):
        lanes = slice(c * LANES, (c + 1) * LANES)
        ext_ref[c, 0:HALO, :] = jnp.where(prev_ok, prev_ref[:, lanes], 0.0)
        ext_ref[c, HALO:HALO + ROW_TILE, :] = cur_ref[:, lanes]
        ext_ref[c, HALO + ROW_TILE:, :] = jnp.where(next_ok, next_ref[:, lanes], 0.0)
    pad = CONV_W // 2
    stride = 8
    starts = [base + r for base in range(0, ROW_TILE, 8 * stride) for r in range(stride)]

    def iteration(i, carry):
        for slab_of, post in slabs:
            c = slab_of(i)
            w = w_ref[c]
            accs = [w[CONV_BIAS_ROW:CONV_BIAS_ROW + 1, :] for _ in starts]
            for j in range(CONV_W):
                taps = [ext_ref[c, pl.ds(HALO - pad + j + s, 8, stride=stride), :] for s in starts]
                accs = [acc + tap * w[j:j + 1, :] for acc, tap in zip(accs, taps)]
            ys = post([_silu(acc) for acc in accs])
            for s, y in zip(starts, ys):
                out_ref[c, pl.ds(s, 8, stride=stride), :] = y
        return carry

    lax.fori_loop(0, n_iter, iteration, 0)


def _even_prep_kernel(xbc_ref, prev_ref, next_ref, small_ref, cw_ref, dtb_ref, aneg_ref,
                      gw_ref, gb_ref, xs_ref, bm_ref, cm_ref, dt_ref, da_ref, gkf_ref, gkb_ref,
                      ext_ref, conv_ref, *, n_tiles):
    n_slabs = ext_ref.shape[0]
    _conv_silu(xbc_ref, prev_ref, next_ref, cw_ref, ext_ref, conv_ref, n_tiles, n_slabs, [(lambda i: i, lambda ys: ys)])
    slab = 0
    for o_ref in (xs_ref, bm_ref, cm_ref):
        for c in range(o_ref.shape[1] // LANES):
            o_ref[:, c * LANES:(c + 1) * LANES] = conv_ref[slab]
            slab += 1
    small = small_ref[...]
    dt = _softplus(small + dtb_ref[...])
    dt_ref[...] = dt
    da_ref[...] = dt * aneg_ref[...]
    s_hi = small.astype(BF16)
    s_lo = (small - s_hi.astype(F32)).astype(BF16)
    lhs = jnp.concatenate([s_hi, s_lo, s_hi], axis=1)
    for d, o_ref in enumerate((gkf_ref, gkb_ref)):
        pre = jnp.dot(lhs, gw_ref[d], preferred_element_type=F32) + gb_ref[d]
        o_ref[...] = -_softplus(-pre) * (1.0 / GLA_GATE_NORM)


def _even_prep(xbc, small, conv_w, conv_b, dt_bias, a_log, gate_w, gate_b):
    ta, cdim = xbc.shape
    n_tiles = ta // ROW_TILE
    inner = SSD_HEADS * SSD_HEAD_DIM
    gs = SSD_GROUPS * SSD_STATE
    gkey = GLA_HEADS * GLA_DK
    cw = _conv_weights(conv_w, conv_b)
    dtb = jnp.zeros((1, LANES), F32).at[0, :2 * SMALL].set(dt_bias.reshape(-1))
    aneg = jnp.zeros((1, LANES), F32).at[0, :2 * SMALL].set(-jnp.exp(a_log.reshape(-1)))
    gw = jnp.zeros((2, LANES, gkey), F32)
    gw = gw.at[0, 2 * SMALL:3 * SMALL].set(gate_w[0]).at[1, 3 * SMALL:4 * SMALL].set(gate_w[1])
    gw_hi = gw.astype(BF16)
    gw_lo = (gw - gw_hi.astype(F32)).astype(BF16)
    gw = jnp.concatenate([gw_hi, gw_hi, gw_lo], axis=1)
    prev_spec, next_spec = _halo_specs(cdim, n_tiles)
    row = lambda w: pl.BlockSpec((ROW_TILE, w), lambda i: (i, 0))
    full = lambda a: pl.BlockSpec(a.shape, lambda i: (0,) * a.ndim)
    gb = gate_b.reshape(2, 1, gkey)
    return pl.pallas_call(
        functools.partial(_even_prep_kernel, n_tiles=n_tiles),
        grid=(n_tiles,),
        in_specs=[row(cdim), prev_spec, next_spec, row(LANES), full(cw), full(dtb), full(aneg), full(gw), full(gb)],
        out_specs=[row(inner), row(gs), row(gs), row(LANES), row(LANES), row(gkey), row(gkey)],
        out_shape=[jax.ShapeDtypeStruct((ta, w), F32) for w in (inner, gs, gs, LANES, LANES, gkey, gkey)],
        scratch_shapes=[pltpu.VMEM((cdim // LANES, ROW_TILE + 2 * HALO, LANES), F32),
                        pltpu.VMEM((cdim // LANES, ROW_TILE, LANES), F32)],
        compiler_params=_cparams(("arbitrary",)),
        name="even_prep",
    )(xbc, xbc, xbc, small, cw, dtb, aneg, gw, gb)


def _scan_consts(rev, heads, lanes_per_head):
    r = jnp.arange(CHUNK)
    tri = (r[:, None] <= r[None, :]) if rev else (r[:, None] >= r[None, :])
    strict = (r[:, None] < r[None, :]) if rev else (r[:, None] > r[None, :])
    eye = r[:, None] == r[None, :]
    expand = (jnp.arange(heads)[:, None] == (jnp.arange(heads * lanes_per_head)[None, :] // lanes_per_head))
    return dict(
        tri3=jnp.tile(tri, (1, 3)).astype(BF16),
        expand3=jnp.tile(expand, (3, 1)).astype(BF16),
        tri_t=jnp.tile(tri, (1, heads)).astype(F32),
        strict_t=jnp.tile(strict, (1, heads)).astype(F32),
        eye_t=jnp.tile(eye, (1, heads)).astype(F32),
    )


def _pair_mask():
    r = jnp.arange(2 * CHUNK)
    return ((r[:, None] // CHUNK) == (r[None, :] // CHUNK)).astype(F32)


def _level_masks():
    r = jnp.arange(2 * CHUNK)
    same = lambda b: (r[:, None] // b) == (r[None, :] // b)
    sizes = [2 ** m for m in range(CHUNK.bit_length() - 1)]
    return jnp.stack([jnp.logical_and(same(2 * b), jnp.logical_not(same(b))) for b in sizes]).astype(F32)


def _chunk_map(rev, n_chunks, n_ctx_chunks):
    if not rev:
        return lambda i: i
    return lambda i: jnp.where(i < n_ctx_chunks, n_ctx_chunks - 1 - i, n_chunks + n_ctx_chunks - 1 - i)


def _decay_tables(tri3s, expand3, eye_t, tri_ts, gs):
    cums = [_sel_dot_l(tri3, g) * LOG2_E for tri3, g in zip(tri3s, gs)]
    cols = [_sel_dot_r(cum, expand3) for cum in cums]
    rowfs = [jnp.sum(col * eye_t, axis=0, keepdims=True) for col in cols]
    decays = [jnp.where(tri_t > 0.5, jnp.exp2(col - rowf), 0.0) for tri_t, col, rowf in zip(tri_ts, cols, rowfs)]
    return cols, decays


SSD_SCAN_INS = 5


def _ssd_kernel(*refs, cps):
    ins = refs[:2 * SSD_SCAN_INS]
    tri_ref, expand_ref, eye_ref, trit_ref, pair_ref, yf_ref, yb_ref, st_ref = refs[2 * SSD_SCAN_INS:]

    @pl.when(pl.program_id(0) == 0)
    def _():
        st_ref[...] = jnp.zeros(st_ref.shape, F32)

    pair = pair_ref[...]
    heads_g = SSD_HEADS // SSD_GROUPS
    gw = heads_g * SSD_HEAD_DIM
    pieces = [(d, c, g) for d in range(2) for c in range(cps) for g in range(SSD_GROUPS)]
    gsl = lambda g: slice(g * SSD_STATE, (g + 1) * SSD_STATE)
    lanes = lambda g: slice(g * gw, (g + 1) * gw)
    rows = lambda d, c: slice((cps - 1 - c if d else c) * CHUNK, (cps - c if d else c + 1) * CHUNK)
    chunks = [(d, c) for d in range(2) for c in range(cps)]
    small = lambda d, c, k: ins[SSD_SCAN_INS * d + k][rows(d, c), d * SMALL:(d + 1) * SMALL]
    cols, decays = _decay_tables([tri_ref[d] for d, _ in chunks], expand_ref[...], eye_ref[...],
                                 [trit_ref[d] for d, _ in chunks], [small(d, c, 4) for d, c in chunks])
    dtes = [_sel_dot_r(small(d, c, 3), expand_ref[...]) for d, c in chunks]
    tabs = {}
    for (d, c), col, decay, dte in zip(chunks, cols, decays, dtes):
        last_row = 0 if d else CHUNK - 1
        xdt = ins[SSD_SCAN_INS * d][rows(d, c), :] * dte
        last = col[last_row:last_row + 1, :]
        tabs[d, c] = dict(decay=decay, xdt=xdt, xw=xdt * jnp.exp2(last - col), chunk_decay=jnp.exp2(last),
                          eac=jnp.exp2(col))
    bgs = {(d, c, g): ins[SSD_SCAN_INS * d + 1][rows(d, c), gsl(g)].astype(BF16) for d, c, g in pieces}
    cgs = {(d, c, g): ins[SSD_SCAN_INS * d + 2][rows(d, c), gsl(g)].astype(BF16) for d, c, g in pieces}
    wms = {p: lax.dot_general(cgs[p], jnp.concatenate([bgs[p]] * heads_g, axis=0), NT_DIMS,
                              preferred_element_type=F32) * tabs[p[:2]]["decay"][:, lanes(p[2])] for p in pieces}
    news = {(d, c, g): lax.dot_general(bgs[d, c, g], tabs[d, c]["xw"][:, lanes(g)].astype(BF16), TN_DIMS,
                                       preferred_element_type=F32) for d, c, g in pieces}
    diags = {}
    for d, c, g in pieces:
        xg = tabs[d, c]["xdt"][:, lanes(g)]
        for j in range(gw // LANES):
            xp = xg[:, j * LANES:(j + 1) * LANES]
            blockdiag = jnp.concatenate([xp, xp], axis=0) * pair
            diags[d, c, g, j] = _bdot(wms[d, c, g][:, j * LANES:(j + 1) * LANES], blockdiag)
    states = {}
    for d in range(2):
        for g in range(SSD_GROUPS):
            state = st_ref[d, g]
            for c in range(cps):
                states[d, c, g] = state
                state = state * tabs[d, c]["chunk_decay"][:, lanes(g)] + news[d, c, g]
            st_ref[d, g] = state
    offs = {p: jnp.dot(cgs[p], states[p].astype(BF16), preferred_element_type=F32) for p in pieces}
    for d, y_ref in enumerate((yf_ref, yb_ref)):
        for c in range(cps):
            for g in range(SSD_GROUPS):
                diag = jnp.concatenate([diags[d, c, g, j] for j in range(gw // LANES)], axis=1)
                y_ref[rows(d, c), lanes(g)] = (diag + offs[d, c, g] * tabs[d, c]["eac"][:, lanes(g)]).astype(y_ref.dtype)


def _ssd_scan(xs, bm, cm, dt, da, n_ctx_chunks):
    ta, inner = xs.shape
    cps = SCAN_CPS
    nb = ta // (cps * CHUNK)
    fwd = _scan_consts(False, SSD_HEADS, SSD_HEAD_DIM)
    bwd = _scan_consts(True, SSD_HEADS, SSD_HEAD_DIM)
    both = lambda name: jnp.stack([fwd[name], bwd[name]])
    consts = [both("tri3"), fwd["expand3"], fwd["eye_t"], both("tri_t"), _pair_mask()]
    full = lambda a: pl.BlockSpec(a.shape, lambda i: (0,) * a.ndim)
    gs = SSD_GROUPS * SSD_STATE

    def row(rev, w):
        cmap = _chunk_map(rev, nb, n_ctx_chunks // cps)
        return pl.BlockSpec((cps * CHUNK, w), lambda i: (cmap(i), 0))

    dir_specs = lambda rev: [row(rev, inner), row(rev, gs), row(rev, gs), row(rev, LANES), row(rev, LANES)]
    return pl.pallas_call(
        functools.partial(_ssd_kernel, cps=cps),
        grid=(nb,),
        in_specs=dir_specs(False) + dir_specs(True) + [full(a) for a in consts],
        out_specs=[row(False, inner), row(True, inner)],
        out_shape=[jax.ShapeDtypeStruct((ta, inner), BF16)] * 2,
        scratch_shapes=[pltpu.VMEM((2, SSD_GROUPS, SSD_STATE, inner // SSD_GROUPS), F32)],
        compiler_params=_cparams(("arbitrary",)),
        name="ssd_scan",
    )(xs, bm, cm, dt, da, xs, bm, cm, dt, da, *consts)


GLA_SUB = 8
GLA_LEVELS = (8, 16, 32)
GLA_SCAN_INS = 4


def _gla_masks():
    r = jnp.arange(CHUNK)
    same = lambda b: (r[:, None] // b) == (r[None, :] // b)
    out = []
    for rev in (False, True):
        strict = (r[:, None] < r[None, :]) if rev else (r[:, None] > r[None, :])
        lv = [same(2 * b) & ~same(b) & strict for b in GLA_LEVELS]
        within = r % GLA_SUB
        dg = []
        for s in range(GLA_SUB):
            seen = (within <= s) if rev else (within >= s)
            dg.append((r[None, :] == (r[:, None] // GLA_SUB) * GLA_SUB + s) & seen[:, None])
        out.append(jnp.stack(lv + dg))
    return jnp.stack(out).astype(F32)


def _block_rows(a, rows, b):
    return jnp.concatenate([jnp.broadcast_to(a[r:r + 1, :], (b, a.shape[1])) for r in rows], axis=0)


def _gla_kernel(*refs, cps):
    ins = refs[:2 * GLA_SCAN_INS]
    tri_ref, mask_ref, of_ref, ob_ref, st_ref = refs[2 * GLA_SCAN_INS:]

    @pl.when(pl.program_id(0) == 0)
    def _():
        st_ref[...] = jnp.zeros(st_ref.shape, F32)

    pieces = [(d, c, h) for d in range(2) for c in range(cps) for h in range(GLA_HEADS)]
    ksl = lambda h: slice(h * GLA_DK, (h + 1) * GLA_DK)
    vsl = lambda h: slice(h * GLA_DV, (h + 1) * GLA_DV)
    rws = lambda d, c: slice((cps - 1 - c if d else c) * CHUNK, (cps - c if d else c + 1) * CHUNK)
    q_of = lambda d, c, h: ins[GLA_SCAN_INS * d][rws(d, c), ksl(h)] * (GLA_DK ** -0.5)
    k_of = lambda d, c, h: ins[GLA_SCAN_INS * d + 1][rws(d, c), ksl(h)]
    v_of = lambda d, c, h: ins[GLA_SCAN_INS * d + 2][rws(d, c), vsl(h)]
    within = lax.broadcasted_iota(jnp.int32, (CHUNK, 1), 0) % GLA_SUB
    gcs = {(d, c, h): _sel_dot_l(tri_ref[d], ins[GLA_SCAN_INS * d + 3][rws(d, c), ksl(h)]) * LOG2_E
           for d, c, h in pieces}
    news = {}
    for d, c, h in pieces:
        gc = gcs[d, c, h]
        last = 0 if d else CHUNK - 1
        glast = gc[last:last + 1, :]
        news[d, c, h] = (jnp.exp2(glast), _bdot_tn(v_of(d, c, h), k_of(d, c, h) * jnp.exp2(glast - gc)))
    states = {}
    for d in range(2):
        for h in range(GLA_HEADS):
            state = st_ref[d, h]
            for c in range(cps):
                states[d, c, h] = state
                state = state * news[d, c, h][0] + news[d, c, h][1]
            st_ref[d, h] = state
    inters = {p: _bdot_nt(q_of(*p) * jnp.exp2(gcs[p]), states[p]) for p in pieces}
    atts = {}
    for d, c, h in pieces:
        gc, qh, kh = gcs[d, c, h], q_of(d, c, h), k_of(d, c, h)
        att = None
        for li, b in enumerate(GLA_LEVELS):
            nb = CHUNK // b
            if d:
                rq = [min(b * (i + 1), CHUNK - 1) for i in range(nb)]
                rk = [b * i for i in range(nb)]
            else:
                rq = [max(b * i - 1, 0) for i in range(nb)]
                rk = [b * (i + 1) - 1 for i in range(nb)]
            qs = qh * jnp.exp2(gc - _block_rows(gc, rq, b))
            ks = kh * jnp.exp2(_block_rows(gc, rk, b) - gc)
            part = _bdot_nt(qs, ks) * mask_ref[d, li]
            att = part if att is None else att + part
        for s in range(GLA_SUB):
            rows = [GLA_SUB * i + s for i in range(CHUNK // GLA_SUB)]
            seen = (within <= s) if d else (within >= s)
            e = jnp.where(seen, jnp.exp2(gc - _block_rows(gc, rows, GLA_SUB)), 0.0)
            a = jnp.sum(qh * _block_rows(kh, rows, GLA_SUB) * e, axis=-1, keepdims=True)
            att = att + a * mask_ref[d, len(GLA_LEVELS) + s]
        atts[d, c, h] = att
    for d, o_ref in enumerate((of_ref, ob_ref)):
        for c in range(cps):
            for h in range(GLA_HEADS):
                o_ref[rws(d, c), vsl(h)] = (inters[d, c, h] + _bdot(atts[d, c, h], v_of(d, c, h))).astype(o_ref.dtype)


def _gla_scan(q, k, v, gk_f, gk_b, n_ctx_chunks):
    ta, kdim = q.shape
    vdim = v.shape[1]
    cps = GLA_CPS
    nb = ta // (cps * CHUNK)
    tri3 = jnp.stack([_scan_consts(rev, 1, 1)["tri3"] for rev in (False, True)])
    masks = _gla_masks()
    full = lambda a: pl.BlockSpec(a.shape, lambda i: (0,) * a.ndim)

    def row(rev, w):
        cmap = _chunk_map(rev, nb, n_ctx_chunks // cps)
        return pl.BlockSpec((cps * CHUNK, w), lambda i: (cmap(i), 0))

    dir_specs = lambda rev: [row(rev, kdim), row(rev, kdim), row(rev, vdim), row(rev, kdim)]
    return pl.pallas_call(
        functools.partial(_gla_kernel, cps=cps),
        grid=(nb,),
        in_specs=dir_specs(False) + dir_specs(True) + [full(tri3), full(masks)],
        out_specs=[row(False, vdim), row(True, vdim)],
        out_shape=[jax.ShapeDtypeStruct((ta, vdim), BF16)] * 2,
        scratch_shapes=[pltpu.VMEM((2, GLA_HEADS, GLA_DV, GLA_DK), F32)],
        compiler_params=_cparams(("arbitrary",)),
        name="gla_scan",
    )(q, k, v, gk_f, q, k, v, gk_b, tri3, masks)


def _group_rms(y, w, width):
    parts = []
    for g in range(y.shape[1] // width):
        yg = y[:, g * width:(g + 1) * width]
        ms = jnp.mean(yg * yg, axis=-1, keepdims=True)
        parts.append(yg * lax.rsqrt(ms + EPS))
    return jnp.concatenate(parts, axis=1) * w


def _even_merge_kernel(yf_ref, yb_ref, xs_ref, z_ref, of_ref, ob_ref, r_ref, dsk_ref, snw_ref, gnw_ref, w_ref, o_ref):
    inner = xs_ref.shape[1]
    f32 = lambda ref: ref[...].astype(F32)
    y = (f32(yf_ref) + f32(yb_ref) + dsk_ref[...] * xs_ref[...]) * _silu(f32(z_ref))
    y = _group_rms(y, snw_ref[...], inner // SSD_GROUPS)
    o = _group_rms(f32(of_ref) + f32(ob_ref), gnw_ref[...], GLA_DV) * _silu(f32(r_ref))
    o_ref[...] = _bdot(y, w_ref[:inner, :]) + _bdot(o, w_ref[inner:, :])


def _even_merge(yf, yb, xs, z, of, ob, r, d_skip, ssd_nw, gla_nw, w_out):
    ta, inner = xs.shape
    vdim = of.shape[1]
    d = w_out.shape[1]
    dsk = jnp.repeat(d_skip, SSD_HEAD_DIM).reshape(1, inner)
    gnw = jnp.tile(gla_nw, GLA_HEADS).reshape(1, vdim)
    row = lambda w: pl.BlockSpec((ROW_TILE, w), lambda i: (i, 0))
    vec = lambda w: pl.BlockSpec((1, w), lambda i: (0, 0))
    return pl.pallas_call(
        _even_merge_kernel,
        grid=(ta // ROW_TILE,),
        in_specs=[row(inner)] * 4 + [row(vdim)] * 3 + [vec(inner), vec(inner), vec(vdim),
                                                        pl.BlockSpec(w_out.shape, lambda i: (0, 0))],
        out_specs=row(d),
        out_shape=jax.ShapeDtypeStruct((ta, d), F32),
        compiler_params=_cparams(("arbitrary",)),
        name="even_merge",
    )(yf, yb, xs, z, of, ob, r, dsk, ssd_nw.reshape(1, inner), gnw, w_out)


def _odd_prep_kernel(qkv_ref, prev_ref, next_ref, small_ref, cw_ref, dtb_ref, aneg_ref,
                     q_ref, k_ref, v_ref, g_ref, beta_ref, ext_ref, conv_ref, *, n_tiles):
    assert GDN_DK == LANES
    kslabs = q_ref.shape[1] // LANES
    n_slabs = ext_ref.shape[0]
    def l2n(scale):
        def post(groups):
            sums = [jnp.sum(a * a, axis=-1, keepdims=True) for a in groups]
            return [a * (lax.rsqrt(s + EPS) * scale) for a, s in zip(groups, sums)]
        return post

    v_per_iter = (n_slabs - 2 * kslabs) // kslabs
    v_slabs = [(lambda i, m=m: 2 * kslabs + v_per_iter * i + m, lambda ys: ys) for m in range(v_per_iter)]
    _conv_silu(qkv_ref, prev_ref, next_ref, cw_ref, ext_ref, conv_ref, n_tiles, kslabs,
               [(lambda i: i, l2n(GDN_DK ** -0.5)), (lambda i: kslabs + i, l2n(1.0))] + v_slabs)
    slab = 0
    for o_ref in (q_ref, k_ref, v_ref):
        for c in range(o_ref.shape[1] // LANES):
            o_ref[:, c * LANES:(c + 1) * LANES] = conv_ref[slab]
            slab += 1
    small = small_ref[...]
    g_ref[...] = _softplus(small + dtb_ref[...]) * aneg_ref[...]
    beta_ref[...] = jax.nn.sigmoid(small)


def _odd_prep(qkv, small, conv_w, dt_bias, a_log):
    ta, cdim = qkv.shape
    n_tiles = ta // ROW_TILE
    kdim = GDN_QK_HEADS * GDN_DK
    vdim = GDN_V_HEADS * GDN_DV
    cw = _conv_weights(conv_w, None)
    dtb = jnp.zeros((1, LANES), F32).at[0, :2 * SMALL].set(dt_bias.reshape(-1))
    aneg = jnp.zeros((1, LANES), F32).at[0, :2 * SMALL].set(-jnp.exp(a_log.reshape(-1)))
    prev_spec, next_spec = _halo_specs(cdim, n_tiles)
    row = lambda w: pl.BlockSpec((ROW_TILE, w), lambda i: (i, 0))
    full = lambda a: pl.BlockSpec(a.shape, lambda i: (0,) * a.ndim)
    return pl.pallas_call(
        functools.partial(_odd_prep_kernel, n_tiles=n_tiles),
        grid=(n_tiles,),
        in_specs=[row(cdim), prev_spec, next_spec, row(LANES), full(cw), full(dtb), full(aneg)],
        out_specs=[row(kdim), row(kdim), row(vdim), row(LANES), row(LANES)],
        out_shape=[jax.ShapeDtypeStruct((ta, w), F32) for w in (kdim, kdim, vdim, LANES, LANES)],
        scratch_shapes=[pltpu.VMEM((cdim // LANES, ROW_TILE + 2 * HALO, LANES), F32),
                        pltpu.VMEM((cdim // LANES, ROW_TILE, LANES), F32)],
        compiler_params=_cparams(("arbitrary",)),
        name="odd_prep",
    )(qkv, qkv, qkv, small, cw, dtb, aneg)


GDN_SCAN_INS = 5


def _interleave(phases):
    phases = list(phases)
    while phases:
        for ph in list(phases):
            if next(ph, StopIteration) is StopIteration:
                phases.remove(ph)


def _gdn_kernel(*refs, cps):
    ins = refs[:2 * GDN_SCAN_INS]
    (tri_ref, expand_ref, eye_ref, trit_ref, strict_ref, pair_ref, lvl_ref, eye2_ref,
     of_ref, ob_ref, st_ref) = refs[2 * GDN_SCAN_INS:]

    @pl.when(pl.program_id(0) == 0)
    def _():
        st_ref[...] = jnp.zeros(st_ref.shape, F32)

    pair = pair_ref[...]
    eye2 = eye2_ref[...]
    left = lax.broadcasted_iota(jnp.int32, (CHUNK, 2 * CHUNK), 1) < CHUNK
    n_levels = lvl_ref.shape[0]
    dirs = range(2)
    o_refs = (of_ref, ob_ref)
    dot16 = lambda x, y: jnp.dot(x, y, preferred_element_type=F32)
    rows = lambda d, c: slice((cps - 1 - c if d else c) * CHUNK, (cps - c if d else c + 1) * CHUNK)
    pairs = [(d, j) for d in dirs for j in range(GDN_QK_HEADS)]
    chains = [(d, h) for d in dirs for h in range(GDN_V_HEADS)]

    keys = [(c, d) for c in range(cps) for d in dirs]
    cols, decays = _decay_tables([tri_ref[d] for _, d in keys], expand_ref[...], eye_ref[...],
                                 [trit_ref[d] for _, d in keys],
                                 [ins[GDN_SCAN_INS * d + 3][rows(d, c), d * SMALL:(d + 1) * SMALL] for c, d in keys])
    beta_ts = [_sel_dot_r(ins[GDN_SCAN_INS * d + 4][rows(d, c), (2 + d) * SMALL:(3 + d) * SMALL], expand_ref[...])
               for c, d in keys]
    tabs = {k: t for k, t in zip(keys, zip(cols, decays, beta_ts))}

    def build(c, out):
        for d in dirs:
            q_ref, k_ref, v_ref = ins[GDN_SCAN_INS * d:GDN_SCAN_INS * d + 3]
            rw = rows(d, c)
            last_row = 0 if d else CHUNK - 1
            col, decay, beta_t = tabs[c, d]
            a_all = beta_t * decay * strict_ref[d]
            for j in range(GDN_QK_HEADS):
                ksl = slice(j * GDN_DK, (j + 1) * GDN_DK)
                psl = slice(j * 2 * CHUNK, (j + 1) * 2 * CHUNK)
                kj = k_ref[rw, ksl]
                kb = kj.astype(BF16)
                krep = jnp.concatenate([kb, kb], axis=0)
                kk = lax.dot_general(kb, krep, NT_DIMS, preferred_element_type=F32)
                qk = lax.dot_general(q_ref[rw, ksl].astype(BF16), krep, NT_DIMS, preferred_element_type=F32)
                out["att"][d, j] = (qk * decay[:, psl]).astype(BF16)
                a_pair = kk * a_all[:, psl]
                a_bd = jnp.concatenate([a_pair, a_pair], axis=0) * pair
                out["tinv"][d, j] = (eye2 - a_bd * lvl_ref[0]).astype(BF16)
                out["a16"][d, j] = a_bd.astype(BF16)
                colp, betap = col[:, psl], beta_t[:, psl]
                colr, betar = pltpu.roll(colp, CHUNK, 1), pltpu.roll(betap, CHUNK, 1)
                rhs = []
                for hh in range(2):
                    h = 2 * j + hh
                    hsl = slice(h * GDN_DV, (h + 1) * GDN_DV)
                    cumx = jnp.where(left, colp, colr) if hh == 0 else jnp.where(left, colr, colp)
                    betax = jnp.where(left, betap, betar) if hh == 0 else jnp.where(left, betar, betap)
                    lastx = cumx[last_row:last_row + 1, :]
                    egx = jnp.exp2(cumx)
                    out["qd"][d, h] = (q_ref[rw, ksl] * egx).astype(BF16)
                    out["kd"][d, h] = (kj * jnp.exp2(lastx - cumx)).astype(BF16)
                    out["gl"][d, h] = jnp.exp2(lastx)
                    rhs.append(jnp.concatenate([v_ref[rw, hsl] * betax, kj * betax * egx], axis=1))
                out["rhs"][d, j] = jnp.concatenate(rhs, axis=0).astype(BF16)
                yield

    def invert(out):
        tinv, a16 = out["tinv"], out["a16"]
        for lv in range(1, n_levels):
            b = 2 ** lv
            mask = lvl_ref[lv].astype(BF16)
            if b < 16:
                tls = {p: dot16(tinv[p], a16[p] * mask).astype(BF16) for p in pairs}
                yield
                for p in pairs:
                    tinv[p] = tinv[p] - dot16(tls[p], tinv[p]).astype(BF16)
                yield
                continue
            blocks = lambda t: [t[i * b:(i + 1) * b] for i in range(t.shape[0] // b)]
            moving = lambda d, t: jnp.concatenate(blocks(t)[(1 - d)::2], axis=0)
            tls = {p: dot16(moving(p[0], tinv[p]), a16[p] * mask).astype(BF16) for p in pairs}
            yield
            for p in pairs:
                new = blocks(moving(p[0], tinv[p]) - dot16(tls[p], tinv[p]).astype(BF16))
                tinv[p] = jnp.concatenate([new[i // 2] if i % 2 == 1 - p[0] else blk
                                           for i, blk in enumerate(blocks(tinv[p]))], axis=0)
            yield
        out["x"] = {p: dot16(tinv[p], out["rhs"][p]) for p in pairs}
        yield

    def recur(c, out, states):
        x_of = lambda d, h: out["x"][d, h // 2][(h % 2) * CHUNK:(h % 2 + 1) * CHUNK]
        ws_qs = {(d, h): dot16(jnp.concatenate([x_of(d, h)[:, GDN_DV:].astype(BF16), out["qd"][d, h]], axis=0),
                               states[d, h].astype(BF16)) for d, h in chains}
        yield
        v_new = {(d, h): (x_of(d, h)[:, :GDN_DV] - ws_qs[d, h][:CHUNK]).astype(BF16) for d, h in chains}
        for d, h in chains:
            att = out["att"][d, h // 2][:, (h % 2) * CHUNK:(h % 2 + 1) * CHUNK]
            o_refs[d][rows(d, c), h * GDN_DV:(h + 1) * GDN_DV] = (
                ws_qs[d, h][CHUNK:] + dot16(att, v_new[d, h])).astype(o_refs[d].dtype)
        yield
        for d, h in chains:
            states[d, h] = states[d, h] * out["gl"][d, h] + lax.dot_general(
                out["kd"][d, h], v_new[d, h], TN_DIMS, preferred_element_type=F32)
        yield

    fields = ("att", "tinv", "a16", "rhs", "qd", "kd", "gl")
    outs = [{f: {} for f in fields} for _ in range(cps)]
    states = {(d, h): st_ref[d, h] for d, h in chains}
    _interleave([build(0, outs[0])])
    for c in range(cps):
        phases = [invert(outs[c])]
        if c + 1 < cps:
            phases.append(build(c + 1, outs[c + 1]))
        if c > 0:
            phases.append(recur(c - 1, outs[c - 1], states))
        _interleave(phases)
    _interleave([recur(cps - 1, outs[cps - 1], states)])
    for d, h in chains:
        st_ref[d, h] = states[d, h]


def _gdn_scan(q, k, v, g, beta, n_ctx_chunks):
    ta, kdim = q.shape
    vdim = v.shape[1]
    cps = GDN_CPS
    nb = ta // (cps * CHUNK)
    fwd = _scan_consts(False, GDN_V_HEADS, CHUNK)
    bwd = _scan_consts(True, GDN_V_HEADS, CHUNK)
    both = lambda name: jnp.stack([fwd[name], bwd[name]])
    consts = [both("tri3"), fwd["expand3"], fwd["eye_t"], both("tri_t"), both("strict_t"),
              _pair_mask(), _level_masks(), jnp.eye(2 * CHUNK, dtype=F32)]
    full = lambda a: pl.BlockSpec(a.shape, lambda i: (0,) * a.ndim)

    def row(rev, w):
        cmap = _chunk_map(rev, nb, n_ctx_chunks // cps)
        return pl.BlockSpec((cps * CHUNK, w), lambda i: (cmap(i), 0))

    dir_specs = lambda rev: [row(rev, kdim), row(rev, kdim), row(rev, vdim), row(rev, LANES), row(rev, LANES)]
    return pl.pallas_call(
        functools.partial(_gdn_kernel, cps=cps),
        grid=(nb,),
        in_specs=dir_specs(False) + dir_specs(True) + [full(a) for a in consts],
        out_specs=[row(False, vdim), row(True, vdim)],
        out_shape=[jax.ShapeDtypeStruct((ta, vdim), BF16)] * 2,
        scratch_shapes=[pltpu.VMEM((2, GDN_V_HEADS, GDN_DK, GDN_DV), F32)],
        compiler_params=_cparams(("arbitrary",)),
        name="gdn_scan",
    )(q, k, v, g, beta, q, k, v, g, beta, *consts)


def _odd_merge_kernel(of_ref, ob_ref, z_ref, nw_ref, w_ref, o_ref):
    f32 = lambda ref: ref[...].astype(F32)
    o = _group_rms(f32(of_ref) + f32(ob_ref), nw_ref[...], GDN_DV) * _silu(f32(z_ref))
    o_ref[...] = _bdot(o, w_ref[...])


def _odd_merge(of, ob, z, gdn_nw, w_out, n_ctx_tiles):
    ta, vdim = of.shape
    d = w_out.shape[1]
    n_lat = ta - n_ctx_tiles * ROW_TILE
    nw = jnp.tile(gdn_nw, GDN_V_HEADS).reshape(1, vdim)
    row_in = pl.BlockSpec((ROW_TILE, vdim), lambda i: (i + n_ctx_tiles, 0))
    return pl.pallas_call(
        _odd_merge_kernel,
        grid=(n_lat // ROW_TILE,),
        in_specs=[row_in] * 3 + [pl.BlockSpec((1, vdim), lambda i: (0, 0)), pl.BlockSpec(w_out.shape, lambda i: (0, 0))],
        out_specs=pl.BlockSpec((ROW_TILE, d), lambda i: (i, 0)),
        out_shape=jax.ShapeDtypeStruct((n_lat, d), F32),
        compiler_params=_cparams(("arbitrary",)),
        name="odd_merge",
    )(of, ob, z, nw, w_out)


def _ffn_kernel(*refs, n_ctx_tiles, final):
    if n_ctx_tiles:
        xc_ref, xl_ref, y_ref, mod_ref, nw_ref, w1_ref, w3_ref, w2_ref, fw_ref, oc_ref, ol_ref = refs
    else:
        xl_ref, y_ref, mod_ref, nw_ref, w1_ref, w3_ref, w2_ref, fw_ref, ol_ref = refs
    d = xl_ref.shape[1]
    is_ctx = pl.program_id(0) < n_ctx_tiles
    x = jnp.where(is_ctx, xc_ref[...], xl_ref[...]) if n_ctx_tiles else xl_ref[...]
    x = x + _mod_rows(mod_ref, is_ctx, 2, d) * y_ref[...]
    h = _norm_mod(x, nw_ref[...], _mod_rows(mod_ref, is_ctx, 3, d), _mod_rows(mod_ref, is_ctx, 4, d)).astype(BF16)
    a = jnp.dot(h, w1_ref[...], preferred_element_type=F32)
    b = jnp.dot(h, w3_ref[...], preferred_element_type=F32)
    out = x + _mod_rows(mod_ref, is_ctx, 5, d) * _bdot(_silu(a) * b, w2_ref[...])
    if final:
        ms = jnp.mean(out * out, axis=-1, keepdims=True)
        out = out * lax.rsqrt(ms + EPS) * fw_ref[...]
    if n_ctx_tiles:
        @pl.when(is_ctx)
        def _():
            oc_ref[...] = out

        @pl.when(jnp.logical_not(is_ctx))
        def _():
            ol_ref[...] = out
    else:
        ol_ref[...] = out


def _ffn(x_ctx, x_lat, y_all, mod, nw, w1, w3, w2, final_w, final):
    d = x_lat.shape[1]
    n_ctx_tiles = 0 if x_ctx is None else x_ctx.shape[0] // ROW_TILE
    n_tiles = n_ctx_tiles + x_lat.shape[0] // ROW_TILE
    row = pl.BlockSpec((ROW_TILE, d), lambda i: (i, 0))
    full = lambda a: pl.BlockSpec(a.shape, lambda i: (0,) * a.ndim)
    nw = nw.reshape(1, d)
    fw = final_w.reshape(1, d)
    params = [mod, nw, w1, w3, w2, fw]
    if n_ctx_tiles:
        x_specs, x_args = list(_ctx_lat_specs(n_ctx_tiles, d)), [x_ctx, x_lat]
        out_specs = list(_ctx_lat_specs(n_ctx_tiles, d))
        out_shape = [jax.ShapeDtypeStruct(x_ctx.shape, F32), jax.ShapeDtypeStruct(x_lat.shape, F32)]
    else:
        x_specs, x_args = [row], [x_lat]
        out_specs, out_shape = [row], [jax.ShapeDtypeStruct(x_lat.shape, F32)]
    outs = pl.pallas_call(
        functools.partial(_ffn_kernel, n_ctx_tiles=n_ctx_tiles, final=final),
        grid=(n_tiles,),
        in_specs=x_specs + [row] + [full(a) for a in params],
        out_specs=out_specs,
        out_shape=out_shape,
        compiler_params=_cparams(("arbitrary",)),
        name="ffn_final" if final else "ffn",
    )(*x_args, y_all, *params)
    return (outs[0], outs[1]) if n_ctx_tiles else (None, outs[0])


def _pad_cols(parts, width):
    w = jnp.concatenate(parts, axis=1)
    return jnp.pad(w, ((0, 0), (0, width - w.shape[1])))


def _even_w_in(w):
    inner = SSD_HEADS * SSD_HEAD_DIM
    conv = inner + 2 * SSD_GROUPS * SSD_STATE
    gkey = GLA_HEADS * GLA_DK
    gval = GLA_HEADS * GLA_DV
    sizes = (inner, conv, SMALL, SMALL, gkey, gkey, gval, gval, SMALL, SMALL)
    offs = [0]
    for s in sizes:
        offs.append(offs[-1] + s)
    z, xbc, dtf, dtb, q, k, v, r, glf, glb = (w[:, offs[i]:offs[i + 1]] for i in range(len(sizes)))
    small = _pad_cols([dtf, dtb, glf, glb], LANES)
    widths = (inner, conv, gkey, gkey, gval, gval, LANES)
    dtypes = (BF16, F32, F32, F32, F32, BF16, F32)
    return jnp.concatenate([z, xbc, q, k, v, r, small], axis=1).astype(BF16), widths, dtypes


def _odd_w_in(w):
    kdim = GDN_QK_HEADS * GDN_DK
    vdim = GDN_V_HEADS * GDN_DV
    conv = 2 * kdim + vdim
    small = _pad_cols([w[:, conv + vdim:]], LANES)
    widths = (conv, vdim, LANES)
    dtypes = (F32, BF16, F32)
    return jnp.concatenate([w[:, :conv + vdim], small], axis=1).astype(BF16), widths, dtypes


def _to_column_major(h):
    t, d = h.shape
    return h.reshape(t // GRID_W, GRID_W, d).transpose(1, 0, 2).reshape(t, d)


def _from_column_major(h):
    t, d = h.shape
    return h.reshape(GRID_W, t // GRID_W, d).transpose(1, 0, 2).reshape(t, d)


def kernel(x, c, ctx, c_ctx, mod_w, mod_b, norm_mix, norm_ffn, ffn_w1, ffn_w3, ffn_w2, ev_w_in, ssd_conv_w, ssd_conv_b, ssd_dt_bias, ssd_a_log, ssd_d, ssd_norm, gla_gate_w, gla_gate_b, gla_norm, ev_w_out, od_w_in, gdn_conv_w, gdn_dt_bias, gdn_a_log, gdn_norm, od_w_out, final_norm):
    batch, seq, d = x.shape
    n_ctx = ctx.shape[1]
    assert batch == 1 and n_ctx == ROW_TILE and seq % ROW_TILE == 0 and seq % GRID_W == 0
    assert mod_w.shape[0] == 2, "one even (SSD + GLA) layer followed by one odd (gated DeltaNet) layer"
    n_ctx_tiles = n_ctx // ROW_TILE
    n_ctx_chunks = n_ctx // CHUNK

    mods = _modulation(c, c_ctx, mod_w, mod_b)
    x_ctx, x_lat = ctx[0], x[0]

    w_in, widths, dtypes = _even_w_in(ev_w_in[0])
    z, xbc, q, k, v, r, small = _norm_proj(x_ctx, x_lat, norm_mix[0], mods[0], w_in, widths, dtypes)
    xs, bm, cm, dt, da, gk_f, gk_b = _even_prep(xbc, small, ssd_conv_w[0], ssd_conv_b[0], ssd_dt_bias[0],
                                                ssd_a_log[0], gla_gate_w[0], gla_gate_b[0])
    y_f, y_b = _ssd_scan(xs, bm, cm, dt, da, n_ctx_chunks)
    o_f, o_b = _gla_scan(q, k, v, gk_f, gk_b, n_ctx_chunks)
    mixed = _even_merge(y_f, y_b, xs, z, o_f, o_b, r, ssd_d[0], ssd_norm[0], gla_norm[0], ev_w_out[0].astype(BF16))
    x_ctx, x_lat = _ffn(x_ctx, x_lat, mixed, mods[0], norm_ffn[0], ffn_w1[0].astype(BF16), ffn_w3[0].astype(BF16),
                        ffn_w2[0].astype(BF16), final_norm, False)

    w_in, widths, dtypes = _odd_w_in(od_w_in[0])
    qkv, z, small = _norm_proj(x_ctx, _to_column_major(x_lat), norm_mix[1], mods[1], w_in, widths, dtypes)
    q, k, v, g, beta = _odd_prep(qkv, small, gdn_conv_w[0], gdn_dt_bias[0], gdn_a_log[0])
    o_f, o_b = _gdn_scan(q, k, v, g, beta, n_ctx_chunks)
    mixed = _odd_merge(o_f, o_b, z, gdn_norm[0], od_w_out[0].astype(BF16), n_ctx_tiles)
    _, out = _ffn(None, x_lat, _from_column_major(mixed), mods[1], norm_ffn[1], ffn_w1[1].astype(BF16),
                  ffn_w3[1].astype(BF16), ffn_w2[1].astype(BF16), final_norm, True)
    return out[None]
```

```python
import functools

import jax
import jax.numpy as jnp
from jax import lax
from jax.experimental import pallas as pl
from jax.experimental.pallas import tpu as pltpu

F32 = jnp.float32
BF16 = jnp.bfloat16
HIGHEST = lax.Precision.HIGHEST

EPS = 1e-6
LOG2_E = 1.4426950408889634
CHUNK = 64
SCAN_CPS = 4
GLA_CPS = 4
GDN_CPS = 2
GRID_W = 64
CONV_W = 5
ROW_TILE = 256
HALO = 8
LANES = 128
VMEM_LIMIT_BYTES = 56 * 1024 * 1024

SSD_HEADS = 16
SSD_HEAD_DIM = 64
SSD_GROUPS = 2
SSD_STATE = 128
GLA_HEADS = 4
GLA_DK = 128
GLA_DV = 256
GLA_GATE_NORM = 16.0
GDN_QK_HEADS = 8
GDN_V_HEADS = 16
GDN_DK = 128
GDN_DV = 128
SMALL = 16

NT_DIMS = (((1,), (1,)), ((), ()))
TN_DIMS = (((0,), (0,)), ((), ()))


def _cparams(sem):
    return pltpu.CompilerParams(dimension_semantics=sem, vmem_limit_bytes=VMEM_LIMIT_BYTES)


def _bdot(a, b):
    return jnp.dot(a.astype(BF16), b.astype(BF16), preferred_element_type=F32)


def _bdot_nt(a, b):
    return lax.dot_general(a.astype(BF16), b.astype(BF16), NT_DIMS, preferred_element_type=F32)


def _bdot_tn(a, b):
    return lax.dot_general(a.astype(BF16), b.astype(BF16), TN_DIMS, preferred_element_type=F32)


def _split3(x):
    hi = x.astype(BF16)
    r1 = x - hi.astype(F32)
    mid = r1.astype(BF16)
    lo = (r1 - mid.astype(F32)).astype(BF16)
    return hi, mid, lo


def _sel_dot_l(sel3, x):
    return jnp.dot(sel3, jnp.concatenate(_split3(x), axis=0), preferred_element_type=F32)


def _sel_dot_r(x, sel3):
    pieces = jnp.concatenate([p.astype(F32) for p in _split3(x)], axis=1)
    return jnp.dot(pieces.astype(BF16), sel3, preferred_element_type=F32)


def _silu(x):
    return x * jax.nn.sigmoid(x)


def _softplus(x):
    return jnp.maximum(x, 0.0) + jnp.log1p(jnp.exp(-jnp.abs(x)))


def _mod_kernel(c_ref, w_ref, b_ref, o_ref):
    s = _silu(c_ref[...])
    o_ref[0] = jnp.dot(s, w_ref[0], precision=HIGHEST, preferred_element_type=F32) + b_ref[0]


def _modulation(c, c_ctx, mod_w, mod_b):
    depth, d, n = mod_w.shape
    cs = jnp.zeros((8, d), F32).at[0].set(c[0]).at[1].set(c_ctx)
    tn = n // 4
    return pl.pallas_call(
        _mod_kernel,
        grid=(depth, n // tn),
        in_specs=[pl.BlockSpec((8, d), lambda i, j: (0, 0)),
                  pl.BlockSpec((1, d, tn), lambda i, j: (i, 0, j)),
                  pl.BlockSpec((1, 1, tn), lambda i, j: (i, 0, j))],
        out_specs=pl.BlockSpec((1, 8, tn), lambda i, j: (i, 0, j)),
        out_shape=jax.ShapeDtypeStruct((depth, 8, n), F32),
        compiler_params=_cparams(("arbitrary", "arbitrary")),
        name="modulation",
    )(cs, mod_w, mod_b.reshape(depth, 1, n))


def _mod_rows(mod_ref, is_ctx, idx, d):
    sl = slice(idx * d, (idx + 1) * d)
    return jnp.where(is_ctx, mod_ref[1:2, sl], mod_ref[0:1, sl])


def _norm_mod(x, nw, shift, scale):
    ms = jnp.mean(x * x, axis=-1, keepdims=True)
    return (x * lax.rsqrt(ms + EPS) * nw) * (1.0 + scale) + shift


def _ctx_lat_specs(n_ctx_tiles, d):
    ctx = pl.BlockSpec((ROW_TILE, d), lambda i: (jnp.minimum(i, n_ctx_tiles - 1), 0))
    lat = pl.BlockSpec((ROW_TILE, d), lambda i: (jnp.maximum(i - n_ctx_tiles, 0), 0))
    return ctx, lat


def _proj_kernel(xc_ref, xl_ref, nw_ref, mod_ref, w_ref, *out_refs, widths, n_ctx_tiles):
    d = xl_ref.shape[1]
    is_ctx = pl.program_id(0) < n_ctx_tiles
    x = jnp.where(is_ctx, xc_ref[...], xl_ref[...])
    h = _norm_mod(x, nw_ref[...], _mod_rows(mod_ref, is_ctx, 0, d), _mod_rows(mod_ref, is_ctx, 1, d))
    h = h.astype(BF16)
    off = 0
    for o_ref, wd in zip(out_refs, widths):
        o_ref[...] = jnp.dot(h, w_ref[:, off:off + wd], preferred_element_type=F32).astype(o_ref.dtype)
        off += wd


def _norm_proj(x_ctx, x_lat, nw, mod, w, widths, dtypes):
    d = x_lat.shape[1]
    n_ctx_tiles = x_ctx.shape[0] // ROW_TILE
    ta = x_ctx.shape[0] + x_lat.shape[0]
    ntot = sum(widths)
    kern = functools.partial(_proj_kernel, widths=widths, n_ctx_tiles=n_ctx_tiles)
    return pl.pallas_call(
        kern,
        grid=(ta // ROW_TILE,),
        in_specs=[*_ctx_lat_specs(n_ctx_tiles, d),
                  pl.BlockSpec((1, d), lambda i: (0, 0)),
                  pl.BlockSpec(mod.shape, lambda i: (0, 0)),
                  pl.BlockSpec((d, ntot), lambda i: (0, 0))],
        out_specs=[pl.BlockSpec((ROW_TILE, wd), lambda i: (i, 0)) for wd in widths],
        out_shape=[jax.ShapeDtypeStruct((ta, wd), dt) for wd, dt in zip(widths, dtypes)],
        compiler_params=_cparams(("arbitrary",)),
        name="norm_proj",
    )(x_ctx, x_lat, nw.reshape(1, d), mod, w)


def _halo_specs(width, n_tiles):
    per = ROW_TILE // HALO
    prev = pl.BlockSpec((HALO, width), lambda i: (jnp.maximum(i * per - 1, 0), 0))
    nxt = pl.BlockSpec((HALO, width), lambda i: (jnp.minimum((i + 1) * per, n_tiles * per - 1), 0))
    return prev, nxt


CONV_BIAS_ROW = CONV_W


def _conv_weights(conv_w, conv_b):
    cdim = conv_w.shape[1]
    w = jnp.zeros((8, cdim), F32).at[:CONV_W].set(conv_w)
    if conv_b is not None:
        w = w.at[CONV_BIAS_ROW].set(conv_b)
    return w.reshape(8, cdim // LANES, LANES).transpose(1, 0, 2)


def _conv_silu(cur_ref, prev_ref, next_ref, w_ref, ext_ref, out_ref, n_tiles, n_iter, slabs):
    i = pl.program_id(0)
    prev_ok = jnp.logical_and(i != 0, i != 1)
    next_ok = jnp.logical_and(i != 0, i != n_tiles - 1)
    for c in range(ext_ref.shape[0]):
        lanes = slice(c * LANES, (c + 1) * LANES)
        ext_ref[c, 0:HALO, :] = jnp.where(prev_ok, prev_ref[:, lanes], 0.0)
        ext_ref[c, HALO:HALO + ROW_TILE, :] = cur_ref[:, lanes]
        ext_ref[c, HALO + ROW_TILE:, :] = jnp.where(next_ok, next_ref[:, lanes], 0.0)
    pad = CONV_W // 2
    stride = 8
    starts = [base + r for base in range(0, ROW_TILE, 8 * stride) for r in range(stride)]

    def iteration(i, carry):
        for slab_of, post in slabs:
            c = slab_of(i)
            w = w_ref[c]
            accs = [w[CONV_BIAS_ROW:CONV_BIAS_ROW + 1, :] for _ in starts]
            for j in range(CONV_W):
                taps = [ext_ref[c, pl.ds(HALO - pad + j + s, 8, stride=stride), :] for s in starts]
                accs = [acc + tap * w[j:j + 1, :] for acc, tap in zip(accs, taps)]
            ys = post([_silu(acc) for acc in accs])
            for s, y in zip(starts, ys):
                out_ref[c, pl.ds(s, 8, stride=stride), :] = y
        return carry

    lax.fori_loop(0, n_iter, iteration, 0)


def _even_prep_kernel(xbc_ref, prev_ref, next_ref, small_ref, cw_ref, dtb_ref, aneg_ref,
                      gw_ref, gb_ref, xs_ref, bm_ref, cm_ref, dt_ref, da_ref, gkf_ref, gkb_ref,
                      ext_ref, conv_ref, *, n_tiles):
    n_slabs = ext_ref.shape[0]
    _conv_silu(xbc_ref, prev_ref, next_ref, cw_ref, ext_ref, conv_ref, n_tiles, n_slabs, [(lambda i: i, lambda ys: ys)])
    slab = 0
    for o_ref in (xs_ref, bm_ref, cm_ref):
        for c in range(o_ref.shape[1] // LANES):
            o_ref[:, c * LANES:(c + 1) * LANES] = conv_ref[slab]
            slab += 1
    small = small_ref[...]
    dt = _softplus(small + dtb_ref[...])
    dt_ref[...] = dt
    da_ref[...] = dt * aneg_ref[...]
    s_hi = small.astype(BF16)
    s_lo = (small - s_hi.astype(F32)).astype(BF16)
    lhs = jnp.concatenate([s_hi, s_lo, s_hi], axis=1)
    for d, o_ref in enumerate((gkf_ref, gkb_ref)):
        pre = jnp.dot(lhs, gw_ref[d], preferred_element_type=F32) + gb_ref[d]
        o_ref[...] = -_softplus(-pre) * (1.0 / GLA_GATE_NORM)


def _even_prep(xbc, small, conv_w, conv_b, dt_bias, a_log, gate_w, gate_b):
    ta, cdim = xbc.shape
    n_tiles = ta // ROW_TILE
    inner = SSD_HEADS * SSD_HEAD_DIM
    gs = SSD_GROUPS * SSD_STATE
    gkey = GLA_HEADS * GLA_DK
    cw = _conv_weights(conv_w, conv_b)
    dtb = jnp.zeros((1, LANES), F32).at[0, :2 * SMALL].set(dt_bias.reshape(-1))
    aneg = jnp.zeros((1, LANES), F32).at[0, :2 * SMALL].set(-jnp.exp(a_log.reshape(-1)))
    gw = jnp.zeros((2, LANES, gkey), F32)
    gw = gw.at[0, 2 * SMALL:3 * SMALL].set(gate_w[0]).at[1, 3 * SMALL:4 * SMALL].set(gate_w[1])
    gw_hi = gw.astype(BF16)
    gw_lo = (gw - gw_hi.astype(F32)).astype(BF16)
    gw = jnp.concatenate([gw_hi, gw_hi, gw_lo], axis=1)
    prev_spec, next_spec = _halo_specs(cdim, n_tiles)
    row = lambda w: pl.BlockSpec((ROW_TILE, w), lambda i: (i, 0))
    full = lambda a: pl.BlockSpec(a.shape, lambda i: (0,) * a.ndim)
    gb = gate_b.reshape(2, 1, gkey)
    return pl.pallas_call(
        functools.partial(_even_prep_kernel, n_tiles=n_tiles),
        grid=(n_tiles,),
        in_specs=[row(cdim), prev_spec, next_spec, row(LANES), full(cw), full(dtb), full(aneg), full(gw), full(gb)],
        out_specs=[row(inner), row(gs), row(gs), row(LANES), row(LANES), row(gkey), row(gkey)],
        out_shape=[jax.ShapeDtypeStruct((ta, w), F32) for w in (inner, gs, gs, LANES, LANES, gkey, gkey)],
        scratch_shapes=[pltpu.VMEM((cdim // LANES, ROW_TILE + 2 * HALO, LANES), F32),
                        pltpu.VMEM((cdim // LANES, ROW_TILE, LANES), F32)],
        compiler_params=_cparams(("arbitrary",)),
        name="even_prep",
    )(xbc, xbc, xbc, small, cw, dtb, aneg, gw, gb)


def _scan_consts(rev, heads, lanes_per_head):
    r = jnp.arange(CHUNK)
    tri = (r[:, None] <= r[None, :]) if rev else (r[:, None] >= r[None, :])
    strict = (r[:, None] < r[None, :]) if rev else (r[:, None] > r[None, :])
    eye = r[:, None] == r[None, :]
    expand = (jnp.arange(heads)[:, None] == (jnp.arange(heads * lanes_per_head)[None, :] // lanes_per_head))
    return dict(
        tri3=jnp.tile(tri, (1, 3)).astype(BF16),
        expand3=jnp.tile(expand, (3, 1)).astype(BF16),
        tri_t=jnp.tile(tri, (1, heads)).astype(F32),
        strict_t=jnp.tile(strict, (1, heads)).astype(F32),
        eye_t=jnp.tile(eye, (1, heads)).astype(F32),
    )


def _pair_mask():
    r = jnp.arange(2 * CHUNK)
    return ((r[:, None] // CHUNK) == (r[None, :] // CHUNK)).astype(F32)


def _level_masks():
    r = jnp.arange(2 * CHUNK)
    same = lambda b: (r[:, None] // b) == (r[None, :] // b)
    sizes = [2 ** m for m in range(CHUNK.bit_length() - 1)]
    return jnp.stack([jnp.logical_and(same(2 * b), jnp.logical_not(same(b))) for b in sizes]).astype(F32)


def _chunk_map(rev, n_chunks, n_ctx_chunks):
    if not rev:
        return lambda i: i
    return lambda i: jnp.where(i < n_ctx_chunks, n_ctx_chunks - 1 - i, n_chunks + n_ctx_chunks - 1 - i)


def _decay_tables(tri3s, expand3, eye_t, tri_ts, gs):
    cums = [_sel_dot_l(tri3, g) * LOG2_E for tri3, g in zip(tri3s, gs)]
    cols = [_sel_dot_r(cum, expand3) for cum in cums]
    rowfs = [jnp.sum(col * eye_t, axis=0, keepdims=True) for col in cols]
    decays = [jnp.where(tri_t > 0.5, jnp.exp2(col - rowf), 0.0) for tri_t, col, rowf in zip(tri_ts, cols, rowfs)]
    return cols, decays


SSD_SCAN_INS = 5


def _ssd_kernel(*refs, cps):
    ins = refs[:2 * SSD_SCAN_INS]
    tri_ref, expand_ref, eye_ref, trit_ref, pair_ref, yf_ref, yb_ref, st_ref = refs[2 * SSD_SCAN_INS:]

    @pl.when(pl.program_id(0) == 0)
    def _():
        st_ref[...] = jnp.zeros(st_ref.shape, F32)

    pair = pair_ref[...]
    heads_g = SSD_HEADS // SSD_GROUPS
    gw = heads_g * SSD_HEAD_DIM
    pieces = [(d, c, g) for d in range(2) for c in range(cps) for g in range(SSD_GROUPS)]
    gsl = lambda g: slice(g * SSD_STATE, (g + 1) * SSD_STATE)
    lanes = lambda g: slice(g * gw, (g + 1) * gw)
    rows = lambda d, c: slice((cps - 1 - c if d else c) * CHUNK, (cps - c if d else c + 1) * CHUNK)
    chunks = [(d, c) for d in range(2) for c in range(cps)]
    small = lambda d, c, k: ins[SSD_SCAN_INS * d + k][rows(d, c), d * SMALL:(d + 1) * SMALL]
    cols, decays = _decay_tables([tri_ref[d] for d, _ in chunks], expand_ref[...], eye_ref[...],
                                 [trit_ref[d] for d, _ in chunks], [small(d, c, 4) for d, c in chunks])
    dtes = [_sel_dot_r(small(d, c, 3), expand_ref[...]) for d, c in chunks]
    tabs = {}
    for (d, c), col, decay, dte in zip(chunks, cols, decays, dtes):
        last_row = 0 if d else CHUNK - 1
        xdt = ins[SSD_SCAN_INS * d][rows(d, c), :] * dte
        last = col[last_row:last_row + 1, :]
        tabs[d, c] = dict(decay=decay, xdt=xdt, xw=xdt * jnp.exp2(last - col), chunk_decay=jnp.exp2(last),
                          eac=jnp.exp2(col))
    bgs = {(d, c, g): ins[SSD_SCAN_INS * d + 1][rows(d, c), gsl(g)].astype(BF16) for d, c, g in pieces}
    cgs = {(d, c, g): ins[SSD_SCAN_INS * d + 2][rows(d, c), gsl(g)].astype(BF16) for d, c, g in pieces}
    wms = {p: lax.dot_general(cgs[p], jnp.concatenate([bgs[p]] * heads_g, axis=0), NT_DIMS,
                              preferred_element_type=F32) * tabs[p[:2]]["decay"][:, lanes(p[2])] for p in pieces}
    news = {(d, c, g): lax.dot_general(bgs[d, c, g], tabs[d, c]["xw"][:, lanes(g)].astype(BF16), TN_DIMS,
                                       preferred_element_type=F32) for d, c, g in pieces}
    diags = {}
    for d, c, g in pieces:
        xg = tabs[d, c]["xdt"][:, lanes(g)]
        for j in range(gw // LANES):
            xp = xg[:, j * LANES:(j + 1) * LANES]
            blockdiag = jnp.concatenate([xp, xp], axis=0) * pair
            diags[d, c, g, j] = _bdot(wms[d, c, g][:, j * LANES:(j + 1) * LANES], blockdiag)
    states = {}
    for d in range(2):
        for g in range(SSD_GROUPS):
            state = st_ref[d, g]
            for c in range(cps):
                states[d, c, g] = state
                state = state * tabs[d, c]["chunk_decay"][:, lanes(g)] + news[d, c, g]
            st_ref[d, g] = state
    offs = {p: jnp.dot(cgs[p], states[p].astype(BF16), preferred_element_type=F32) for p in pieces}
    for d, y_ref in enumerate((yf_ref, yb_ref)):
        for c in range(cps):
            for g in range(SSD_GROUPS):
                diag = jnp.concatenate([diags[d, c, g, j] for j in range(gw // LANES)], axis=1)
                y_ref[rows(d, c), lanes(g)] = (diag + offs[d, c, g] * tabs[d, c]["eac"][:, lanes(g)]).astype(y_ref.dtype)


def _ssd_scan(xs, bm, cm, dt, da, n_ctx_chunks):
    ta, inner = xs.shape
    cps = SCAN_CPS
    nb = ta // (cps * CHUNK)
    fwd = _scan_consts(False, SSD_HEADS, SSD_HEAD_DIM)
    bwd = _scan_consts(True, SSD_HEADS, SSD_HEAD_DIM)
    both = lambda name: jnp.stack([fwd[name], bwd[name]])
    consts = [both("tri3"), fwd["expand3"], fwd["eye_t"], both("tri_t"), _pair_mask()]
    full = lambda a: pl.BlockSpec(a.shape, lambda i: (0,) * a.ndim)
    gs = SSD_GROUPS * SSD_STATE

    def row(rev, w):
        cmap = _chunk_map(rev, nb, n_ctx_chunks // cps)
        return pl.BlockSpec((cps * CHUNK, w), lambda i: (cmap(i), 0))

    dir_specs = lambda rev: [row(rev, inner), row(rev, gs), row(rev, gs), row(rev, LANES), row(rev, LANES)]
    return pl.pallas_call(
        functools.partial(_ssd_kernel, cps=cps),
        grid=(nb,),
        in_specs=dir_specs(False) + dir_specs(True) + [full(a) for a in consts],
        out_specs=[row(False, inner), row(True, inner)],
        out_shape=[jax.ShapeDtypeStruct((ta, inner), BF16)] * 2,
        scratch_shapes=[pltpu.VMEM((2, SSD_GROUPS, SSD_STATE, inner // SSD_GROUPS), F32)],
        compiler_params=_cparams(("arbitrary",)),
        name="ssd_scan",
    )(xs, bm, cm, dt, da, xs, bm, cm, dt, da, *consts)


GLA_SUB = 8
GLA_LEVELS = (8, 16, 32)
GLA_SCAN_INS = 4


def _gla_masks():
    r = jnp.arange(CHUNK)
    same = lambda b: (r[:, None] // b) == (r[None, :] // b)
    out = []
    for rev in (False, True):
        strict = (r[:, None] < r[None, :]) if rev else (r[:, None] > r[None, :])
        lv = [same(2 * b) & ~same(b) & strict for b in GLA_LEVELS]
        within = r % GLA_SUB
        dg = []
        for s in range(GLA_SUB):
            seen = (within <= s) if rev else (within >= s)
            dg.append((r[None, :] == (r[:, None] // GLA_SUB) * GLA_SUB + s) & seen[:, None])
        out.append(jnp.stack(lv + dg))
    return jnp.stack(out).astype(F32)


def _block_rows(a, rows, b):
    return jnp.concatenate([jnp.broadcast_to(a[r:r + 1, :], (b, a.shape[1])) for r in rows], axis=0)


def _gla_kernel(*refs, cps):
    ins = refs[:2 * GLA_SCAN_INS]
    tri_ref, mask_ref, of_ref, ob_ref, st_ref = refs[2 * GLA_SCAN_INS:]

    @pl.when(pl.program_id(0) == 0)
    def _():
        st_ref[...] = jnp.zeros(st_ref.shape, F32)

    pieces = [(d, c, h) for d in range(2) for c in range(cps) for h in range(GLA_HEADS)]
    ksl = lambda h: slice(h * GLA_DK, (h + 1) * GLA_DK)
    vsl = lambda h: slice(h * GLA_DV, (h + 1) * GLA_DV)
    rws = lambda d, c: slice((cps - 1 - c if d else c) * CHUNK, (cps - c if d else c + 1) * CHUNK)
    q_of = lambda d, c, h: ins[GLA_SCAN_INS * d][rws(d, c), ksl(h)] * (GLA_DK ** -0.5)
    k_of = lambda d, c, h: ins[GLA_SCAN_INS * d + 1][rws(d, c), ksl(h)]
    v_of = lambda d, c, h: ins[GLA_SCAN_INS * d + 2][rws(d, c), vsl(h)]
    within = lax.broadcasted_iota(jnp.int32, (CHUNK, 1), 0) % GLA_SUB
    gcs = {(d, c, h): _sel_dot_l(tri_ref[d], ins[GLA_SCAN_INS * d + 3][rws(d, c), ksl(h)]) * LOG2_E
           for d, c, h in pieces}
    news = {}
    for d, c, h in pieces:
        gc = gcs[d, c, h]
        last = 0 if d else CHUNK - 1
        glast = gc[last:last + 1, :]
        news[d, c, h] = (jnp.exp2(glast), _bdot_tn(v_of(d, c, h), k_of(d, c, h) * jnp.exp2(glast - gc)))
    states = {}
    for d in range(2):
        for h in range(GLA_HEADS):
            state = st_ref[d, h]
            for c in range(cps):
                states[d, c, h] = state
                state = state * news[d, c, h][0] + news[d, c, h][1]
            st_ref[d, h] = state
    inters = {p: _bdot_nt(q_of(*p) * jnp.exp2(gcs[p]), states[p]) for p in pieces}
    atts = {}
    for d, c, h in pieces:
        gc, qh, kh = gcs[d, c, h], q_of(d, c, h), k_of(d, c, h)
        att = None
        for li, b in enumerate(GLA_LEVELS):
            nb = CHUNK // b
            if d:
                rq = [min(b * (i + 1), CHUNK - 1) for i in range(nb)]
                rk = [b * i for i in range(nb)]
            else:
                rq = [max(b * i - 1, 0) for i in range(nb)]
                rk = [b * (i + 1) - 1 for i in range(nb)]
            qs = qh * jnp.exp2(gc - _block_rows(gc, rq, b))
            ks = kh * jnp.exp2(_block_rows(gc, rk, b) - gc)
            part = _bdot_nt(qs, ks) * mask_ref[d, li]
            att = part if att is None else att + part
        for s in range(GLA_SUB):
            rows = [GLA_SUB * i + s for i in range(CHUNK // GLA_SUB)]
            seen = (within <= s) if d else (within >= s)
            e = jnp.where(seen, jnp.exp2(gc - _block_rows(gc, rows, GLA_SUB)), 0.0)
            a = jnp.sum(qh * _block_rows(kh, rows, GLA_SUB) * e, axis=-1, keepdims=True)
            att = att + a * mask_ref[d, len(GLA_LEVELS) + s]
        atts[d, c, h] = att
    for d, o_ref in enumerate((of_ref, ob_ref)):
        for c in range(cps):
            for h in range(GLA_HEADS):
                o_ref[rws(d, c), vsl(h)] = (inters[d, c, h] + _bdot(atts[d, c, h], v_of(d, c, h))).astype(o_ref.dtype)


def _gla_scan(q, k, v, gk_f, gk_b, n_ctx_chunks):
    ta, kdim = q.shape
    vdim = v.shape[1]
    cps = GLA_CPS
    nb = ta // (cps * CHUNK)
    tri3 = jnp.stack([_scan_consts(rev, 1, 1)["tri3"] for rev in (False, True)])
    masks = _gla_masks()
    full = lambda a: pl.BlockSpec(a.shape, lambda i: (0,) * a.ndim)

    def row(rev, w):
        cmap = _chunk_map(rev, nb, n_ctx_chunks // cps)
        return pl.BlockSpec((cps * CHUNK, w), lambda i: (cmap(i), 0))

    dir_specs = lambda rev: [row(rev, kdim), row(rev, kdim), row(rev, vdim), row(rev, kdim)]
    return pl.pallas_call(
        functools.partial(_gla_kernel, cps=cps),
        grid=(nb,),
        in_specs=dir_specs(False) + dir_specs(True) + [full(tri3), full(masks)],
        out_specs=[row(False, vdim), row(True, vdim)],
        out_shape=[jax.ShapeDtypeStruct((ta, vdim), BF16)] * 2,
        scratch_shapes=[pltpu.VMEM((2, GLA_HEADS, GLA_DV, GLA_DK), F32)],
        compiler_params=_cparams(("arbitrary",)),
        name="gla_scan",
    )(q, k, v, gk_f, q, k, v, gk_b, tri3, masks)


def _group_rms(y, w, width):
    parts = []
    for g in range(y.shape[1] // width):
        yg = y[:, g * width:(g + 1) * width]
        ms = jnp.mean(yg * yg, axis=-1, keepdims=True)
        parts.append(yg * lax.rsqrt(ms + EPS))
    return jnp.concatenate(parts, axis=1) * w


def _even_merge_kernel(yf_ref, yb_ref, xs_ref, z_ref, of_ref, ob_ref, r_ref, dsk_ref, snw_ref, gnw_ref, w_ref, o_ref):
    inner = xs_ref.shape[1]
    f32 = lambda ref: ref[...].astype(F32)
    y = (f32(yf_ref) + f32(yb_ref) + dsk_ref[...] * xs_ref[...]) * _silu(f32(z_ref))
    y = _group_rms(y, snw_ref[...], inner // SSD_GROUPS)
    o = _group_rms(f32(of_ref) + f32(ob_ref), gnw_ref[...], GLA_DV) * _silu(f32(r_ref))
    o_ref[...] = _bdot(y, w_ref[:inner, :]) + _bdot(o, w_ref[inner:, :])


def _even_merge(yf, yb, xs, z, of, ob, r, d_skip, ssd_nw, gla_nw, w_out):
    ta, inner = xs.shape
    vdim = of.shape[1]
    d = w_out.shape[1]
    dsk = jnp.repeat(d_skip, SSD_HEAD_DIM).reshape(1, inner)
    gnw = jnp.tile(gla_nw, GLA_HEADS).reshape(1, vdim)
    row = lambda w: pl.BlockSpec((ROW_TILE, w), lambda i: (i, 0))
    vec = lambda w: pl.BlockSpec((1, w), lambda i: (0, 0))
    return pl.pallas_call(
        _even_merge_kernel,
        grid=(ta // ROW_TILE,),
        in_specs=[row(inner)] * 4 + [row(vdim)] * 3 + [vec(inner), vec(inner), vec(vdim),
                                                        pl.BlockSpec(w_out.shape, lambda i: (0, 0))],
        out_specs=row(d),
        out_shape=jax.ShapeDtypeStruct((ta, d), F32),
        compiler_params=_cparams(("arbitrary",)),
        name="even_merge",
    )(yf, yb, xs, z, of, ob, r, dsk, ssd_nw.reshape(1, inner), gnw, w_out)


def _odd_prep_kernel(qkv_ref, prev_ref, next_ref, small_ref, cw_ref, dtb_ref, aneg_ref,
                     q_ref, k_ref, v_ref, g_ref, beta_ref, ext_ref, conv_ref, *, n_tiles):
    assert GDN_DK == LANES
    kslabs = q_ref.shape[1] // LANES
    n_slabs = ext_ref.shape[0]
    def l2n(scale):
        def post(groups):
            sums = [jnp.sum(a * a, axis=-1, keepdims=True) for a in groups]
            return [a * (lax.rsqrt(s + EPS) * scale) for a, s in zip(groups, sums)]
        return post

    v_per_iter = (n_slabs - 2 * kslabs) // kslabs
    v_slabs = [(lambda i, m=m: 2 * kslabs + v_per_iter * i + m, lambda ys: ys) for m in range(v_per_iter)]
    _conv_silu(qkv_ref, prev_ref, next_ref, cw_ref, ext_ref, conv_ref, n_tiles, kslabs,
               [(lambda i: i, l2n(GDN_DK ** -0.5)), (lambda i: kslabs + i, l2n(1.0))] + v_slabs)
    slab = 0
    for o_ref in (q_ref, k_ref, v_ref):
        for c in range(o_ref.shape[1] // LANES):
            o_ref[:, c * LANES:(c + 1) * LANES] = conv_ref[slab]
            slab += 1
    small = small_ref[...]
    g_ref[...] = _softplus(small + dtb_ref[...]) * aneg_ref[...]
    beta_ref[...] = jax.nn.sigmoid(small)


def _odd_prep(qkv, small, conv_w, dt_bias, a_log):
    ta, cdim = qkv.shape
    n_tiles = ta // ROW_TILE
    kdim = GDN_QK_HEADS * GDN_DK
    vdim = GDN_V_HEADS * GDN_DV
    cw = _conv_weights(conv_w, None)
    dtb = jnp.zeros((1, LANES), F32).at[0, :2 * SMALL].set(dt_bias.reshape(-1))
    aneg = jnp.zeros((1, LANES), F32).at[0, :2 * SMALL].set(-jnp.exp(a_log.reshape(-1)))
    prev_spec, next_spec = _halo_specs(cdim, n_tiles)
    row = lambda w: pl.BlockSpec((ROW_TILE, w), lambda i: (i, 0))
    full = lambda a: pl.BlockSpec(a.shape, lambda i: (0,) * a.ndim)
    return pl.pallas_call(
        functools.partial(_odd_prep_kernel, n_tiles=n_tiles),
        grid=(n_tiles,),
        in_specs=[row(cdim), prev_spec, next_spec, row(LANES), full(cw), full(dtb), full(aneg)],
        out_specs=[row(kdim), row(kdim), row(vdim), row(LANES), row(LANES)],
        out_shape=[jax.ShapeDtypeStruct((ta, w), F32) for w in (kdim, kdim, vdim, LANES, LANES)],
        scratch_shapes=[pltpu.VMEM((cdim // LANES, ROW_TILE + 2 * HALO, LANES), F32),
                        pltpu.VMEM((cdim // LANES, ROW_TILE, LANES), F32)],
        compiler_params=_cparams(("arbitrary",)),
        name="odd_prep",
    )(qkv, qkv, qkv, small, cw, dtb, aneg)


GDN_SCAN_INS = 5


def _interleave(phases):
    phases = list(phases)
    while phases:
        for ph in list(phases):
            if next(ph, StopIteration) is StopIteration:
                phases.remove(ph)


def _gdn_kernel(*refs, cps):
    ins = refs[:2 * GDN_SCAN_INS]
    (tri_ref, expand_ref, eye_ref, trit_ref, strict_ref, pair_ref, lvl_ref, eye2_ref,
     of_ref, ob_ref, st_ref) = refs[2 * GDN_SCAN_INS:]

    @pl.when(pl.program_id(0) == 0)
    def _():
        st_ref[...] = jnp.zeros(st_ref.shape, F32)

    pair = pair_ref[...]
    eye2 = eye2_ref[...]
    left = lax.broadcasted_iota(jnp.int32, (CHUNK, 2 * CHUNK), 1) < CHUNK
    n_levels = lvl_ref.shape[0]
    dirs = range(2)
    o_refs = (of_ref, ob_ref)
    dot16 = lambda x, y: jnp.dot(x, y, preferred_element_type=F32)
    rows = lambda d, c: slice((cps - 1 - c if d else c) * CHUNK, (cps - c if d else c + 1) * CHUNK)
    pairs = [(d, j) for d in dirs for j in range(GDN_QK_HEADS)]
    chains = [(d, h) for d in dirs for h in range(GDN_V_HEADS)]

    keys = [(c, d) for c in range(cps) for d in dirs]
    cols, decays = _decay_tables([tri_ref[d] for _, d in keys], expand_ref[...], eye_ref[...],
                                 [trit_ref[d] for _, d in keys],
                                 [ins[GDN_SCAN_INS * d + 3][rows(d, c), d * SMALL:(d + 1) * SMALL] for c, d in keys])
    beta_ts = [_sel_dot_r(ins[GDN_SCAN_INS * d + 4][rows(d, c), (2 + d) * SMALL:(3 + d) * SMALL], expand_ref[...])
               for c, d in keys]
    tabs = {k: t for k, t in zip(keys, zip(cols, decays, beta_ts))}

    def build(c, out):
        for d in dirs:
            q_ref, k_ref, v_ref = ins[GDN_SCAN_INS * d:GDN_SCAN_INS * d + 3]
            rw = rows(d, c)
            last_row = 0 if d else CHUNK - 1
            col, decay, beta_t = tabs[c, d]
            a_all = beta_t * decay * strict_ref[d]
            for j in range(GDN_QK_HEADS):
                ksl = slice(j * GDN_DK, (j + 1) * GDN_DK)
                psl = slice(j * 2 * CHUNK, (j + 1) * 2 * CHUNK)
                kj = k_ref[rw, ksl]
                kb = kj.astype(BF16)
                krep = jnp.concatenate([kb, kb], axis=0)
                kk = lax.dot_general(kb, krep, NT_DIMS, preferred_element_type=F32)
                qk = lax.dot_general(q_ref[rw, ksl].astype(BF16), krep, NT_DIMS, preferred_element_type=F32)
                out["att"][d, j] = (qk * decay[:, psl]).astype(BF16)
                a_pair = kk * a_all[:, psl]
                a_bd = jnp.concatenate([a_pair, a_pair], axis=0) * pair
                out["tinv"][d, j] = (eye2 - a_bd * lvl_ref[0]).astype(BF16)
                out["a16"][d, j] = a_bd.astype(BF16)
                colp, betap = col[:, psl], beta_t[:, psl]
                colr, betar = pltpu.roll(colp, CHUNK, 1), pltpu.roll(betap, CHUNK, 1)
                rhs = []
                for hh in range(2):
                    h = 2 * j + hh
                    hsl = slice(h * GDN_DV, (h + 1) * GDN_DV)
                    cumx = jnp.where(left, colp, colr) if hh == 0 else jnp.where(left, colr, colp)
                    betax = jnp.where(left, betap, betar) if hh == 0 else jnp.where(left, betar, betap)
                    lastx = cumx[last_row:last_row + 1, :]
                    egx = jnp.exp2(cumx)
                    out["qd"][d, h] = (q_ref[rw, ksl] * egx).astype(BF16)
                    out["kd"][d, h] = (kj * jnp.exp2(lastx - cumx)).astype(BF16)
                    out["gl"][d, h] = jnp.exp2(lastx)
                    rhs.append(jnp.concatenate([v_ref[rw, hsl] * betax, kj * betax * egx], axis=1))
                out["rhs"][d, j] = jnp.concatenate(rhs, axis=0).astype(BF16)
                yield

    def invert(out):
        tinv, a16 = out["tinv"], out["a16"]
        for lv in range(1, n_levels):
            b = 2 ** lv
            mask = lvl_ref[lv].astype(BF16)
            if b < 16:
                tls = {p: dot16(tinv[p], a16[p] * mask).astype(BF16) for p in pairs}
                yield
                for p in pairs:
                    tinv[p] = tinv[p] - dot16(tls[p], tinv[p]).astype(BF16)
                yield
                continue
            blocks = lambda t: [t[i * b:(i + 1) * b] for i in range(t.shape[0] // b)]
            moving = lambda d, t: jnp.concatenate(blocks(t)[(1 - d)::2], axis=0)
            tls = {p: dot16(moving(p[0], tinv[p]), a16[p] * mask).astype(BF16) for p in pairs}
            yield
            for p in pairs:
                new = blocks(moving(p[0], tinv[p]) - dot16(tls[p], tinv[p]).astype(BF16))
                tinv[p] = jnp.concatenate([new[i // 2] if i % 2 == 1 - p[0] else blk
                                           for i, blk in enumerate(blocks(tinv[p]))], axis=0)
            yield
        out["x"] = {p: dot16(tinv[p], out["rhs"][p]) for p in pairs}
        yield

    def recur(c, out, states):
        x_of = lambda d, h: out["x"][d, h // 2][(h % 2) * CHUNK:(h % 2 + 1) * CHUNK]
        ws_qs = {(d, h): dot16(jnp.concatenate([x_of(d, h)[:, GDN_DV:].astype(BF16), out["qd"][d, h]], axis=0),
                               states[d, h].astype(BF16)) for d, h in chains}
        yield
        v_new = {(d, h): (x_of(d, h)[:, :GDN_DV] - ws_qs[d, h][:CHUNK]).astype(BF16) for d, h in chains}
        for d, h in chains:
            att = out["att"][d, h // 2][:, (h % 2) * CHUNK:(h % 2 + 1) * CHUNK]
            o_refs[d][rows(d, c), h * GDN_DV:(h + 1) * GDN_DV] = (
                ws_qs[d, h][CHUNK:] + dot16(att, v_new[d, h])).astype(o_refs[d].dtype)
        yield
        for d, h in chains:
            states[d, h] = states[d, h] * out["gl"][d, h] + lax.dot_general(
                out["kd"][d, h], v_new[d, h], TN_DIMS, preferred_element_type=F32)
        yield

    fields = ("att", "tinv", "a16", "rhs", "qd", "kd", "gl")
    outs = [{f: {} for f in fields} for _ in range(cps)]
    states = {(d, h): st_ref[d, h] for d, h in chains}
    _interleave([build(0, outs[0])])
    for c in range(cps):
        phases = [invert(outs[c])]
        if c + 1 < cps:
            phases.append(build(c + 1, outs[c + 1]))
        if c > 0:
            phases.append(recur(c - 1, outs[c - 1], states))
        _interleave(phases)
    _interleave([recur(cps - 1, outs[cps - 1], states)])
    for d, h in chains:
        st_ref[d, h] = states[d, h]


def _gdn_scan(q, k, v, g, beta, n_ctx_chunks):
    ta, kdim = q.shape
    vdim = v.shape[1]
    cps = GDN_CPS
    nb = ta // (cps * CHUNK)
    fwd = _scan_consts(False, GDN_V_HEADS, CHUNK)
    bwd = _scan_consts(True, GDN_V_HEADS, CHUNK)
    both = lambda name: jnp.stack([fwd[name], bwd[name]])
    consts = [both("tri3"), fwd["expand3"], fwd["eye_t"], both("tri_t"), both("strict_t"),
              _pair_mask(), _level_masks(), jnp.eye(2 * CHUNK, dtype=F32)]
    full = lambda a: pl.BlockSpec(a.shape, lambda i: (0,) * a.ndim)

    def row(rev, w):
        cmap = _chunk_map(rev, nb, n_ctx_chunks // cps)
        return pl.BlockSpec((cps * CHUNK, w), lambda i: (cmap(i), 0))

    dir_specs = lambda rev: [row(rev, kdim), row(rev, kdim), row(rev, vdim), row(rev, LANES), row(rev, LANES)]
    return pl.pallas_call(
        functools.partial(_gdn_kernel, cps=cps),
        grid=(nb,),
        in_specs=dir_specs(False) + dir_specs(True) + [full(a) for a in consts],
        out_specs=[row(False, vdim), row(True, vdim)],
        out_shape=[jax.ShapeDtypeStruct((ta, vdim), BF16)] * 2,
        scratch_shapes=[pltpu.VMEM((2, GDN_V_HEADS, GDN_DK, GDN_DV), F32)],
        compiler_params=_cparams(("arbitrary",)),
        name="gdn_scan",
    )(q, k, v, g, beta, q, k, v, g, beta, *consts)


def _odd_merge_kernel(of_ref, ob_ref, z_ref, nw_ref, w_ref, o_ref):
    f32 = lambda ref: ref[...].astype(F32)
    o = _group_rms(f32(of_ref) + f32(ob_ref), nw_ref[...], GDN_DV) * _silu(f32(z_ref))
    o_ref[...] = _bdot(o, w_ref[...])


def _odd_merge(of, ob, z, gdn_nw, w_out, n_ctx_tiles):
    ta, vdim = of.shape
    d = w_out.shape[1]
    n_lat = ta - n_ctx_tiles * ROW_TILE
    nw = jnp.tile(gdn_nw, GDN_V_HEADS).reshape(1, vdim)
    row_in = pl.BlockSpec((ROW_TILE, vdim), lambda i: (i + n_ctx_tiles, 0))
    return pl.pallas_call(
        _odd_merge_kernel,
        grid=(n_lat // ROW_TILE,),
        in_specs=[row_in] * 3 + [pl.BlockSpec((1, vdim), lambda i: (0, 0)), pl.BlockSpec(w_out.shape, lambda i: (0, 0))],
        out_specs=pl.BlockSpec((ROW_TILE, d), lambda i: (i, 0)),
        out_shape=jax.ShapeDtypeStruct((n_lat, d), F32),
        compiler_params=_cparams(("arbitrary",)),
        name="odd_merge",
    )(of, ob, z, nw, w_out)


def _ffn_kernel(*refs, n_ctx_tiles, final):
    if n_ctx_tiles:
        xc_ref, xl_ref, y_ref, mod_ref, nw_ref, w1_ref, w3_ref, w2_ref, fw_ref, oc_ref, ol_ref = refs
    else:
        xl_ref, y_ref, mod_ref, nw_ref, w1_ref, w3_ref, w2_ref, fw_ref, ol_ref = refs
    d = xl_ref.shape[1]
    is_ctx = pl.program_id(0) < n_ctx_tiles
    x = jnp.where(is_ctx, xc_ref[...], xl_ref[...]) if n_ctx_tiles else xl_ref[...]
    x = x + _mod_rows(mod_ref, is_ctx, 2, d) * y_ref[...]
    h = _norm_mod(x, nw_ref[...], _mod_rows(mod_ref, is_ctx, 3, d), _mod_rows(mod_ref, is_ctx, 4, d)).astype(BF16)
    a = jnp.dot(h, w1_ref[...], preferred_element_type=F32)
    b = jnp.dot(h, w3_ref[...], preferred_element_type=F32)
    out = x + _mod_rows(mod_ref, is_ctx, 5, d) * _bdot(_silu(a) * b, w2_ref[...])
    if final:
        ms = jnp.mean(out * out, axis=-1, keepdims=True)
        out = out * lax.rsqrt(ms + EPS) * fw_ref[...]
    if n_ctx_tiles:
        @pl.when(is_ctx)
        def _():
            oc_ref[...] = out

        @pl.when(jnp.logical_not(is_ctx))
        def _():
            ol_ref[...] = out
    else:
        ol_ref[...] = out


def _ffn(x_ctx, x_lat, y_all, mod, nw, w1, w3, w2, final_w, final):
    d = x_lat.shape[1]
    n_ctx_tiles = 0 if x_ctx is None else x_ctx.shape[0] // ROW_TILE
    n_tiles = n_ctx_tiles + x_lat.shape[0] // ROW_TILE
    row = pl.BlockSpec((ROW_TILE, d), lambda i: (i, 0))
    full = lambda a: pl.BlockSpec(a.shape, lambda i: (0,) * a.ndim)
    nw = nw.reshape(1, d)
    fw = final_w.reshape(1, d)
    params = [mod, nw, w1, w3, w2, fw]
    if n_ctx_tiles:
        x_specs, x_args = list(_ctx_lat_specs(n_ctx_tiles, d)), [x_ctx, x_lat]
        out_specs = list(_ctx_lat_specs(n_ctx_tiles, d))
        out_shape = [jax.ShapeDtypeStruct(x_ctx.shape, F32), jax.ShapeDtypeStruct(x_lat.shape, F32)]
    else:
        x_specs, x_args = [row], [x_lat]
        out_specs, out_shape = [row], [jax.ShapeDtypeStruct(x_lat.shape, F32)]
    outs = pl.pallas_call(
        functools.partial(_ffn_kernel, n_ctx_tiles=n_ctx_tiles, final=final),
        grid=(n_tiles,),
        in_specs=x_specs + [row] + [full(a) for a in params],
        out_specs=out_specs,
        out_shape=out_shape,
        compiler_params=_cparams(("arbitrary",)),
        name="ffn_final" if final else "ffn",
    )(*x_args, y_all, *params)
    return (outs[0], outs[1]) if n_ctx_tiles else (None, outs[0])


def _pad_cols(parts, width):
    w = jnp.concatenate(parts, axis=1)
    return jnp.pad(w, ((0, 0), (0, width - w.shape[1])))


def _even_w_in(w):
    inner = SSD_HEADS * SSD_HEAD_DIM
    conv = inner + 2 * SSD_GROUPS * SSD_STATE
    gkey = GLA_HEADS * GLA_DK
    gval = GLA_HEADS * GLA_DV
    sizes = (inner, conv, SMALL, SMALL, gkey, gkey, gval, gval, SMALL, SMALL)
    offs = [0]
    for s in sizes:
        offs.append(offs[-1] + s)
    z, xbc, dtf, dtb, q, k, v, r, glf, glb = (w[:, offs[i]:offs[i + 1]] for i in range(len(sizes)))
    small = _pad_cols([dtf, dtb, glf, glb], LANES)
    widths = (inner, conv, gkey, gkey, gval, gval, LANES)
    dtypes = (BF16, F32, F32, F32, F32, BF16, F32)
    return jnp.concatenate([z, xbc, q, k, v, r, small], axis=1).astype(BF16), widths, dtypes


def _odd_w_in(w):
    kdim = GDN_QK_HEADS * GDN_DK
    vdim = GDN_V_HEADS * GDN_DV
    conv = 2 * kdim + vdim
    small = _pad_cols([w[:, conv + vdim:]], LANES)
    widths = (conv, vdim, LANES)
    dtypes = (F32, BF16, F32)
    return jnp.concatenate([w[:, :conv + vdim], small], axis=1).astype(BF16), widths, dtypes


def _to_column_major(h):
    t, d = h.shape
    return h.reshape(t // GRID_W, GRID_W, d).transpose(1, 0, 2).reshape(t, d)


def _from_column_major(h):
    t, d = h.shape
    return h.reshape(GRID_W, t // GRID_W, d).transpose(1, 0, 2).reshape(t, d)


def kernel(x, c, ctx, c_ctx, mod_w, mod_b, norm_mix, norm_ffn, ffn_w1, ffn_w3, ffn_w2, ev_w_in, ssd_conv_w, ssd_conv_b, ssd_dt_bias, ssd_a_log, ssd_d, ssd_norm, gla_gate_w, gla_gate_b, gla_norm, ev_w_out, od_w_in, gdn_conv_w, gdn_dt_bias, gdn_a_log, gdn_norm, od_w_out, final_norm):
    batch, seq, d = x.shape
    n_ctx = ctx.shape[1]
    assert batch == 1 and n_ctx == ROW_TILE and seq % ROW_TILE == 0 and seq % GRID_W == 0
    assert mod_w.shape[0] == 2, "one even (SSD + GLA) layer followed by one odd (gated DeltaNet) layer"
    n_ctx_tiles = n_ctx // ROW_TILE
    n_ctx_chunks = n_ctx // CHUNK

    mods = _modulation(c, c_ctx, mod_w, mod_b)
    x_ctx, x_lat = ctx[0], x[0]

    w_in, widths, dtypes = _even_w_in(ev_w_in[0])
    z, xbc, q, k, v, r, small = _norm_proj(x_ctx, x_lat, norm_mix[0], mods[0], w_in, widths, dtypes)
    xs, bm, cm, dt, da, gk_f, gk_b = _even_prep(xbc, small, ssd_conv_w[0], ssd_conv_b[0], ssd_dt_bias[0],
                                                ssd_a_log[0], gla_gate_w[0], gla_gate_b[0])
    y_f, y_b = _ssd_scan(xs, bm, cm, dt, da, n_ctx_chunks)
    o_f, o_b = _gla_scan(q, k, v, gk_f, gk_b, n_ctx_chunks)
    mixed = _even_merge(y_f, y_b, xs, z, o_f, o_b, r, ssd_d[0], ssd_norm[0], gla_norm[0], ev_w_out[0].astype(BF16))
    x_ctx, x_lat = _ffn(x_ctx, x_lat, mixed, mods[0], norm_ffn[0], ffn_w1[0].astype(BF16), ffn_w3[0].astype(BF16),
                        ffn_w2[0].astype(BF16), final_norm, False)

    w_in, widths, dtypes = _odd_w_in(od_w_in[0])
    qkv, z, small = _norm_proj(x_ctx, _to_column_major(x_lat), norm_mix[1], mods[1], w_in, widths, dtypes)
    q, k, v, g, beta = _odd_prep(qkv, small, gdn_conv_w[0], gdn_dt_bias[0], gdn_a_log[0])
    o_f, o_b = _gdn_scan(q, k, v, g, beta, n_ctx_chunks)
    mixed = _odd_merge(o_f, o_b, z, gdn_norm[0], od_w_out[0].astype(BF16), n_ctx_tiles)
    _, out = _ffn(None, x_lat, _from_column_major(mixed), mods[1], norm_ffn[1], ffn_w1[1].astype(BF16),
                  ffn_w3[1].astype(BF16), ffn_w2[1].astype(BF16), final_norm, True)
    return out[None]
```

```python
import functools

import jax
import jax.numpy as jnp
from jax import lax
from jax.experimental import pallas as pl
from jax.experimental.pallas import tpu as pltpu

F32 = jnp.float32
BF16 = jnp.bfloat16
HIGHEST = lax.Precision.HIGHEST

EPS = 1e-6
LOG2_E = 1.4426950408889634
CHUNK = 64
SCAN_CPS = 4
GLA_CPS = 4
GDN_CPS = 2
GRID_W = 64
CONV_W = 5
ROW_TILE = 256
HALO = 8
LANES = 128
VMEM_LIMIT_BYTES = 56 * 1024 * 1024

SSD_HEADS = 16
SSD_HEAD_DIM = 64
SSD_GROUPS = 2
SSD_STATE = 128
GLA_HEADS = 4
GLA_DK = 128
GLA_DV = 256
GLA_GATE_NORM = 16.0
GDN_QK_HEADS = 8
GDN_V_HEADS = 16
GDN_DK = 128
GDN_DV = 128
SMALL = 16

NT_DIMS = (((1,), (1,)), ((), ()))
TN_DIMS = (((0,), (0,)), ((), ()))


def _cparams(sem):
    return pltpu.CompilerParams(dimension_semantics=sem, vmem_limit_bytes=VMEM_LIMIT_BYTES)


def _bdot(a, b):
    return jnp.dot(a.astype(BF16), b.astype(BF16), preferred_element_type=F32)


def _bdot_nt(a, b):
    return lax.dot_general(a.astype(BF16), b.astype(BF16), NT_DIMS, preferred_element_type=F32)


def _bdot_tn(a, b):
    return lax.dot_general(a.astype(BF16), b.astype(BF16), TN_DIMS, preferred_element_type=F32)


def _split3(x):
    hi = x.astype(BF16)
    r1 = x - hi.astype(F32)
    mid = r1.astype(BF16)
    lo = (r1 - mid.astype(F32)).astype(BF16)
    return hi, mid, lo


def _sel_dot_l(sel3, x):
    return jnp.dot(sel3, jnp.concatenate(_split3(x), axis=0), preferred_element_type=F32)


def _sel_dot_r(x, sel3):
    pieces = jnp.concatenate([p.astype(F32) for p in _split3(x)], axis=1)
    return jnp.dot(pieces.astype(BF16), sel3, preferred_element_type=F32)


def _silu(x):
    return x * jax.nn.sigmoid(x)


def _softplus(x):
    return jnp.maximum(x, 0.0) + jnp.log1p(jnp.exp(-jnp.abs(x)))


def _mod_kernel(c_ref, w_ref, b_ref, o_ref):
    s = _silu(c_ref[...])
    o_ref[0] = jnp.dot(s, w_ref[0], precision=HIGHEST, preferred_element_type=F32) + b_ref[0]


def _modulation(c, c_ctx, mod_w, mod_b):
    depth, d, n = mod_w.shape
    cs = jnp.zeros((8, d), F32).at[0].set(c[0]).at[1].set(c_ctx)
    tn = n // 4
    return pl.pallas_call(
        _mod_kernel,
        grid=(depth, n // tn),
        in_specs=[pl.BlockSpec((8, d), lambda i, j: (0, 0)),
                  pl.BlockSpec((1, d, tn), lambda i, j: (i, 0, j)),
                  pl.BlockSpec((1, 1, tn), lambda i, j: (i, 0, j))],
        out_specs=pl.BlockSpec((1, 8, tn), lambda i, j: (i, 0, j)),
        out_shape=jax.ShapeDtypeStruct((depth, 8, n), F32),
        compiler_params=_cparams(("arbitrary", "arbitrary")),
        name="modulation",
    )(cs, mod_w, mod_b.reshape(depth, 1, n))


def _mod_rows(mod_ref, is_ctx, idx, d):
    sl = slice(idx * d, (idx + 1) * d)
    return jnp.where(is_ctx, mod_ref[1:2, sl], mod_ref[0:1, sl])


def _norm_mod(x, nw, shift, scale):
    ms = jnp.mean(x * x, axis=-1, keepdims=True)
    return (x * lax.rsqrt(ms + EPS) * nw) * (1.0 + scale) + shift


def _ctx_lat_specs(n_ctx_tiles, d):
    ctx = pl.BlockSpec((ROW_TILE, d), lambda i: (jnp.minimum(i, n_ctx_tiles - 1), 0))
    lat = pl.BlockSpec((ROW_TILE, d), lambda i: (jnp.maximum(i - n_ctx_tiles, 0), 0))
    return ctx, lat


def _proj_kernel(xc_ref, xl_ref, nw_ref, mod_ref, w_ref, *out_refs, widths, n_ctx_tiles):
    d = xl_ref.shape[1]
    is_ctx = pl.program_id(0) < n_ctx_tiles
    x = jnp.where(is_ctx, xc_ref[...], xl_ref[...])
    h = _norm_mod(x, nw_ref[...], _mod_rows(mod_ref, is_ctx, 0, d), _mod_rows(mod_ref, is_ctx, 1, d))
    h = h.astype(BF16)
    off = 0
    for o_ref, wd in zip(out_refs, widths):
        o_ref[...] = jnp.dot(h, w_ref[:, off:off + wd], preferred_element_type=F32).astype(o_ref.dtype)
        off += wd


def _norm_proj(x_ctx, x_lat, nw, mod, w, widths, dtypes):
    d = x_lat.shape[1]
    n_ctx_tiles = x_ctx.shape[0] // ROW_TILE
    ta = x_ctx.shape[0] + x_lat.shape[0]
    ntot = sum(widths)
    kern = functools.partial(_proj_kernel, widths=widths, n_ctx_tiles=n_ctx_tiles)
    return pl.pallas_call(
        kern,
        grid=(ta // ROW_TILE,),
        in_specs=[*_ctx_lat_specs(n_ctx_tiles, d),
                  pl.BlockSpec((1, d), lambda i: (0, 0)),
                  pl.BlockSpec(mod.shape, lambda i: (0, 0)),
                  pl.BlockSpec((d, ntot), lambda i: (0, 0))],
        out_specs=[pl.BlockSpec((ROW_TILE, wd), lambda i: (i, 0)) for wd in widths],
        out_shape=[jax.ShapeDtypeStruct((ta, wd), dt) for wd, dt in zip(widths, dtypes)],
        compiler_params=_cparams(("arbitrary",)),
        name="norm_proj",
    )(x_ctx, x_lat, nw.reshape(1, d), mod, w)


def _halo_specs(width, n_tiles):
    per = ROW_TILE // HALO
    prev = pl.BlockSpec((HALO, width), lambda i: (jnp.maximum(i * per - 1, 0), 0))
    nxt = pl.BlockSpec((HALO, width), lambda i: (jnp.minimum((i + 1) * per, n_tiles * per - 1), 0))
    return prev, nxt


CONV_BIAS_ROW = CONV_W


def _conv_weights(conv_w, conv_b):
    cdim = conv_w.shape[1]
    w = jnp.zeros((8, cdim), F32).at[:CONV_W].set(conv_w)
    if conv_b is not None:
        w = w.at[CONV_BIAS_ROW].set(conv_b)
    return w.reshape(8, cdim // LANES, LANES).transpose(1, 0, 2)


def _conv_silu(cur_ref, prev_ref, next_ref, w_ref, ext_ref, out_ref, n_tiles, n_iter, slabs):
    i = pl.program_id(0)
    prev_ok = jnp.logical_and(i != 0, i != 1)
    next_ok = jnp.logical_and(i != 0, i != n_tiles - 1)
    for c in range(ext_ref.shape[0]):
        lanes = slice(c * LANES, (c + 1) * LANES)
        ext_ref[c, 0:HALO, :] = jnp.where(prev_ok, prev_ref[:, lanes], 0.0)
        ext_ref[c, HALO:HALO + ROW_TILE, :] = cur_ref[:, lanes]
        ext_ref[c, HALO + ROW_TILE:, :] = jnp.where(next_ok, next_ref[:, lanes], 0.0)
    pad = CONV_W // 2
    stride = 8
    starts = [base + r for base in range(0, ROW_TILE, 8 * stride) for r in range(stride)]

    def iteration(i, carry):
        for slab_of, post in slabs:
            c = slab_of(i)
            w = w_ref[c]
            accs = [w[CONV_BIAS_ROW:CONV_BIAS_ROW + 1, :] for _ in starts]
            for j in range(CONV_W):
                taps = [ext_ref[c, pl.ds(HALO - pad + j + s, 8, stride=stride), :] for s in starts]
                accs = [acc + tap * w[j:j + 1, :] for acc, tap in zip(accs, taps)]
            ys = post([_silu(acc) for acc in accs])
            for s, y in zip(starts, ys):
                out_ref[c, pl.ds(s, 8, stride=stride), :] = y
        return carry

    lax.fori_loop(0, n_iter, iteration, 0)


def _even_prep_kernel(xbc_ref, prev_ref, next_ref, small_ref, cw_ref, dtb_ref, aneg_ref,
                      gw_ref, gb_ref, xs_ref, bm_ref, cm_ref, dt_ref, da_ref, gkf_ref, gkb_ref,
                      ext_ref, conv_ref, *, n_tiles):
    n_slabs = ext_ref.shape[0]
    _conv_silu(xbc_ref, prev_ref, next_ref, cw_ref, ext_ref, conv_ref, n_tiles, n_slabs, [(lambda i: i, lambda ys: ys)])
    slab = 0
    for o_ref in (xs_ref, bm_ref, cm_ref):
        for c in range(o_ref.shape[1] // LANES):
            o_ref[:, c * LANES:(c + 1) * LANES] = conv_ref[slab]
            slab += 1
    small = small_ref[...]
    dt = _softplus(small + dtb_ref[...])
    dt_ref[...] = dt
    da_ref[...] = dt * aneg_ref[...]
    s_hi = small.astype(BF16)
    s_lo = (small - s_hi.astype(F32)).astype(BF16)
    lhs = jnp.concatenate([s_hi, s_lo, s_hi], axis=1)
    for d, o_ref in enumerate((gkf_ref, gkb_ref)):
        pre = jnp.dot(lhs, gw_ref[d], preferred_element_type=F32) + gb_ref[d]
        o_ref[...] = -_softplus(-pre) * (1.0 / GLA_GATE_NORM)


def _even_prep(xbc, small, conv_w, conv_b, dt_bias, a_log, gate_w, gate_b):
    ta, cdim = xbc.shape
    n_tiles = ta // ROW_TILE
    inner = SSD_HEADS * SSD_HEAD_DIM
    gs = SSD_GROUPS * SSD_STATE
    gkey = GLA_HEADS * GLA_DK
    cw = _conv_weights(conv_w, conv_b)
    dtb = jnp.zeros((1, LANES), F32).at[0, :2 * SMALL].set(dt_bias.reshape(-1))
    aneg = jnp.zeros((1, LANES), F32).at[0, :2 * SMALL].set(-jnp.exp(a_log.reshape(-1)))
    gw = jnp.zeros((2, LANES, gkey), F32)
    gw = gw.at[0, 2 * SMALL:3 * SMALL].set(gate_w[0]).at[1, 3 * SMALL:4 * SMALL].set(gate_w[1])
    gw_hi = gw.astype(BF16)
    gw_lo = (gw - gw_hi.astype(F32)).astype(BF16)
    gw = jnp.concatenate([gw_hi, gw_hi, gw_lo], axis=1)
    prev_spec, next_spec = _halo_specs(cdim, n_tiles)
    row = lambda w: pl.BlockSpec((ROW_TILE, w), lambda i: (i, 0))
    full = lambda a: pl.BlockSpec(a.shape, lambda i: (0,) * a.ndim)
    gb = gate_b.reshape(2, 1, gkey)
    return pl.pallas_call(
        functools.partial(_even_prep_kernel, n_tiles=n_tiles),
        grid=(n_tiles,),
        in_specs=[row(cdim), prev_spec, next_spec, row(LANES), full(cw), full(dtb), full(aneg), full(gw), full(gb)],
        out_specs=[row(inner), row(gs), row(gs), row(LANES), row(LANES), row(gkey), row(gkey)],
        out_shape=[jax.ShapeDtypeStruct((ta, w), F32) for w in (inner, gs, gs, LANES, LANES, gkey, gkey)],
        scratch_shapes=[pltpu.VMEM((cdim // LANES, ROW_TILE + 2 * HALO, LANES), F32),
                        pltpu.VMEM((cdim // LANES, ROW_TILE, LANES), F32)],
        compiler_params=_cparams(("arbitrary",)),
        name="even_prep",
    )(xbc, xbc, xbc, small, cw, dtb, aneg, gw, gb)


def _scan_consts(rev, heads, lanes_per_head):
    r = jnp.arange(CHUNK)
    tri = (r[:, None] <= r[None, :]) if rev else (r[:, None] >= r[None, :])
    strict = (r[:, None] < r[None, :]) if rev else (r[:, None] > r[None, :])
    eye = r[:, None] == r[None, :]
    expand = (jnp.arange(heads)[:, None] == (jnp.arange(heads * lanes_per_head)[None, :] // lanes_per_head))
    return dict(
        tri3=jnp.tile(tri, (1, 3)).astype(BF16),
        expand3=jnp.tile(expand, (3, 1)).astype(BF16),
        tri_t=jnp.tile(tri, (1, heads)).astype(F32),
        strict_t=jnp.tile(strict, (1, heads)).astype(F32),
        eye_t=jnp.tile(eye, (1, heads)).astype(F32),
    )


def _pair_mask():
    r = jnp.arange(2 * CHUNK)
    return ((r[:, None] // CHUNK) == (r[None, :] // CHUNK)).astype(F32)


def _level_masks():
    r = jnp.arange(2 * CHUNK)
    same = lambda b: (r[:, None] // b) == (r[None, :] // b)
    sizes = [2 ** m for m in range(CHUNK.bit_length() - 1)]
    return jnp.stack([jnp.logical_and(same(2 * b), jnp.logical_not(same(b))) for b in sizes]).astype(F32)


def _chunk_map(rev, n_chunks, n_ctx_chunks):
    if not rev:
        return lambda i: i
    return lambda i: jnp.where(i < n_ctx_chunks, n_ctx_chunks - 1 - i, n_chunks + n_ctx_chunks - 1 - i)


def _decay_tables(tri3s, expand3, eye_t, tri_ts, gs):
    cums = [_sel_dot_l(tri3, g) * LOG2_E for tri3, g in zip(tri3s, gs)]
    cols = [_sel_dot_r(cum, expand3) for cum in cums]
    rowfs = [jnp.sum(col * eye_t, axis=0, keepdims=True) for col in cols]
    decays = [jnp.where(tri_t > 0.5, jnp.exp2(col - rowf), 0.0) for tri_t, col, rowf in zip(tri_ts, cols, rowfs)]
    return cols, decays


SSD_SCAN_INS = 5


def _ssd_kernel(*refs, cps):
    ins = refs[:2 * SSD_SCAN_INS]
    tri_ref, expand_ref, eye_ref, trit_ref, pair_ref, yf_ref, yb_ref, st_ref = refs[2 * SSD_SCAN_INS:]

    @pl.when(pl.program_id(0) == 0)
    def _():
        st_ref[...] = jnp.zeros(st_ref.shape, F32)

    pair = pair_ref[...]
    heads_g = SSD_HEADS // SSD_GROUPS
    gw = heads_g * SSD_HEAD_DIM
    pieces = [(d, c, g) for d in range(2) for c in range(cps) for g in range(SSD_GROUPS)]
    gsl = lambda g: slice(g * SSD_STATE, (g + 1) * SSD_STATE)
    lanes = lambda g: slice(g * gw, (g + 1) * gw)
    rows = lambda d, c: slice((cps - 1 - c if d else c) * CHUNK, (cps - c if d else c + 1) * CHUNK)
    chunks = [(d, c) for d in range(2) for c in range(cps)]
    small = lambda d, c, k: ins[SSD_SCAN_INS * d + k][rows(d, c), d * SMALL:(d + 1) * SMALL]
    cols, decays = _decay_tables([tri_ref[d] for d, _ in chunks], expand_ref[...], eye_ref[...],
                                 [trit_ref[d] for d, _ in chunks], [small(d, c, 4) for d, c in chunks])
    dtes = [_sel_dot_r(small(d, c, 3), expand_ref[...]) for d, c in chunks]
    tabs = {}
    for (d, c), col, decay, dte in zip(chunks, cols, decays, dtes):
        last_row = 0 if d else CHUNK - 1
        xdt = ins[SSD_SCAN_INS * d][rows(d, c), :] * dte
        last = col[last_row:last_row + 1, :]
        tabs[d, c] = dict(decay=decay, xdt=xdt, xw=xdt * jnp.exp2(last - col), chunk_decay=jnp.exp2(last),
                          eac=jnp.exp2(col))
    bgs = {(d, c, g): ins[SSD_SCAN_INS * d + 1][rows(d, c), gsl(g)].astype(BF16) for d, c, g in pieces}
    cgs = {(d, c, g): ins[SSD_SCAN_INS * d + 2][rows(d, c), gsl(g)].astype(BF16) for d, c, g in pieces}
    wms = {p: lax.dot_general(cgs[p], jnp.concatenate([bgs[p]] * heads_g, axis=0), NT_DIMS,
                              preferred_element_type=F32) * tabs[p[:2]]["decay"][:, lanes(p[2])] for p in pieces}
    news = {(d, c, g): lax.dot_general(bgs[d, c, g], tabs[d, c]["xw"][:, lanes(g)].astype(BF16), TN_DIMS,
                                       preferred_element_type=F32) for d, c, g in pieces}
    diags = {}
    for d, c, g in pieces:
        xg = tabs[d, c]["xdt"][:, lanes(g)]
        for j in range(gw // LANES):
            xp = xg[:, j * LANES:(j + 1) * LANES]
            blockdiag = jnp.concatenate([xp, xp], axis=0) * pair
            diags[d, c, g, j] = _bdot(wms[d, c, g][:, j * LANES:(j + 1) * LANES], blockdiag)
    states = {}
    for d in range(2):
        for g in range(SSD_GROUPS):
            state = st_ref[d, g]
            for c in range(cps):
                states[d, c, g] = state
                state = state * tabs[d, c]["chunk_decay"][:, lanes(g)] + news[d, c, g]
            st_ref[d, g] = state
    offs = {p: jnp.dot(cgs[p], states[p].astype(BF16), preferred_element_type=F32) for p in pieces}
    for d, y_ref in enumerate((yf_ref, yb_ref)):
        for c in range(cps):
            for g in range(SSD_GROUPS):
                diag = jnp.concatenate([diags[d, c, g, j] for j in range(gw // LANES)], axis=1)
                y_ref[rows(d, c), lanes(g)] = (diag + offs[d, c, g] * tabs[d, c]["eac"][:, lanes(g)]).astype(y_ref.dtype)


def _ssd_scan(xs, bm, cm, dt, da, n_ctx_chunks):
    ta, inner = xs.shape
    cps = SCAN_CPS
    nb = ta // (cps * CHUNK)
    fwd = _scan_consts(False, SSD_HEADS, SSD_HEAD_DIM)
    bwd = _scan_consts(True, SSD_HEADS, SSD_HEAD_DIM)
    both = lambda name: jnp.stack([fwd[name], bwd[name]])
    consts = [both("tri3"), fwd["expand3"], fwd["eye_t"], both("tri_t"), _pair_mask()]
    full = lambda a: pl.BlockSpec(a.shape, lambda i: (0,) * a.ndim)
    gs = SSD_GROUPS * SSD_STATE

    def row(rev, w):
        cmap = _chunk_map(rev, nb, n_ctx_chunks // cps)
        return pl.BlockSpec((cps * CHUNK, w), lambda i: (cmap(i), 0))

    dir_specs = lambda rev: [row(rev, inner), row(rev, gs), row(rev, gs), row(rev, LANES), row(rev, LANES)]
    return pl.pallas_call(
        functools.partial(_ssd_kernel, cps=cps),
        grid=(nb,),
        in_specs=dir_specs(False) + dir_specs(True) + [full(a) for a in consts],
        out_specs=[row(False, inner), row(True, inner)],
        out_shape=[jax.ShapeDtypeStruct((ta, inner), BF16)] * 2,
        scratch_shapes=[pltpu.VMEM((2, SSD_GROUPS, SSD_STATE, inner // SSD_GROUPS), F32)],
        compiler_params=_cparams(("arbitrary",)),
        name="ssd_scan",
    )(xs, bm, cm, dt, da, xs, bm, cm, dt, da, *consts)


GLA_SUB = 8
GLA_LEVELS = (8, 16, 32)
GLA_SCAN_INS = 4


def _gla_masks():
    r = jnp.arange(CHUNK)
    same = lambda b: (r[:, None] // b) == (r[None, :] // b)
    out = []
    for rev in (False, True):
        strict = (r[:, None] < r[None, :]) if rev else (r[:, None] > r[None, :])
        lv = [same(2 * b) & ~same(b) & strict for b in GLA_LEVELS]
        within = r % GLA_SUB
        dg = []
        for s in range(GLA_SUB):
            seen = (within <= s) if rev else (within >= s)
            dg.append((r[None, :] == (r[:, None] // GLA_SUB) * GLA_SUB + s) & seen[:, None])
        out.append(jnp.stack(lv + dg))
    return jnp.stack(out).astype(F32)


def _block_rows(a, rows, b):
    return jnp.concatenate([jnp.broadcast_to(a[r:r + 1, :], (b, a.shape[1])) for r in rows], axis=0)


def _gla_kernel(*refs, cps):
    ins = refs[:2 * GLA_SCAN_INS]
    tri_ref, mask_ref, of_ref, ob_ref, st_ref = refs[2 * GLA_SCAN_INS:]

    @pl.when(pl.program_id(0) == 0)
    def _():
        st_ref[...] = jnp.zeros(st_ref.shape, F32)

    pieces = [(d, c, h) for d in range(2) for c in range(cps) for h in range(GLA_HEADS)]
    ksl = lambda h: slice(h * GLA_DK, (h + 1) * GLA_DK)
    vsl = lambda h: slice(h * GLA_DV, (h + 1) * GLA_DV)
    rws = lambda d, c: slice((cps - 1 - c if d else c) * CHUNK, (cps - c if d else c + 1) * CHUNK)
    q_of = lambda d, c, h: ins[GLA_SCAN_INS * d][rws(d, c), ksl(h)] * (GLA_DK ** -0.5)
    k_of = lambda d, c, h: ins[GLA_SCAN_INS * d + 1][rws(d, c), ksl(h)]
    v_of = lambda d, c, h: ins[GLA_SCAN_INS * d + 2][rws(d, c), vsl(h)]
    within = lax.broadcasted_iota(jnp.int32, (CHUNK, 1), 0) % GLA_SUB
    sel_rows = lax.broadcasted_iota(jnp.int32, (8, GLA_DV), 0)
    gcs = {(d, c, h): _sel_dot_l(tri_ref[d], ins[GLA_SCAN_INS * d + 3][rws(d, c), ksl(h)]) * LOG2_E
           for d, c, h in pieces}
    news = {}
    for d, c, h in pieces:
        gc = gcs[d, c, h]
        last = 0 if d else CHUNK - 1
        glast = gc[last:last + 1, :]
        e_rows = jnp.concatenate([p.astype(F32) for p in _split3(jnp.exp2(glast))]
                                 + [jnp.zeros((8 - 3, GLA_DK), F32)], axis=0)
        e_col = _bdot_tn(e_rows, (sel_rows < 3).astype(F32))
        news[d, c, h] = (e_col, _bdot_tn(k_of(d, c, h) * jnp.exp2(glast - gc), v_of(d, c, h)))
    states = {}
    for d in range(2):
        for h in range(GLA_HEADS):
            state = st_ref[d, h]
            for c in range(cps):
                states[d, c, h] = state
                state = state * news[d, c, h][0] + news[d, c, h][1]
            st_ref[d, h] = state
    inters = {p: _bdot(q_of(*p) * jnp.exp2(gcs[p]), states[p]) for p in pieces}
    atts = {}
    for d, c, h in pieces:
        gc, qh, kh = gcs[d, c, h], q_of(d, c, h), k_of(d, c, h)
        att = None
        for li, b in enumerate(GLA_LEVELS):
            nb = CHUNK // b
            if d:
                rq = [min(b * (i + 1), CHUNK - 1) for i in range(nb)]
                rk = [b * i for i in range(nb)]
            else:
                rq = [max(b * i - 1, 0) for i in range(nb)]
                rk = [b * (i + 1) - 1 for i in range(nb)]
            qs = qh * jnp.exp2(gc - _block_rows(gc, rq, b))
            ks = kh * jnp.exp2(_block_rows(gc, rk, b) - gc)
            part = _bdot_nt(qs, ks) * mask_ref[d, li]
            att = part if att is None else att + part
        for s in range(GLA_SUB):
            rows = [GLA_SUB * i + s for i in range(CHUNK // GLA_SUB)]
            seen = (within <= s) if d else (within >= s)
            e = jnp.where(seen, jnp.exp2(gc - _block_rows(gc, rows, GLA_SUB)), 0.0)
            a = jnp.sum(qh * _block_rows(kh, rows, GLA_SUB) * e, axis=-1, keepdims=True)
            att = att + a * mask_ref[d, len(GLA_LEVELS) + s]
        atts[d, c, h] = att
    for d, o_ref in enumerate((of_ref, ob_ref)):
        for c in range(cps):
            for h in range(GLA_HEADS):
                o_ref[rws(d, c), vsl(h)] = (inters[d, c, h] + _bdot(atts[d, c, h], v_of(d, c, h))).astype(o_ref.dtype)


def _gla_scan(q, k, v, gk_f, gk_b, n_ctx_chunks):
    ta, kdim = q.shape
    vdim = v.shape[1]
    cps = GLA_CPS
    nb = ta // (cps * CHUNK)
    tri3 = jnp.stack([_scan_consts(rev, 1, 1)["tri3"] for rev in (False, True)])
    masks = _gla_masks()
    full = lambda a: pl.BlockSpec(a.shape, lambda i: (0,) * a.ndim)

    def row(rev, w):
        cmap = _chunk_map(rev, nb, n_ctx_chunks // cps)
        return pl.BlockSpec((cps * CHUNK, w), lambda i: (cmap(i), 0))

    dir_specs = lambda rev: [row(rev, kdim), row(rev, kdim), row(rev, vdim), row(rev, kdim)]
    return pl.pallas_call(
        functools.partial(_gla_kernel, cps=cps),
        grid=(nb,),
        in_specs=dir_specs(False) + dir_specs(True) + [full(tri3), full(masks)],
        out_specs=[row(False, vdim), row(True, vdim)],
        out_shape=[jax.ShapeDtypeStruct((ta, vdim), BF16)] * 2,
        scratch_shapes=[pltpu.VMEM((2, GLA_HEADS, GLA_DK, GLA_DV), F32)],
        compiler_params=_cparams(("arbitrary",)),
        name="gla_scan",
    )(q, k, v, gk_f, q, k, v, gk_b, tri3, masks)


def _group_rms(y, w, width):
    parts = []
    for g in range(y.shape[1] // width):
        yg = y[:, g * width:(g + 1) * width]
        ms = jnp.mean(yg * yg, axis=-1, keepdims=True)
        parts.append(yg * lax.rsqrt(ms + EPS))
    return jnp.concatenate(parts, axis=1) * w


def _even_merge_kernel(yf_ref, yb_ref, xs_ref, z_ref, of_ref, ob_ref, r_ref, dsk_ref, snw_ref, gnw_ref, w_ref, o_ref):
    inner = xs_ref.shape[1]
    f32 = lambda ref: ref[...].astype(F32)
    y = (f32(yf_ref) + f32(yb_ref) + dsk_ref[...] * xs_ref[...]) * _silu(f32(z_ref))
    y = _group_rms(y, snw_ref[...], inner // SSD_GROUPS)
    o = _group_rms(f32(of_ref) + f32(ob_ref), gnw_ref[...], GLA_DV) * _silu(f32(r_ref))
    o_ref[...] = _bdot(y, w_ref[:inner, :]) + _bdot(o, w_ref[inner:, :])


def _even_merge(yf, yb, xs, z, of, ob, r, d_skip, ssd_nw, gla_nw, w_out):
    ta, inner = xs.shape
    vdim = of.shape[1]
    d = w_out.shape[1]
    dsk = jnp.repeat(d_skip, SSD_HEAD_DIM).reshape(1, inner)
    gnw = jnp.tile(gla_nw, GLA_HEADS).reshape(1, vdim)
    row = lambda w: pl.BlockSpec((ROW_TILE, w), lambda i: (i, 0))
    vec = lambda w: pl.BlockSpec((1, w), lambda i: (0, 0))
    return pl.pallas_call(
        _even_merge_kernel,
        grid=(ta // ROW_TILE,),
        in_specs=[row(inner)] * 4 + [row(vdim)] * 3 + [vec(inner), vec(inner), vec(vdim),
                                                        pl.BlockSpec(w_out.shape, lambda i: (0, 0))],
        out_specs=row(d),
        out_shape=jax.ShapeDtypeStruct((ta, d), F32),
        compiler_params=_cparams(("arbitrary",)),
        name="even_merge",
    )(yf, yb, xs, z, of, ob, r, dsk, ssd_nw.reshape(1, inner), gnw, w_out)


def _odd_prep_kernel(qkv_ref, prev_ref, next_ref, small_ref, cw_ref, dtb_ref, aneg_ref,
                     q_ref, k_ref, v_ref, g_ref, beta_ref, ext_ref, conv_ref, *, n_tiles):
    assert GDN_DK == LANES
    kslabs = q_ref.shape[1] // LANES
    n_slabs = ext_ref.shape[0]
    def l2n(scale):
        def post(groups):
            sums = [jnp.sum(a * a, axis=-1, keepdims=True) for a in groups]
            return [a * (lax.rsqrt(s + EPS) * scale) for a, s in zip(groups, sums)]
        return post

    v_per_iter = (n_slabs - 2 * kslabs) // kslabs
    v_slabs = [(lambda i, m=m: 2 * kslabs + v_per_iter * i + m, lambda ys: ys) for m in range(v_per_iter)]
    _conv_silu(qkv_ref, prev_ref, next_ref, cw_ref, ext_ref, conv_ref, n_tiles, kslabs,
               [(lambda i: i, l2n(GDN_DK ** -0.5)), (lambda i: kslabs + i, l2n(1.0))] + v_slabs)
    slab = 0
    for o_ref in (q_ref, k_ref, v_ref):
        for c in range(o_ref.shape[1] // LANES):
            o_ref[:, c * LANES:(c + 1) * LANES] = conv_ref[slab]
            slab += 1
    small = small_ref[...]
    g_ref[...] = _softplus(small + dtb_ref[...]) * aneg_ref[...]
    beta_ref[...] = jax.nn.sigmoid(small)


def _odd_prep(qkv, small, conv_w, dt_bias, a_log):
    ta, cdim = qkv.shape
    n_tiles = ta // ROW_TILE
    kdim = GDN_QK_HEADS * GDN_DK
    vdim = GDN_V_HEADS * GDN_DV
    cw = _conv_weights(conv_w, None)
    dtb = jnp.zeros((1, LANES), F32).at[0, :2 * SMALL].set(dt_bias.reshape(-1))
    aneg = jnp.zeros((1, LANES), F32).at[0, :2 * SMALL].set(-jnp.exp(a_log.reshape(-1)))
    prev_spec, next_spec = _halo_specs(cdim, n_tiles)
    row = lambda w: pl.BlockSpec((ROW_TILE, w), lambda i: (i, 0))
    full = lambda a: pl.BlockSpec(a.shape, lambda i: (0,) * a.ndim)
    return pl.pallas_call(
        functools.partial(_odd_prep_kernel, n_tiles=n_tiles),
        grid=(n_tiles,),
        in_specs=[row(cdim), prev_spec, next_spec, row(LANES), full(cw), full(dtb), full(aneg)],
        out_specs=[row(kdim), row(kdim), row(vdim), row(LANES), row(LANES)],
        out_shape=[jax.ShapeDtypeStruct((ta, w), F32) for w in (kdim, kdim, vdim, LANES, LANES)],
        scratch_shapes=[pltpu.VMEM((cdim // LANES, ROW_TILE + 2 * HALO, LANES), F32),
                        pltpu.VMEM((cdim // LANES, ROW_TILE, LANES), F32)],
        compiler_params=_cparams(("arbitrary",)),
        name="odd_prep",
    )(qkv, qkv, qkv, small, cw, dtb, aneg)


GDN_SCAN_INS = 5


def _interleave(phases):
    phases = list(phases)
    while phases:
        for ph in list(phases):
            if next(ph, StopIteration) is StopIteration:
                phases.remove(ph)


def _gdn_kernel(*refs, cps):
    ins = refs[:2 * GDN_SCAN_INS]
    (tri_ref, expand_ref, eye_ref, trit_ref, strict_ref, pair_ref, lvl_ref, eye2_ref,
     of_ref, ob_ref, st_ref) = refs[2 * GDN_SCAN_INS:]

    @pl.when(pl.program_id(0) == 0)
    def _():
        st_ref[...] = jnp.zeros(st_ref.shape, F32)

    pair = pair_ref[...]
    eye2 = eye2_ref[...]
    left = lax.broadcasted_iota(jnp.int32, (CHUNK, 2 * CHUNK), 1) < CHUNK
    n_levels = lvl_ref.shape[0]
    dirs = range(2)
    o_refs = (of_ref, ob_ref)
    dot16 = lambda x, y: jnp.dot(x, y, preferred_element_type=F32)
    rows = lambda d, c: slice((cps - 1 - c if d else c) * CHUNK, (cps - c if d else c + 1) * CHUNK)
    pairs = [(d, j) for d in dirs for j in range(GDN_QK_HEADS)]
    chains = [(d, h) for d in dirs for h in range(GDN_V_HEADS)]

    keys = [(c, d) for c in range(cps) for d in dirs]
    cols, decays = _decay_tables([tri_ref[d] for _, d in keys], expand_ref[...], eye_ref[...],
                                 [trit_ref[d] for _, d in keys],
                                 [ins[GDN_SCAN_INS * d + 3][rows(d, c), d * SMALL:(d + 1) * SMALL] for c, d in keys])
    beta_ts = [_sel_dot_r(ins[GDN_SCAN_INS * d + 4][rows(d, c), (2 + d) * SMALL:(3 + d) * SMALL], expand_ref[...])
               for c, d in keys]
    tabs = {k: t for k, t in zip(keys, zip(cols, decays, beta_ts))}

    def build(c, out):
        for d in dirs:
            q_ref, k_ref, v_ref = ins[GDN_SCAN_INS * d:GDN_SCAN_INS * d + 3]
            rw = rows(d, c)
            last_row = 0 if d else CHUNK - 1
            col, decay, beta_t = tabs[c, d]
            a_all = beta_t * decay * strict_ref[d]
            for j in range(GDN_QK_HEADS):
                ksl = slice(j * GDN_DK, (j + 1) * GDN_DK)
                psl = slice(j * 2 * CHUNK, (j + 1) * 2 * CHUNK)
                kj = k_ref[rw, ksl]
                kb = kj.astype(BF16)
                krep = jnp.concatenate([kb, kb], axis=0)
                kk = lax.dot_general(kb, krep, NT_DIMS, preferred_element_type=F32)
                qk = lax.dot_general(q_ref[rw, ksl].astype(BF16), krep, NT_DIMS, preferred_element_type=F32)
                out["att"][d, j] = (qk * decay[:, psl]).astype(BF16)
                a_pair = kk * a_all[:, psl]
                a_bd = jnp.concatenate([a_pair, a_pair], axis=0) * pair
                out["tinv"][d, j] = (eye2 - a_bd * lvl_ref[0]).astype(BF16)
                out["a16"][d, j] = a_bd.astype(BF16)
                colp, betap = col[:, psl], beta_t[:, psl]
                colr, betar = pltpu.roll(colp, CHUNK, 1), pltpu.roll(betap, CHUNK, 1)
                rhs = []
                for hh in range(2):
                    h = 2 * j + hh
                    hsl = slice(h * GDN_DV, (h + 1) * GDN_DV)
                    cumx = jnp.where(left, colp, colr) if hh == 0 else jnp.where(left, colr, colp)
                    betax = jnp.where(left, betap, betar) if hh == 0 else jnp.where(left, betar, betap)
                    lastx = cumx[last_row:last_row + 1, :]
                    egx = jnp.exp2(cumx)
                    out["qd"][d, h] = (q_ref[rw, ksl] * egx).astype(BF16)
                    out["kd"][d, h] = (kj * jnp.exp2(lastx - cumx)).astype(BF16)
                    out["gl"][d, h] = jnp.exp2(lastx)
                    rhs.append(jnp.concatenate([v_ref[rw, hsl] * betax, kj * betax * egx], axis=1))
                out["rhs"][d, j] = jnp.concatenate(rhs, axis=0).astype(BF16)
                yield

    def invert(out):
        tinv, a16 = out["tinv"], out["a16"]
        for lv in range(1, n_levels):
            b = 2 ** lv
            mask = lvl_ref[lv].astype(BF16)
            if b < 16:
                tls = {p: dot16(tinv[p], a16[p] * mask).astype(BF16) for p in pairs}
                yield
                for p in pairs:
                    tinv[p] = tinv[p] - dot16(tls[p], tinv[p]).astype(BF16)
                yield
                continue
            blocks = lambda t: [t[i * b:(i + 1) * b] for i in range(t.shape[0] // b)]
            moving = lambda d, t: jnp.concatenate(blocks(t)[(1 - d)::2], axis=0)
            tls = {p: dot16(moving(p[0], tinv[p]), a16[p] * mask).astype(BF16) for p in pairs}
            yield
            for p in pairs:
                new = blocks(moving(p[0], tinv[p]) - dot16(tls[p], tinv[p]).astype(BF16))
                tinv[p] = jnp.concatenate([new[i // 2] if i % 2 == 1 - p[0] else blk
                                           for i, blk in enumerate(blocks(tinv[p]))], axis=0)
            yield
        out["x"] = {p: dot16(tinv[p], out["rhs"][p]) for p in pairs}
        yield

    def recur(c, out, states):
        x_of = lambda d, h: out["x"][d, h // 2][(h % 2) * CHUNK:(h % 2 + 1) * CHUNK]
        ws_qs = {(d, h): dot16(jnp.concatenate([x_of(d, h)[:, GDN_DV:].astype(BF16), out["qd"][d, h]], axis=0),
                               states[d, h].astype(BF16)) for d, h in chains}
        yield
        v_new = {(d, h): (x_of(d, h)[:, :GDN_DV] - ws_qs[d, h][:CHUNK]).astype(BF16) for d, h in chains}
        for d, h in chains:
            att = out["att"][d, h // 2][:, (h % 2) * CHUNK:(h % 2 + 1) * CHUNK]
            o_refs[d][rows(d, c), h * GDN_DV:(h + 1) * GDN_DV] = (
                ws_qs[d, h][CHUNK:] + dot16(att, v_new[d, h])).astype(o_refs[d].dtype)
        yield
        for d, h in chains:
            states[d, h] = states[d, h] * out["gl"][d, h] + lax.dot_general(
                out["kd"][d, h], v_new[d, h], TN_DIMS, preferred_element_type=F32)
        yield

    fields = ("att", "tinv", "a16", "rhs", "qd", "kd", "gl")
    outs = [{f: {} for f in fields} for _ in range(cps)]
    states = {(d, h): st_ref[d, h] for d, h in chains}
    _interleave([build(0, outs[0])])
    for c in range(cps):
        phases = [invert(outs[c])]
        if c + 1 < cps:
            phases.append(build(c + 1, outs[c + 1]))
        if c > 0:
            phases.append(recur(c - 1, outs[c - 1], states))
        _interleave(phases)
    _interleave([recur(cps - 1, outs[cps - 1], states)])
    for d, h in chains:
        st_ref[d, h] = states[d, h]


def _gdn_scan(q, k, v, g, beta, n_ctx_chunks):
    ta, kdim = q.shape
    vdim = v.shape[1]
    cps = GDN_CPS
    nb = ta // (cps * CHUNK)
    fwd = _scan_consts(False, GDN_V_HEADS, CHUNK)
    bwd = _scan_consts(True, GDN_V_HEADS, CHUNK)
    both = lambda name: jnp.stack([fwd[name], bwd[name]])
    consts = [both("tri3"), fwd["expand3"], fwd["eye_t"], both("tri_t"), both("strict_t"),
              _pair_mask(), _level_masks(), jnp.eye(2 * CHUNK, dtype=F32)]
    full = lambda a: pl.BlockSpec(a.shape, lambda i: (0,) * a.ndim)

    def row(rev, w):
        cmap = _chunk_map(rev, nb, n_ctx_chunks // cps)
        return pl.BlockSpec((cps * CHUNK, w), lambda i: (cmap(i), 0))

    dir_specs = lambda rev: [row(rev, kdim), row(rev, kdim), row(rev, vdim), row(rev, LANES), row(rev, LANES)]
    return pl.pallas_call(
        functools.partial(_gdn_kernel, cps=cps),
        grid=(nb,),
        in_specs=dir_specs(False) + dir_specs(True) + [full(a) for a in consts],
        out_specs=[row(False, vdim), row(True, vdim)],
        out_shape=[jax.ShapeDtypeStruct((ta, vdim), BF16)] * 2,
        scratch_shapes=[pltpu.VMEM((2, GDN_V_HEADS, GDN_DK, GDN_DV), F32)],
        compiler_params=_cparams(("arbitrary",)),
        name="gdn_scan",
    )(q, k, v, g, beta, q, k, v, g, beta, *consts)


def _odd_merge_kernel(of_ref, ob_ref, z_ref, nw_ref, w_ref, o_ref):
    f32 = lambda ref: ref[...].astype(F32)
    o = _group_rms(f32(of_ref) + f32(ob_ref), nw_ref[...], GDN_DV) * _silu(f32(z_ref))
    o_ref[...] = _bdot(o, w_ref[...])


def _odd_merge(of, ob, z, gdn_nw, w_out, n_ctx_tiles):
    ta, vdim = of.shape
    d = w_out.shape[1]
    n_lat = ta - n_ctx_tiles * ROW_TILE
    nw = jnp.tile(gdn_nw, GDN_V_HEADS).reshape(1, vdim)
    row_in = pl.BlockSpec((ROW_TILE, vdim), lambda i: (i + n_ctx_tiles, 0))
    return pl.pallas_call(
        _odd_merge_kernel,
        grid=(n_lat // ROW_TILE,),
        in_specs=[row_in] * 3 + [pl.BlockSpec((1, vdim), lambda i: (0, 0)), pl.BlockSpec(w_out.shape, lambda i: (0, 0))],
        out_specs=pl.BlockSpec((ROW_TILE, d), lambda i: (i, 0)),
        out_shape=jax.ShapeDtypeStruct((n_lat, d), F32),
        compiler_params=_cparams(("arbitrary",)),
        name="odd_merge",
    )(of, ob, z, nw, w_out)


def _ffn_kernel(*refs, n_ctx_tiles, final):
    if n_ctx_tiles:
        xc_ref, xl_ref, y_ref, mod_ref, nw_ref, w1_ref, w3_ref, w2_ref, fw_ref, oc_ref, ol_ref = refs
    else:
        xl_ref, y_ref, mod_ref, nw_ref, w1_ref, w3_ref, w2_ref, fw_ref, ol_ref = refs
    d = xl_ref.shape[1]
    is_ctx = pl.program_id(0) < n_ctx_tiles
    x = jnp.where(is_ctx, xc_ref[...], xl_ref[...]) if n_ctx_tiles else xl_ref[...]
    x = x + _mod_rows(mod_ref, is_ctx, 2, d) * y_ref[...]
    h = _norm_mod(x, nw_ref[...], _mod_rows(mod_ref, is_ctx, 3, d), _mod_rows(mod_ref, is_ctx, 4, d)).astype(BF16)
    a = jnp.dot(h, w1_ref[...], preferred_element_type=F32)
    b = jnp.dot(h, w3_ref[...], preferred_element_type=F32)
    out = x + _mod_rows(mod_ref, is_ctx, 5, d) * _bdot(_silu(a) * b, w2_ref[...])
    if final:
        ms = jnp.mean(out * out, axis=-1, keepdims=True)
        out = out * lax.rsqrt(ms + EPS) * fw_ref[...]
    if n_ctx_tiles:
        @pl.when(is_ctx)
        def _():
            oc_ref[...] = out

        @pl.when(jnp.logical_not(is_ctx))
        def _():
            ol_ref[...] = out
    else:
        ol_ref[...] = out


def _ffn(x_ctx, x_lat, y_all, mod, nw, w1, w3, w2, final_w, final):
    d = x_lat.shape[1]
    n_ctx_tiles = 0 if x_ctx is None else x_ctx.shape[0] // ROW_TILE
    n_tiles = n_ctx_tiles + x_lat.shape[0] // ROW_TILE
    row = pl.BlockSpec((ROW_TILE, d), lambda i: (i, 0))
    full = lambda a: pl.BlockSpec(a.shape, lambda i: (0,) * a.ndim)
    nw = nw.reshape(1, d)
    fw = final_w.reshape(1, d)
    params = [mod, nw, w1, w3, w2, fw]
    if n_ctx_tiles:
        x_specs, x_args = list(_ctx_lat_specs(n_ctx_tiles, d)), [x_ctx, x_lat]
        out_specs = list(_ctx_lat_specs(n_ctx_tiles, d))
        out_shape = [jax.ShapeDtypeStruct(x_ctx.shape, F32), jax.ShapeDtypeStruct(x_lat.shape, F32)]
    else:
        x_specs, x_args = [row], [x_lat]
        out_specs, out_shape = [row], [jax.ShapeDtypeStruct(x_lat.shape, F32)]
    outs = pl.pallas_call(
        functools.partial(_ffn_kernel, n_ctx_tiles=n_ctx_tiles, final=final),
        grid=(n_tiles,),
        in_specs=x_specs + [row] + [full(a) for a in params],
        out_specs=out_specs,
        out_shape=out_shape,
        compiler_params=_cparams(("arbitrary",)),
        name="ffn_final" if final else "ffn",
    )(*x_args, y_all, *params)
    return (outs[0], outs[1]) if n_ctx_tiles else (None, outs[0])


def _pad_cols(parts, width):
    w = jnp.concatenate(parts, axis=1)
    return jnp.pad(w, ((0, 0), (0, width - w.shape[1])))


def _even_w_in(w):
    inner = SSD_HEADS * SSD_HEAD_DIM
    conv = inner + 2 * SSD_GROUPS * SSD_STATE
    gkey = GLA_HEADS * GLA_DK
    gval = GLA_HEADS * GLA_DV
    sizes = (inner, conv, SMALL, SMALL, gkey, gkey, gval, gval, SMALL, SMALL)
    offs = [0]
    for s in sizes:
        offs.append(offs[-1] + s)
    z, xbc, dtf, dtb, q, k, v, r, glf, glb = (w[:, offs[i]:offs[i + 1]] for i in range(len(sizes)))
    small = _pad_cols([dtf, dtb, glf, glb], LANES)
    widths = (inner, conv, gkey, gkey, gval, gval, LANES)
    dtypes = (BF16, F32, F32, F32, F32, BF16, F32)
    return jnp.concatenate([z, xbc, q, k, v, r, small], axis=1).astype(BF16), widths, dtypes


def _odd_w_in(w):
    kdim = GDN_QK_HEADS * GDN_DK
    vdim = GDN_V_HEADS * GDN_DV
    conv = 2 * kdim + vdim
    small = _pad_cols([w[:, conv + vdim:]], LANES)
    widths = (conv, vdim, LANES)
    dtypes = (F32, BF16, F32)
    return jnp.concatenate([w[:, :conv + vdim], small], axis=1).astype(BF16), widths, dtypes


def _to_column_major(h):
    t, d = h.shape
    return h.reshape(t // GRID_W, GRID_W, d).transpose(1, 0, 2).reshape(t, d)


def _from_column_major(h):
    t, d = h.shape
    return h.reshape(GRID_W, t // GRID_W, d).transpose(1, 0, 2).reshape(t, d)


def kernel(x, c, ctx, c_ctx, mod_w, mod_b, norm_mix, norm_ffn, ffn_w1, ffn_w3, ffn_w2, ev_w_in, ssd_conv_w, ssd_conv_b, ssd_dt_bias, ssd_a_log, ssd_d, ssd_norm, gla_gate_w, gla_gate_b, gla_norm, ev_w_out, od_w_in, gdn_conv_w, gdn_dt_bias, gdn_a_log, gdn_norm, od_w_out, final_norm):
    batch, seq, d = x.shape
    n_ctx = ctx.shape[1]
    assert batch == 1 and n_ctx == ROW_TILE and seq % ROW_TILE == 0 and seq % GRID_W == 0
    assert mod_w.shape[0] == 2, "one even (SSD + GLA) layer followed by one odd (gated DeltaNet) layer"
    n_ctx_tiles = n_ctx // ROW_TILE
    n_ctx_chunks = n_ctx // CHUNK

    mods = _modulation(c, c_ctx, mod_w, mod_b)
    x_ctx, x_lat = ctx[0], x[0]

    w_in, widths, dtypes = _even_w_in(ev_w_in[0])
    z, xbc, q, k, v, r, small = _norm_proj(x_ctx, x_lat, norm_mix[0], mods[0], w_in, widths, dtypes)
    xs, bm, cm, dt, da, gk_f, gk_b = _even_prep(xbc, small, ssd_conv_w[0], ssd_conv_b[0], ssd_dt_bias[0],
                                                ssd_a_log[0], gla_gate_w[0], gla_gate_b[0])
    y_f, y_b = _ssd_scan(xs, bm, cm, dt, da, n_ctx_chunks)
    o_f, o_b = _gla_scan(q, k, v, gk_f, gk_b, n_ctx_chunks)
    mixed = _even_merge(y_f, y_b, xs, z, o_f, o_b, r, ssd_d[0], ssd_norm[0], gla_norm[0], ev_w_out[0].astype(BF16))
    x_ctx, x_lat = _ffn(x_ctx, x_lat, mixed, mods[0], norm_ffn[0], ffn_w1[0].astype(BF16), ffn_w3[0].astype(BF16),
                        ffn_w2[0].astype(BF16), final_norm, False)

    w_in, widths, dtypes = _odd_w_in(od_w_in[0])
    qkv, z, small = _norm_proj(x_ctx, _to_column_major(x_lat), norm_mix[1], mods[1], w_in, widths, dtypes)
    q, k, v, g, beta = _odd_prep(qkv, small, gdn_conv_w[0], gdn_dt_bias[0], gdn_a_log[0])
    o_f, o_b = _gdn_scan(q, k, v, g, beta, n_ctx_chunks)
    mixed = _odd_merge(o_f, o_b, z, gdn_norm[0], od_w_out[0].astype(BF16), n_ctx_tiles)
    _, out = _ffn(None, x_lat, _from_column_major(mixed), mods[1], norm_ffn[1], ffn_w1[1].astype(BF16),
                  ffn_w3[1].astype(BF16), ffn_w2[1].astype(BF16), final_norm, True)
    return out[None]
```
